```python
import jax, jax.numpy as jnp
from jax import lax
import numpy as np

D_MODEL = 1024
BATCH = 8
SEQ = 4096
DEPTH = 1
DEC_BATCH = 32
DEC_SEQ = 64
PAST_LEN = 4096

CHUNK = 64
N_META = 16
D_MIX = D_MODEL
D_LRU = D_MIX // 2
LRU_BLOCKS = 8
LRU_BLOCK_W = D_LRU // LRU_BLOCKS
LRU_C = 8.0
CONV_W = 4
GDN_HEADS = 4
GDN_DK = (D_MIX - D_LRU) // GDN_HEADS
GDN_DV = GDN_DK
GDN_QKV = GDN_HEADS * (2 * GDN_DK + GDN_DV)
D_IN = 2 * D_LRU + GDN_QKV + GDN_HEADS * GDN_DV + 2 * GDN_HEADS
SPLITS = [D_LRU, 2 * D_LRU, 2 * D_LRU + GDN_QKV,
          2 * D_LRU + GDN_QKV + GDN_HEADS * GDN_DV,
          2 * D_LRU + GDN_QKV + GDN_HEADS * GDN_DV + GDN_HEADS]
N_EXPERTS = 32
TOP_K = 4
D_FF = D_MODEL
SWIGLU_LIMIT = 7.0
SWIGLU_ALPHA = 1.702
EXPERT_BLOCK = 128
EPS = 1e-6

kernel_name = 'hymba_rglru_gdn_moe_stream_step'


def rmsnorm(x, gain):
    xf = x.astype(jnp.float32)
    y = xf * lax.rsqrt(jnp.mean(xf * xf, axis=-1, keepdims=True) + EPS)
    return (y * gain.astype(jnp.float32)).astype(x.dtype)


def l2norm(x):
    return x * lax.rsqrt(jnp.sum(x * x, axis=-1, keepdims=True) + EPS)


def causal_conv(x, w, prev):
    T = x.shape[1]
    xp = jnp.concatenate([prev.astype(x.dtype), x], axis=1)
    y = xp[:, 0:T] * w[0]
    for j in range(1, CONV_W):
        y = y + xp[:, j:j + T] * w[j]
    return y, xp[:, -(CONV_W - 1):]


def _lin_combine(c1, c2):
    a1, b1 = c1
    a2, b2 = c2
    return a1 * a2, a2 * b1 + b2


def rg_lru(x, h0, w_a, b_a, w_x, b_x, lam, reset_first):
    B, T, _ = x.shape
    xf = x.astype(jnp.float32)
    xb = xf.reshape(B, T, LRU_BLOCKS, LRU_BLOCK_W)
    r = jax.nn.sigmoid(jnp.einsum('btnc,ncd->btnd', xb, w_a.astype(jnp.float32)).reshape(B, T, D_LRU) + b_a.astype(jnp.float32))
    i = jax.nn.sigmoid(jnp.einsum('btnc,ncd->btnd', xb, w_x.astype(jnp.float32)).reshape(B, T, D_LRU) + b_x.astype(jnp.float32))
    log_a = -LRU_C * r * jax.nn.softplus(-lam.astype(jnp.float32))
    a = jnp.exp(log_a)
    mult = jnp.sqrt(-jnp.expm1(2.0 * log_a))
    if reset_first:
        mult = mult.at[:, 0].set(1.0)
    b = mult * i * xf
    b = b.at[:, 0].add(a[:, 0] * h0.astype(jnp.float32))
    _, h = lax.associative_scan(_lin_combine, (a, b), axis=1)
    return h, h[:, -1]


def gated_delta_rule(q, k, v, g, beta, S0):
    B, T, H, DK = q.shape
    DV = v.shape[-1]
    pad = (-T) % CHUNK
    def padT(t):
        return jnp.pad(t, [(0, 0), (pad, 0)] + [(0, 0)] * (t.ndim - 2))
    q, k, v, g, beta = padT(q), padT(k), padT(v), padT(g), padT(beta)
    N = (T + pad) // CHUNK
    def chunks(t):
        return jnp.moveaxis(t.reshape((B, N, CHUNK, H) + t.shape[3:]), (1, 3), (0, 2))
    qc, kc, vc, bc = chunks(q), chunks(k), chunks(v), chunks(beta)
    gc = jnp.cumsum(chunks(g), axis=-1)
    idx = jnp.arange(CHUNK)
    causal = idx[:, None] >= idx[None, :]
    strict = idx[:, None] > idx[None, :]
    decay = jnp.exp(jnp.where(causal, gc[..., :, None] - gc[..., None, :], -jnp.inf))
    kb = kc * bc[..., None]
    lower = jnp.where(strict, jnp.einsum('nbhik,nbhjk->nbhij', kb, kc) * decay, 0.0)
    a_mat = lower + jnp.eye(CHUNK, dtype=lower.dtype)
    rhs = jnp.concatenate([vc * bc[..., None], kb * jnp.exp(gc)[..., None]], axis=-1)
    sol = lax.linalg.triangular_solve(a_mat, rhs, left_side=True, lower=True, unit_diagonal=True)
    u, w = sol[..., :DV], sol[..., DV:]
    attn = jnp.einsum('nbhik,nbhjk->nbhij', qc, kc) * decay
    q_dec = qc * jnp.exp(gc)[..., None]
    k_dec = kc * jnp.exp(gc[..., -1:] - gc)[..., None]
    g_last = jnp.exp(gc[..., -1])

    def step(S, inp):
        u_i, w_i, attn_i, q_i, k_i, gl_i = inp
        v_new = u_i - jnp.einsum('bhck,bhkv->bhcv', w_i, S)
        o = jnp.einsum('bhck,bhkv->bhcv', q_i, S) + jnp.einsum('bhij,bhjv->bhiv', attn_i, v_new)
        S = S * gl_i[..., None, None] + jnp.einsum('bhck,bhcv->bhkv', k_i, v_new)
        return S, o

    S_last, o = lax.scan(step, S0, (u, w, attn, q_dec, k_dec, g_last))
    o = jnp.moveaxis(o, (0, 2), (1, 3)).reshape(B, N * CHUNK, H, DV)[:, pad:]
    return o, S_last


def mixer(xn, lru_conv0, lru_h0, gdn_conv0, gdn_S0, reset_first, p):
    B, T, _ = xn.shape
    dt = xn.dtype
    proj = xn @ p['w_in']
    lru_x, lru_y, qkv, z, a_in, b_in = jnp.split(proj, SPLITS, axis=-1)
    lru_x, lru_conv1 = causal_conv(lru_x, p['lru_conv_w'], lru_conv0)
    lru_x = lru_x + p['lru_conv_b']
    h, lru_h1 = rg_lru(lru_x, lru_h0, p['lru_w_a'], p['lru_b_a'], p['lru_w_x'], p['lru_b_x'], p['lru_lambda'], reset_first)
    lru_out = rmsnorm(h * jax.nn.gelu(lru_y.astype(jnp.float32)), p['lru_norm'])
    qkv, gdn_conv1 = causal_conv(qkv, p['gdn_conv_w'], gdn_conv0)
    qkv = jax.nn.silu(qkv.astype(jnp.float32))
    q, k, v = jnp.split(qkv, [GDN_HEADS * GDN_DK, 2 * GDN_HEADS * GDN_DK], axis=-1)
    q = l2norm(q.reshape(B, T, GDN_HEADS, GDN_DK)) * (GDN_DK ** -0.5)
    k = l2norm(k.reshape(B, T, GDN_HEADS, GDN_DK))
    v = v.reshape(B, T, GDN_HEADS, GDN_DV)
    g = -jnp.exp(p['gdn_A_log'].astype(jnp.float32)) * jax.nn.softplus(a_in.astype(jnp.float32) + p['gdn_dt_bias'].astype(jnp.float32))
    beta = jax.nn.sigmoid(b_in.astype(jnp.float32))
    o, S1 = gated_delta_rule(q, k, v, g, beta, gdn_S0.astype(jnp.float32))
    o = rmsnorm(o, p['gdn_norm']) * jax.nn.silu(z.astype(jnp.float32).reshape(B, T, GDN_HEADS, GDN_DV))
    mixed = jnp.concatenate([lru_out.astype(dt), o.reshape(B, T, GDN_HEADS * GDN_DV).astype(dt)], axis=-1)
    return mixed @ p['w_out'], (lru_conv1, lru_h1.astype(dt), gdn_conv1, S1.astype(dt))


def moe(x2d, p):
    T, D = x2d.shape
    logits = (x2d @ p['router_w']).astype(jnp.float32) + p['router_b'].astype(jnp.float32)
    top_val, top_idx = lax.top_k(logits, TOP_K)
    probs = jax.nn.softmax(top_val, axis=-1).astype(x2d.dtype)
    M = T * TOP_K
    e_flat = top_idx.reshape(-1)
    tok_flat = jnp.repeat(jnp.arange(T, dtype=jnp.int32), TOP_K)
    p_flat = probs.reshape(-1)
    order = jnp.argsort(e_flat)
    e_sorted = e_flat[order]
    counts = jnp.bincount(e_flat, length=N_EXPERTS)
    padded = (counts + EXPERT_BLOCK - 1) // EXPERT_BLOCK * EXPERT_BLOCK
    starts = jnp.cumsum(counts) - counts
    pends = jnp.cumsum(padded)
    pstarts = pends - padded
    dest = pstarts[e_sorted] + jnp.arange(M) - starts[e_sorted]
    n_blocks = (M + N_EXPERTS * (EXPERT_BLOCK - 1) + EXPERT_BLOCK - 1) // EXPERT_BLOCK
    P = n_blocks * EXPERT_BLOCK
    tok_buf = jnp.zeros((P,), jnp.int32).at[dest].set(tok_flat[order])
    p_buf = jnp.zeros((P,), x2d.dtype).at[dest].set(p_flat[order])
    block_expert = jnp.clip(jnp.searchsorted(pends, jnp.arange(n_blocks) * EXPERT_BLOCK, side='right'), 0, N_EXPERTS - 1)

    def expert_block(args):
        tok_blk, p_blk, e = args
        xb = x2d[tok_blk]
        gt = jnp.minimum(xb @ p['exp_w_gate'][e] + p['exp_b_gate'][e], SWIGLU_LIMIT)
        up = jnp.clip(xb @ p['exp_w_up'][e] + p['exp_b_up'][e], -SWIGLU_LIMIT, SWIGLU_LIMIT)
        hid = (up + 1.0) * gt * jax.nn.sigmoid(SWIGLU_ALPHA * gt)
        return (hid @ p['exp_w_down'][e] + p['exp_b_down'][e]) * p_blk[:, None]

    yb = lax.map(expert_block, (tok_buf.reshape(n_blocks, EXPERT_BLOCK), p_buf.reshape(n_blocks, EXPERT_BLOCK), block_expert))
    return jax.ops.segment_sum(yb.reshape(P, D), tok_buf, num_segments=T)


def trunk(x, states, params, reset_first):
    new = []
    for l in range(DEPTH):
        lp = {name: arr[l] for name, arr in params.items()}
        st = [s[l] for s in states]
        m, ns = mixer(rmsnorm(x, lp['norm_mix']), st[0], st[1], st[2], st[3], reset_first, lp)
        x = x + m
        B, T, D = x.shape
        x = x + moe(rmsnorm(x, lp['norm_ffn']).reshape(B * T, D), lp).reshape(B, T, D)
        new.append(ns)
    return x, [jnp.stack([n[i] for n in new]) for i in range(4)]


def setup_inputs(seed: int = 0) -> dict:
    key = jax.random.key(seed)
    ks = jax.random.split(key, 40)
    f32 = jnp.float32
    nrm = lambda i, shape, s: jax.random.normal(ks[i], shape, f32) * s
    u = jax.random.uniform(ks[14], (DEPTH, D_LRU), f32, 0.9, 0.999)
    base = u ** (1.0 / LRU_C)
    dtv = jnp.exp(jax.random.uniform(ks[18], (DEPTH, GDN_HEADS), f32, np.log(1e-3), np.log(1e-1)))
    return {
        'x_prompt': nrm(0, (BATCH, SEQ, D_MODEL), 1.0),
        'x_sample': nrm(1, (DEC_BATCH, DEC_SEQ, D_MODEL), 1.0),
        'state_lru_conv': nrm(2, (DEPTH, DEC_BATCH, CONV_W - 1, D_LRU), 0.5),
        'state_lru_h': nrm(3, (DEPTH, DEC_BATCH, D_LRU), 0.5),
        'state_gdn_conv': nrm(4, (DEPTH, DEC_BATCH, CONV_W - 1, GDN_QKV), 0.5),
        'state_gdn_S': nrm(5, (DEPTH, DEC_BATCH, GDN_HEADS, GDN_DK, GDN_DV), 0.3),
        'meta_tokens': nrm(6, (N_META, D_MODEL), 1.0),
        'norm_mix': 1.0 + nrm(7, (DEPTH, D_MODEL), 0.02),
        'w_in': nrm(8, (DEPTH, D_MODEL, D_IN), D_MODEL ** -0.5),
        'lru_conv_w': nrm(9, (DEPTH, CONV_W, D_LRU), CONV_W ** -0.5),
        'lru_conv_b': nrm(10, (DEPTH, D_LRU), 0.02),
        'lru_w_a': nrm(11, (DEPTH, LRU_BLOCKS, LRU_BLOCK_W, LRU_BLOCK_W), LRU_BLOCK_W ** -0.5),
        'lru_b_a': nrm(12, (DEPTH, D_LRU), 0.1),
        'lru_w_x': nrm(13, (DEPTH, LRU_BLOCKS, LRU_BLOCK_W, LRU_BLOCK_W), LRU_BLOCK_W ** -0.5),
        'lru_b_x': nrm(15, (DEPTH, D_LRU), 0.1),
        'lru_lambda': jnp.log(base / (1.0 - base)),
        'lru_norm': 1.0 + nrm(16, (DEPTH, D_LRU), 0.02),
        'gdn_conv_w': nrm(17, (DEPTH, CONV_W, GDN_QKV), CONV_W ** -0.5),
        'gdn_A_log': jnp.log(jax.random.uniform(ks[19], (DEPTH, GDN_HEADS), f32, 1.0, 16.0)),
        'gdn_dt_bias': dtv + jnp.log(-jnp.expm1(-dtv)),
        'gdn_norm': 1.0 + nrm(20, (DEPTH, GDN_DV), 0.02),
        'w_out': nrm(21, (DEPTH, D_MIX, D_MODEL), D_MIX ** -0.5),
        'norm_ffn': 1.0 + nrm(22, (DEPTH, D_MODEL), 0.02),
        'router_w': nrm(23, (DEPTH, D_MODEL, N_EXPERTS), D_MODEL ** -0.5),
        'router_b': nrm(24, (DEPTH, N_EXPERTS), 0.01),
        'exp_w_gate': nrm(25, (DEPTH, N_EXPERTS, D_MODEL, D_FF), D_MODEL ** -0.5),
        'exp_b_gate': nrm(26, (DEPTH, N_EXPERTS, D_FF), 0.02),
        'exp_w_up': nrm(27, (DEPTH, N_EXPERTS, D_MODEL, D_FF), D_MODEL ** -0.5),
        'exp_b_up': nrm(28, (DEPTH, N_EXPERTS, D_FF), 0.02),
        'exp_w_down': nrm(29, (DEPTH, N_EXPERTS, D_FF, D_MODEL), D_FF ** -0.5),
        'exp_b_down': nrm(30, (DEPTH, N_EXPERTS, D_MODEL), 0.02),
        'final_norm': 1.0 + nrm(31, (D_MODEL,), 0.02),
    }


def reference(x_prompt, x_sample, state_lru_conv, state_lru_h, state_gdn_conv, state_gdn_S, meta_tokens,
              norm_mix, w_in, lru_conv_w, lru_conv_b, lru_w_a, lru_b_a, lru_w_x, lru_b_x, lru_lambda, lru_norm,
              gdn_conv_w, gdn_A_log, gdn_dt_bias, gdn_norm, w_out, norm_ffn, router_w, router_b,
              exp_w_gate, exp_b_gate, exp_w_up, exp_b_up, exp_w_down, exp_b_down, final_norm):
    params = dict(norm_mix=norm_mix, w_in=w_in, lru_conv_w=lru_conv_w, lru_conv_b=lru_conv_b,
                  lru_w_a=lru_w_a, lru_b_a=lru_b_a, lru_w_x=lru_w_x, lru_b_x=lru_b_x,
                  lru_lambda=lru_lambda, lru_norm=lru_norm, gdn_conv_w=gdn_conv_w, gdn_A_log=gdn_A_log,
                  gdn_dt_bias=gdn_dt_bias, gdn_norm=gdn_norm, w_out=w_out, norm_ffn=norm_ffn,
                  router_w=router_w, router_b=router_b, exp_w_gate=exp_w_gate, exp_b_gate=exp_b_gate,
                  exp_w_up=exp_w_up, exp_b_up=exp_b_up, exp_w_down=exp_w_down, exp_b_down=exp_b_down)
    dt = x_prompt.dtype
    Bp = x_prompt.shape[0]
    xp = jnp.concatenate([jnp.broadcast_to(meta_tokens.astype(dt)[None], (Bp, N_META, D_MODEL)), x_prompt], axis=1)
    zero_states = [jnp.zeros((DEPTH, Bp, CONV_W - 1, D_LRU), dt), jnp.zeros((DEPTH, Bp, D_LRU), dt),
                   jnp.zeros((DEPTH, Bp, CONV_W - 1, GDN_QKV), dt),
                   jnp.zeros((DEPTH, Bp, GDN_HEADS, GDN_DK, GDN_DV), dt)]
    hp, p_new = trunk(xp, zero_states, params, True)
    y_prompt = rmsnorm(hp, final_norm)[:, N_META:]
    hs, s_new = trunk(x_sample, [state_lru_conv, state_lru_h, state_gdn_conv, state_gdn_S], params, False)
    y_sample = rmsnorm(hs, final_norm)
    return (y_prompt, y_sample, p_new[0], p_new[1], p_new[2], p_new[3], s_new[0], s_new[1], s_new[2], s_new[3])
```

```python
import functools

import jax
import jax.numpy as jnp
from jax import lax
from jax.experimental import pallas as pl
from jax.experimental.pallas import tpu as pltpu

F32 = jnp.float32
BF16 = jnp.bfloat16

D_MODEL = 1024
D_LRU = 512
LRU_BLOCKS = 8
LRU_C = 8.0
CONV_W = 4
GDN_HEADS = 4
GDN_DK = 128
GDN_DV = 128
GDN_QKV = GDN_HEADS * (2 * GDN_DK + GDN_DV)
D_IN = 2 * D_LRU + GDN_QKV + GDN_HEADS * GDN_DV + 2 * GDN_HEADS
N_EXPERTS = 32
TOP_K = 4
SWIGLU_LIMIT = 7.0
SWIGLU_ALPHA = 1.702
EPS = 1e-6

LANES = 128
SUBLANES = 8
D_IN_PAD = 3200
COL_AB = (2 * D_LRU + GDN_QKV + GDN_HEADS * GDN_DV) // LANES
VMEM_LIMIT = 56 * 1024 * 1024


def _cparams(sem):
    return pltpu.CompilerParams(dimension_semantics=sem, vmem_limit_bytes=VMEM_LIMIT)


def _rms(x, gain):
    return x * lax.rsqrt(jnp.mean(x * x, axis=-1, keepdims=True) + EPS) * gain


def _softplus(x):
    return jnp.maximum(x, 0.0) + jnp.log1p(jnp.exp(-jnp.abs(x)))


def _sigmoid(x):
    return 1.0 / (1.0 + jnp.exp(-x))


def _mm(a, b):
    return jnp.dot(a.astype(BF16), b.astype(BF16), preferred_element_type=F32)


def _mm_nt(a, b):
    return lax.dot_general(a.astype(BF16), b.astype(BF16), (((1,), (1,)), ((), ())), preferred_element_type=F32)


def _mm_tn(a, b):
    return lax.dot_general(a.astype(BF16), b.astype(BF16), (((0,), (0,)), ((), ())), preferred_element_type=F32)


def _inproj_kernel(x_ref, g_ref, w_ref, o_ref):
    xn = _rms(x_ref[...], g_ref[...])
    o_ref[...] = jnp.dot(xn.astype(BF16), w_ref[...], preferred_element_type=F32)


def _inproj(x2d, gain, w_pad, tm):
    n = x2d.shape[0]
    return pl.pallas_call(
        _inproj_kernel,
        grid=(n // tm,),
        in_specs=[pl.BlockSpec((tm, D_MODEL), lambda i: (i, 0)),
                  pl.BlockSpec((1, D_MODEL), lambda i: (0, 0)),
                  pl.BlockSpec((D_MODEL, D_IN_PAD), lambda i: (0, 0))],
        out_specs=pl.BlockSpec((tm, D_IN_PAD), lambda i: (i, 0)),
        out_shape=jax.ShapeDtypeStruct((n, D_IN_PAD), F32),
        compiler_params=_cparams(("arbitrary",)),
        name="inproj",
    )(x2d, gain, w_pad)


def _conv_step(buf, x, w_ref, tt):
    buf[SUBLANES:SUBLANES + tt, :] = x
    y = buf[SUBLANES - 3:SUBLANES - 3 + tt, :] * w_ref[0:1, :]
    for j in range(1, CONV_W):
        y = y + buf[SUBLANES - 3 + j:SUBLANES - 3 + j + tt, :] * w_ref[j:j + 1, :]
    tail = buf[tt:tt + SUBLANES, :]
    buf[0:SUBLANES, :] = tail
    return y


def _lru_kernel(xl_ref, yl_ref, conv0_ref, h0_ref, cw_ref, cb_ref, wa_ref, ba_ref, wx_ref, bx_ref, lam_ref, nrm_ref,
                out_ref, conv1_ref, h1_ref, cbuf, a_buf, b_buf, hcar, *, tt, reset_first):
    t = pl.program_id(1)

    @pl.when(t == 0)
    def _():
        cbuf[0:SUBLANES, :] = conv0_ref[0]
        hcar[...] = h0_ref[0]

    xc = _conv_step(cbuf, xl_ref[...], cw_ref, tt) + cb_ref[...]
    xb = xc.astype(BF16)
    r = _sigmoid(jnp.dot(xb, wa_ref[...], preferred_element_type=F32) + ba_ref[...])
    i = _sigmoid(jnp.dot(xb, wx_ref[...], preferred_element_type=F32) + bx_ref[...])
    log_a = (-LRU_C) * r * _softplus(-lam_ref[...])
    a = jnp.exp(log_a)
    mult = jnp.sqrt(-jnp.tanh(log_a) * (a * a + 1.0))
    if reset_first:
        row = lax.broadcasted_iota(jnp.int32, (tt, D_LRU), 0)
        mult = jnp.where((row == 0) & (t == 0), 1.0, mult)
    a_buf[...] = a
    b_buf[...] = mult * i * xc

    row8 = lax.broadcasted_iota(jnp.int32, (SUBLANES, D_LRU), 0)

    def group(gi, h):
        r0 = pl.multiple_of(gi * SUBLANES, SUBLANES)
        a8 = a_buf[pl.ds(r0, SUBLANES), :]
        b8 = b_buf[pl.ds(r0, SUBLANES), :]
        for d in (1, 2, 4):
            keep = row8 >= d
            b8 = jnp.where(keep, a8 * pltpu.roll(b8, d, 0) + b8, b8)
            a8 = jnp.where(keep, a8 * pltpu.roll(a8, d, 0), a8)
        h8 = a8 * h + b8
        b_buf[pl.ds(r0, SUBLANES), :] = h8
        return h8[SUBLANES - 1:SUBLANES, :]

    h_last = lax.fori_loop(0, tt // SUBLANES, group, hcar[...])
    hcar[...] = h_last
    hh = b_buf[...]
    out_ref[...] = _rms(hh * jax.nn.gelu(yl_ref[...]), nrm_ref[...]).astype(out_ref.dtype)

    @pl.when(t == pl.num_programs(1) - 1)
    def _():
        conv1_ref[0] = cbuf[0:SUBLANES, :]
        h1_ref[0] = h_last


def _lru(proj, conv0, h0, shared_init, n_streams, t_len, tt, reset_first, p):
    nt = t_len // tt
    st = (lambda s, t: (0, 0, 0)) if shared_init else (lambda s, t: (s, 0, 0))
    vec = lambda: pl.BlockSpec((1, D_LRU), lambda s, t: (0, 0))
    n = n_streams * t_len
    return pl.pallas_call(
        functools.partial(_lru_kernel, tt=tt, reset_first=reset_first),
        grid=(n_streams, nt),
        in_specs=[pl.BlockSpec((tt, D_LRU), lambda s, t: (s * nt + t, 0)),
                  pl.BlockSpec((tt, D_LRU), lambda s, t: (s * nt + t, 1)),
                  pl.BlockSpec((1, SUBLANES, D_LRU), st),
                  pl.BlockSpec((1, 1, D_LRU), st),
                  pl.BlockSpec((CONV_W, D_LRU), lambda s, t: (0, 0)),
                  vec(),
                  pl.BlockSpec((D_LRU, D_LRU), lambda s, t: (0, 0)),
                  vec(),
                  pl.BlockSpec((D_LRU, D_LRU), lambda s, t: (0, 0)),
                  vec(), vec(), vec()],
        out_specs=[pl.BlockSpec((tt, D_LRU), lambda s, t: (s * nt + t, 0)),
                   pl.BlockSpec((1, SUBLANES, D_LRU), lambda s, t: (s, 0, 0)),
                   pl.BlockSpec((1, 1, D_LRU), lambda s, t: (s, 0, 0))],
        out_shape=[jax.ShapeDtypeStruct((n, D_LRU), BF16),
                   jax.ShapeDtypeStruct((n_streams, SUBLANES, D_LRU), F32),
                   jax.ShapeDtypeStruct((n_streams, 1, D_LRU), F32)],
        scratch_shapes=[pltpu.VMEM((tt + SUBLANES, D_LRU), F32),
                        pltpu.VMEM((tt, D_LRU), F32),
                        pltpu.VMEM((tt, D_LRU), F32),
                        pltpu.VMEM((1, D_LRU), F32)],
        compiler_params=_cparams(("arbitrary", "arbitrary")),
        name="rglru",
    )(proj, proj, conv0, h0, p["lru_conv_w"], p["lru_conv_b"], p["lru_wa_bd"], p["lru_b_a"], p["lru_wx_bd"],
      p["lru_b_x"], p["lru_lambda"], p["lru_norm"])


def _gdn_kernel(q_ref, k_ref, v_ref, z_ref, ab_ref, conv0_ref, s0_ref, cw_ref, par_ref, nrm_ref,
                o_ref, conv1_ref, s1_ref, cbuf, s_scr, *, c):
    t = pl.program_id(1)
    hd = GDN_HEADS * GDN_DK

    @pl.when(t == 0)
    def _():
        for j in range(3):
            cbuf[j, 0:SUBLANES, :] = conv0_ref[0, :, j * hd:(j + 1) * hd]
        s_scr[...] = s0_ref[0]

    def conv_silu(j, ref):
        y = _conv_step(cbuf.at[j], ref[...], cw_ref.at[:, j * hd:(j + 1) * hd], c)
        return y * _sigmoid(y)

    qa = conv_silu(0, q_ref)
    ka = conv_silu(1, k_ref)
    va = conv_silu(2, v_ref)

    ab = ab_ref[...]
    g_all = -jnp.exp(par_ref[0:1, :]) * _softplus(ab + par_ref[1:2, :])
    beta_all = _sigmoid(ab)
    row = lax.broadcasted_iota(jnp.int32, (c, LANES), 0)
    gc = g_all
    d = 1
    while d < c:
        gc = gc + jnp.where(row >= d, pltpu.roll(gc, d, 0), 0.0)
        d *= 2

    ri = lax.broadcasted_iota(jnp.int32, (c, c), 0)
    ci = lax.broadcasted_iota(jnp.int32, (c, c), 1)
    causal = ri >= ci
    strict = ri > ci
    eye = ri == ci

    for h in range(GDN_HEADS):
        sl = slice(h * GDN_DK, (h + 1) * GDN_DK)
        q = qa[:, sl]
        k = ka[:, sl]
        v = va[:, sl]
        q = q * lax.rsqrt(jnp.sum(q * q, axis=-1, keepdims=True) + EPS) * (GDN_DK ** -0.5)
        k = k * lax.rsqrt(jnp.sum(k * k, axis=-1, keepdims=True) + EPS)
        gcol = gc[:, h:h + 1]
        bcol = beta_all[:, GDN_HEADS + h:GDN_HEADS + h + 1]
        grow = jnp.sum(jnp.where(eye, gcol, 0.0), axis=0, keepdims=True)
        decay = jnp.where(causal, jnp.exp(jnp.where(causal, gcol - grow, 0.0)), 0.0)
        egc = jnp.exp(gcol)
        g_last = gc[c - 1:c, h:h + 1]
        kb = k * bcol
        n1 = jnp.where(strict, _mm_nt(kb, k) * decay, 0.0)
        x = jnp.concatenate([v * bcol, kb * egc], axis=1)
        x = x - _mm(n1, x)
        pw = n1
        m = 2
        while m < c:
            pw = _mm(pw, pw)
            x = x + _mm(pw, x)
            m *= 2
        u = x[:, :GDN_DV]
        w = x[:, GDN_DV:]
        attn = _mm_nt(q, k) * decay
        s = s_scr[h]
        v_new = u - _mm(w, s)
        o = _mm(q * egc, s) + _mm(attn, v_new)
        s_scr[h] = s * jnp.exp(g_last) + _mm_tn(k * jnp.exp(g_last - gcol), v_new)
        z = z_ref[:, sl]
        o_ref[:, sl] = (_rms(o, nrm_ref[...]) * (z * _sigmoid(z))).astype(o_ref.dtype)

    @pl.when(t == pl.num_programs(1) - 1)
    def _():
        for j in range(3):
            conv1_ref[0, :, j * hd:(j + 1) * hd] = cbuf[j, 0:SUBLANES, :]
        s1_ref[0] = s_scr[...]


def _gdn(proj, conv0, s0, shared_init, n_streams, t_len, c, p):
    nt = t_len // c
    hd = GDN_HEADS * GDN_DK
    st3 = (lambda s, t: (0, 0, 0)) if shared_init else (lambda s, t: (s, 0, 0))
    st4 = (lambda s, t: (0, 0, 0, 0)) if shared_init else (lambda s, t: (s, 0, 0, 0))
    n = n_streams * t_len
    col = lambda j: pl.BlockSpec((c, hd), lambda s, t: (s * nt + t, j))
    return pl.pallas_call(
        functools.partial(_gdn_kernel, c=c),
        grid=(n_streams, nt),
        in_specs=[col(2), col(3), col(4), col(5),
                  pl.BlockSpec((c, LANES), lambda s, t: (s * nt + t, COL_AB)),
                  pl.BlockSpec((1, SUBLANES, GDN_QKV), st3),
                  pl.BlockSpec((1, GDN_HEADS, GDN_DK, GDN_DV), st4),
                  pl.BlockSpec((CONV_W, GDN_QKV), lambda s, t: (0, 0)),
                  pl.BlockSpec((2, LANES), lambda s, t: (0, 0)),
                  pl.BlockSpec((1, GDN_DV), lambda s, t: (0, 0))],
        out_specs=[pl.BlockSpec((c, hd), lambda s, t: (s * nt + t, 0)),
                   pl.BlockSpec((1, SUBLANES, GDN_QKV), lambda s, t: (s, 0, 0)),
                   pl.BlockSpec((1, GDN_HEADS, GDN_DK, GDN_DV), lambda s, t: (s, 0, 0, 0))],
        out_shape=[jax.ShapeDtypeStruct((n, hd), BF16),
                   jax.ShapeDtypeStruct((n_streams, SUBLANES, GDN_QKV), F32),
                   jax.ShapeDtypeStruct((n_streams, GDN_HEADS, GDN_DK, GDN_DV), F32)],
        scratch_shapes=[pltpu.VMEM((3, c + SUBLANES, hd), F32),
                        pltpu.VMEM((GDN_HEADS, GDN_DK, GDN_DV), F32)],
        compiler_params=_cparams(("arbitrary", "arbitrary")),
        name="gdn",
    )(proj, proj, proj, proj, proj, conv0, s0, p["gdn_conv_w"], p["gdn_par"], p["gdn_norm"])


def _post_kernel(lru_ref, gdn_ref, x_ref, wo1_ref, wo2_ref, nrm_ref, rw_ref, rb_ref, cnt0_ref,
                 x1_ref, xn2_ref, route_ref, pw_ref, cnt_ref, cnt_scr, *, tt):
    i = pl.program_id(0)

    @pl.when(i == 0)
    def _():
        cnt_scr[...] = cnt0_ref[...]

    m = (jnp.dot(lru_ref[...], wo1_ref[...], preferred_element_type=F32)
         + jnp.dot(gdn_ref[...], wo2_ref[...], preferred_element_type=F32))
    x1 = x_ref[...] + m
    x1_ref[...] = x1
    xn2 = _rms(x1, nrm_ref[...])
    xn2_ref[...] = xn2
    lane = lax.broadcasted_iota(jnp.int32, (tt, LANES), 1)
    logits = jnp.dot(xn2.astype(BF16), rw_ref[...], preferred_element_type=F32) + rb_ref[...]
    logits = jnp.where(lane < N_EXPERTS, logits, -jnp.inf)
    vals, idxs = [], []
    for _ in range(TOP_K):
        mx = jnp.max(logits, axis=-1, keepdims=True)
        ix = jnp.min(jnp.where(logits == mx, lane, LANES), axis=-1, keepdims=True)
        logits = jnp.where(lane == ix, -jnp.inf, logits)
        vals.append(mx)
        idxs.append(ix)
    es = [jnp.exp(v - vals[0]) for v in vals]
    den = es[0] + es[1] + es[2] + es[3]
    onehot = jnp.zeros((tt, LANES), F32)
    for ix in idxs:
        onehot = onehot + (lane == ix).astype(F32)
    ri = lax.broadcasted_iota(jnp.int32, (tt, tt), 0)
    ci = lax.broadcasted_iota(jnp.int32, (tt, tt), 1)
    before = jnp.dot((ri > ci).astype(BF16), onehot.astype(BF16), preferred_element_type=F32) + cnt_scr[...]
    route = jnp.zeros((tt, LANES), jnp.int32)
    pw = jnp.zeros((tt, LANES), F32)
    for kk in range(TOP_K):
        rank = jnp.sum(jnp.where(lane == idxs[kk], before, 0.0), axis=-1, keepdims=True).astype(jnp.int32)
        route = jnp.where(lane == kk, idxs[kk], route)
        route = jnp.where(lane == TOP_K + kk, rank, route)
        pw = jnp.where(lane == kk, es[kk] / den, pw)
    route_ref[...] = route
    pw_ref[...] = pw
    cnt_scr[...] = cnt_scr[...] + jnp.sum(onehot, axis=0, keepdims=True)
    cnt_ref[...] = cnt_scr[...]


def _post(lru_out, gdn_out, x2d, cnt0, tt, p):
    n = x2d.shape[0]
    row = lambda w: pl.BlockSpec((tt, w), lambda i: (i, 0))
    full = lambda a, b: pl.BlockSpec((a, b), lambda i: (0, 0))
    return pl.pallas_call(
        functools.partial(_post_kernel, tt=tt),
        grid=(n // tt,),
        in_specs=[row(D_LRU), row(D_LRU), row(D_MODEL), full(D_LRU, D_MODEL), full(D_LRU, D_MODEL),
                  full(1, D_MODEL), full(D_MODEL, LANES), full(1, LANES), full(1, LANES)],
        out_specs=[row(D_MODEL), row(D_MODEL), row(LANES), row(LANES), full(1, LANES)],
        out_shape=[jax.ShapeDtypeStruct((n, D_MODEL), F32), jax.ShapeDtypeStruct((n, D_MODEL), F32),
                   jax.ShapeDtypeStruct((n, LANES), jnp.int32), jax.ShapeDtypeStruct((n, LANES), F32),
                   jax.ShapeDtypeStruct((1, LANES), F32)],
        scratch_shapes=[pltpu.VMEM((1, LANES), F32)],
        compiler_params=_cparams(("arbitrary",)),
        name="outproj_router",
    )(lru_out, gdn_out, x2d, p["w_out1"], p["w_out2"], p["norm_ffn"], p["router_w"], p["router_b"], cnt0)


def _gather_rows(src_hbm, idx_smem, dst, sem, n_rows):
    def issue(r, carry):
        pltpu.make_async_copy(src_hbm.at[pl.ds(idx_smem[0, 0, r], 1), :], dst.at[pl.ds(r, 1), :], sem).start()
        return carry
    lax.fori_loop(0, n_rows, issue, 0, unroll=8)


def _wait_rows(src_hbm, dst, sem, n_rows):
    pltpu.make_async_copy(src_hbm.at[pl.ds(0, n_rows), :], dst, sem).wait()


def _expert_kernel(be_ref, nu_ref, tok_ref, tokn_ref, x_hbm, wg_ref, bg_ref, wu_ref, bu_ref, wd_ref, bd_ref,
                   o_ref, xbuf, sems, *, blk):
    b = pl.program_id(0)
    n_used = nu_ref[0]
    slot = b % 2

    @pl.when(b == 0)
    def _():
        _gather_rows(x_hbm, tok_ref, xbuf.at[0], sems.at[0], blk)

    @pl.when(b + 1 < n_used)
    def _():
        _gather_rows(x_hbm, tokn_ref, xbuf.at[1 - slot], sems.at[1 - slot], blk)

    @pl.when(b < n_used)
    def _():
        _wait_rows(x_hbm, xbuf.at[slot], sems.at[slot], blk)
        xb = xbuf[slot].astype(BF16)
        gt = jnp.minimum(jnp.dot(xb, wg_ref[0], preferred_element_type=F32) + bg_ref[0], SWIGLU_LIMIT)
        up = jnp.clip(jnp.dot(xb, wu_ref[0], preferred_element_type=F32) + bu_ref[0], -SWIGLU_LIMIT, SWIGLU_LIMIT)
        hid = (up + 1.0) * gt * _sigmoid(SWIGLU_ALPHA * gt)
        o_ref[...] = jnp.dot(hid.astype(BF16), wd_ref[0], preferred_element_type=F32) + bd_ref[0]

    @pl.when(b >= n_used)
    def _():
        o_ref[...] = jnp.zeros_like(o_ref)


def _experts(block_expert, n_used, tok_buf, xn2, p, blk):
    n_blocks = block_expert.shape[0]
    tok3 = tok_buf.reshape(n_blocks, 1, blk)
    wspec = lambda: pl.BlockSpec((1, D_MODEL, D_MODEL), lambda b, be, nu: (be[b], 0, 0))
    bspec = lambda: pl.BlockSpec((1, 1, D_MODEL), lambda b, be, nu: (be[b], 0, 0))
    grid_spec = pltpu.PrefetchScalarGridSpec(
        num_scalar_prefetch=2,
        grid=(n_blocks,),
        in_specs=[pl.BlockSpec((1, 1, blk), lambda b, be, nu: (b, 0, 0), memory_space=pltpu.SMEM),
                  pl.BlockSpec((1, 1, blk), lambda b, be, nu: (jnp.minimum(b + 1, n_blocks - 1), 0, 0),
                               memory_space=pltpu.SMEM),
                  pl.BlockSpec(memory_space=pl.ANY),
                  wspec(), bspec(), wspec(), bspec(), wspec(), bspec()],
        out_specs=pl.BlockSpec((blk, D_MODEL), lambda b, be, nu: (b, 0)),
        scratch_shapes=[pltpu.VMEM((2, blk, D_MODEL), F32), pltpu.SemaphoreType.DMA((2,))],
    )
    return pl.pallas_call(
        functools.partial(_expert_kernel, blk=blk),
        grid_spec=grid_spec,
        out_shape=jax.ShapeDtypeStruct((n_blocks * blk, D_MODEL), F32),
        compiler_params=_cparams(("arbitrary",)),
        name="experts",
    )(block_expert, n_used, tok3, tok3, xn2, p["exp_w_gate"], p["exp_b_gate"], p["exp_w_up"], p["exp_b_up"],
      p["exp_w_down"], p["exp_b_down"])


def _combine_kernel(dst_ref, dstn_ref, y_hbm, x1_ref, pw_ref, nrm_ref, o_ref, ybuf, sems, *, tt):
    i = pl.program_id(0)
    n = pl.num_programs(0)
    slot = i % 2
    rows = TOP_K * tt

    @pl.when(i == 0)
    def _():
        _gather_rows(y_hbm, dst_ref, ybuf.at[0], sems.at[0], rows)

    @pl.when(i + 1 < n)
    def _():
        _gather_rows(y_hbm, dstn_ref, ybuf.at[1 - slot], sems.at[1 - slot], rows)

    _wait_rows(y_hbm, ybuf.at[slot], sems.at[slot], rows)
    acc = x1_ref[...]
    pw = pw_ref[...]
    moe = pw[:, 0:1] * ybuf[slot, 0:tt, :]
    for kk in range(1, TOP_K):
        moe = moe + pw[:, kk:kk + 1] * ybuf[slot, kk * tt:(kk + 1) * tt, :]
    o_ref[...] = _rms(acc + moe, nrm_ref[...])


def _combine(dest, yb, x1, pw, final_norm, tt):
    n = x1.shape[0]
    nt = n // tt
    d3 = dest.reshape(nt, tt, TOP_K).transpose(0, 2, 1).reshape(nt, 1, TOP_K * tt)
    return pl.pallas_call(
        functools.partial(_combine_kernel, tt=tt),
        grid=(nt,),
        in_specs=[pl.BlockSpec((1, 1, TOP_K * tt), lambda i: (i, 0, 0), memory_space=pltpu.SMEM),
                  pl.BlockSpec((1, 1, TOP_K * tt), lambda i: (jnp.minimum(i + 1, nt - 1), 0, 0),
                               memory_space=pltpu.SMEM),
                  pl.BlockSpec(memory_space=pl.ANY),
                  pl.BlockSpec((tt, D_MODEL), lambda i: (i, 0)),
                  pl.BlockSpec((tt, LANES), lambda i: (i, 0)),
                  pl.BlockSpec((1, D_MODEL), lambda i: (0, 0))],
        out_specs=pl.BlockSpec((tt, D_MODEL), lambda i: (i, 0)),
        out_shape=jax.ShapeDtypeStruct((n, D_MODEL), F32),
        scratch_shapes=[pltpu.VMEM((2, TOP_K * tt, D_MODEL), F32), pltpu.SemaphoreType.DMA((2,))],
        compiler_params=_cparams(("arbitrary",)),
        name="combine",
    )(d3, d3, yb, x1, pw, final_norm)


def _pad_rows(a, rows):
    return jnp.pad(a, ((0, 0), (rows - a.shape[1], 0), (0, 0)))


def _tile(n, pref):
    t = pref
    while n % t:
        t //= 2
    return t


def _mixer_group(x2d, n_streams, t_len, lru_conv0, lru_h0, gdn_conv0, gdn_s0, shared_init, reset_first, chunk, p):
    proj = _inproj(x2d, p["norm_mix"], p["w_in"], _tile(x2d.shape[0], 256))
    lru_out, lru_conv1, lru_h1 = _lru(proj, lru_conv0, lru_h0, shared_init, n_streams, t_len, _tile(t_len, 256),
                                      reset_first, p)
    gdn_out, gdn_conv1, gdn_s1 = _gdn(proj, gdn_conv0, gdn_s0, shared_init, n_streams, t_len, chunk, p)
    return lru_out, gdn_out, (lru_conv1, lru_h1, gdn_conv1, gdn_s1)


def kernel(x_prompt, x_sample, state_lru_conv, state_lru_h, state_gdn_conv, state_gdn_S, meta_tokens, norm_mix, w_in, lru_conv_w, lru_conv_b, lru_w_a, lru_b_a, lru_w_x, lru_b_x, lru_lambda, lru_norm, gdn_conv_w, gdn_A_log, gdn_dt_bias, gdn_norm, w_out, norm_ffn, router_w, router_b, exp_w_gate, exp_b_gate, exp_w_up, exp_b_up, exp_w_down, exp_b_down, final_norm):
    bp, tp, _ = x_prompt.shape
    bs, ts, _ = x_sample.shape
    n_meta = meta_tokens.shape[0]
    blk = 256

    def blockdiag(w):
        eye = jnp.eye(LRU_BLOCKS, dtype=w.dtype)
        return jnp.einsum("ncd,nm->ncmd", w, eye).reshape(D_LRU, D_LRU)

    lane_pad = lambda v: jnp.pad(v, (0, LANES - v.shape[0]))[None]
    p = dict(
        norm_mix=norm_mix[0][None],
        w_in=jnp.pad(w_in[0], ((0, 0), (0, D_IN_PAD - D_IN))).astype(BF16),
        lru_conv_w=lru_conv_w[0], lru_conv_b=lru_conv_b[0][None],
        lru_wa_bd=blockdiag(lru_w_a[0]).astype(BF16), lru_b_a=lru_b_a[0][None],
        lru_wx_bd=blockdiag(lru_w_x[0]).astype(BF16), lru_b_x=lru_b_x[0][None],
        lru_lambda=lru_lambda[0][None], lru_norm=lru_norm[0][None],
        gdn_conv_w=gdn_conv_w[0],
        gdn_par=jnp.concatenate([lane_pad(gdn_A_log[0]), lane_pad(gdn_dt_bias[0])], axis=0),
        gdn_norm=gdn_norm[0][None],
        w_out1=w_out[0][:D_LRU].astype(BF16), w_out2=w_out[0][D_LRU:].astype(BF16),
        norm_ffn=norm_ffn[0][None],
        router_w=jnp.pad(router_w[0], ((0, 0), (0, LANES - N_EXPERTS))).astype(BF16),
        router_b=lane_pad(router_b[0]),
        exp_w_gate=exp_w_gate[0].astype(BF16), exp_b_gate=exp_b_gate[0][:, None, :],
        exp_w_up=exp_w_up[0].astype(BF16), exp_b_up=exp_b_up[0][:, None, :],
        exp_w_down=exp_w_down[0].astype(BF16), exp_b_down=exp_b_down[0][:, None, :],
    )

    zc = lambda c: jnp.zeros((1, SUBLANES, c), F32)
    _, _, m_state = _mixer_group(meta_tokens, 1, n_meta, zc(D_LRU), jnp.zeros((1, 1, D_LRU), F32), zc(GDN_QKV),
                                 jnp.zeros((1, GDN_HEADS, GDN_DK, GDN_DV), F32), False, True, n_meta, p)
    xp2 = x_prompt.reshape(bp * tp, D_MODEL)
    xs2 = x_sample.reshape(bs * ts, D_MODEL)
    p_lru, p_gdn, p_state = _mixer_group(xp2, bp, tp, m_state[0], m_state[1], m_state[2], m_state[3], True, False,
                                         64, p)
    s_lru, s_gdn, s_state = _mixer_group(xs2, bs, ts, _pad_rows(state_lru_conv[0], SUBLANES),
                                         state_lru_h[0][:, None, :], _pad_rows(state_gdn_conv[0], SUBLANES),
                                         state_gdn_S[0], False, False, 64, p)

    cnt0 = jnp.zeros((1, LANES), F32)
    x1_p, xn2_p, route_p, pw_p, cnt_p = _post(p_lru, p_gdn, xp2, cnt0, _tile(bp * tp, 256), p)
    x1_s, xn2_s, route_s, pw_s, cnt_s = _post(s_lru, s_gdn, xs2, cnt_p, _tile(bs * ts, 256), p)

    n_tok = bp * tp + bs * ts
    route = jnp.concatenate([route_p[:, :2 * TOP_K], route_s[:, :2 * TOP_K]], axis=0)
    idx, rank = route[:, :TOP_K], route[:, TOP_K:]
    counts = cnt_s[0, :N_EXPERTS].astype(jnp.int32)
    padded = (counts + blk - 1) // blk * blk
    pends = jnp.cumsum(padded)
    pstarts = pends - padded
    dest = pstarts[idx] + rank
    m_pairs = n_tok * TOP_K
    n_blocks = (m_pairs + N_EXPERTS * (blk - 1) + blk - 1) // blk
    tok_flat = jnp.repeat(jnp.arange(n_tok, dtype=jnp.int32), TOP_K)
    tok_buf = jnp.zeros((n_blocks * blk,), jnp.int32).at[dest.reshape(-1)].set(tok_flat)
    block_expert = jnp.clip(jnp.searchsorted(pends, jnp.arange(n_blocks, dtype=jnp.int32) * blk, side="right"),
                            0, N_EXPERTS - 1).astype(jnp.int32)
    n_used = (pends[-1] // blk).astype(jnp.int32)[None]

    xn2 = jnp.concatenate([xn2_p, xn2_s], axis=0)
    yb = _experts(block_expert, n_used, tok_buf, xn2, p, blk)

    fn = final_norm[None]
    y_p = _combine(dest[:bp * tp], yb, x1_p, pw_p, fn, _tile(bp * tp, 128))
    y_s = _combine(dest[bp * tp:], yb, x1_s, pw_s, fn, _tile(bs * ts, 128))

    def states(st, b):
        return (st[0][:, SUBLANES - 3:, :][None], st[1].reshape(1, b, D_LRU), st[2][:, SUBLANES - 3:, :][None],
                st[3][None])

    return (y_p.reshape(bp, tp, D_MODEL), y_s.reshape(bs, ts, D_MODEL)) + states(p_state, bp) + states(s_state, bs)
```

```python
import functools

import jax
import jax.numpy as jnp
from jax import lax
from jax.experimental import pallas as pl
from jax.experimental.pallas import tpu as pltpu

F32 = jnp.float32
BF16 = jnp.bfloat16

D_MODEL = 1024
D_LRU = 512
LRU_BLOCKS = 8
LRU_C = 8.0
CONV_W = 4
GDN_HEADS = 4
GDN_DK = 128
GDN_DV = 128
GDN_QKV = GDN_HEADS * (2 * GDN_DK + GDN_DV)
D_IN = 2 * D_LRU + GDN_QKV + GDN_HEADS * GDN_DV + 2 * GDN_HEADS
N_EXPERTS = 32
TOP_K = 4
SWIGLU_LIMIT = 7.0
SWIGLU_ALPHA = 1.702
EPS = 1e-6

LANES = 128
SUBLANES = 8
D_IN_PAD = 3200
COL_AB = (2 * D_LRU + GDN_QKV + GDN_HEADS * GDN_DV) // LANES
VMEM_LIMIT = 56 * 1024 * 1024

TOKEN_TILE = 256
EXPERT_ROWS = 256
GDN_CHUNK = 64


def _cparams(sem):
    return pltpu.CompilerParams(dimension_semantics=sem, vmem_limit_bytes=VMEM_LIMIT)


def _rms(x, gain):
    return x * lax.rsqrt(jnp.mean(x * x, axis=-1, keepdims=True) + EPS) * gain


def _softplus(x):
    return jnp.maximum(x, 0.0) + jnp.log1p(jnp.exp(-jnp.abs(x)))


def _sigmoid(x):
    return 1.0 / (1.0 + jnp.exp(-x))


TOKEN_ROWS = D_MODEL // LANES


def _load_token_tiles(ref, tok0, n_tok):
    return jnp.concatenate(
        [ref[pl.ds(tok0 * TOKEN_ROWS + j, n_tok, stride=TOKEN_ROWS), :] for j in range(TOKEN_ROWS)], axis=1)


def _store_token_tiles(ref, tok0, val):
    for j in range(TOKEN_ROWS):
        ref[pl.ds(tok0 * TOKEN_ROWS + j, val.shape[0], stride=TOKEN_ROWS), :] = val[:, j * LANES:(j + 1) * LANES]


def _mm(a, b):
    return jnp.dot(a.astype(BF16), b.astype(BF16), preferred_element_type=F32)


def _mm_nt(a, b):
    return lax.dot_general(a.astype(BF16), b.astype(BF16), (((1,), (1,)), ((), ())), preferred_element_type=F32)


def _mm_tn(a, b):
    return lax.dot_general(a.astype(BF16), b.astype(BF16), (((0,), (0,)), ((), ())), preferred_element_type=F32)


def _inproj_kernel(x_ref, g_ref, w_ref, o_ref):
    xn = _rms(x_ref[...], g_ref[...])
    o_ref[...] = jnp.dot(xn.astype(BF16), w_ref[...], preferred_element_type=F32)


def _inproj(x2d, gain, w_pad, tm):
    n = x2d.shape[0]
    return pl.pallas_call(
        _inproj_kernel,
        grid=(n // tm,),
        in_specs=[pl.BlockSpec((tm, D_MODEL), lambda i: (i, 0)),
                  pl.BlockSpec((1, D_MODEL), lambda i: (0, 0)),
                  pl.BlockSpec((D_MODEL, D_IN_PAD), lambda i: (0, 0))],
        out_specs=pl.BlockSpec((tm, D_IN_PAD), lambda i: (i, 0)),
        out_shape=jax.ShapeDtypeStruct((n, D_IN_PAD), F32),
        compiler_params=_cparams(("arbitrary",)),
        name="inproj",
    )(x2d, gain, w_pad)


def _conv_step(buf, x, w_ref, tt):
    buf[SUBLANES:SUBLANES + tt, :] = x
    y = buf[SUBLANES - 3:SUBLANES - 3 + tt, :] * w_ref[0:1, :]
    for j in range(1, CONV_W):
        y = y + buf[SUBLANES - 3 + j:SUBLANES - 3 + j + tt, :] * w_ref[j:j + 1, :]
    tail = buf[tt:tt + SUBLANES, :]
    buf[0:SUBLANES, :] = tail
    return y


def _lru_kernel(xl_ref, yl_ref, conv0_ref, h0_ref, cw_ref, cb_ref, wa_ref, ba_ref, wx_ref, bx_ref, lam_ref, nrm_ref,
                out_ref, conv1_ref, h1_ref, cbuf, a_buf, b_buf, hcar, *, tt, reset_first):
    t = pl.program_id(1)

    @pl.when(t == 0)
    def _():
        cbuf[0:SUBLANES, :] = conv0_ref[0]
        hcar[...] = h0_ref[0]

    xc = _conv_step(cbuf, xl_ref[...], cw_ref, tt) + cb_ref[...]
    xb = xc.astype(BF16)
    r = _sigmoid(jnp.dot(xb, wa_ref[...], preferred_element_type=F32) + ba_ref[...])
    i = _sigmoid(jnp.dot(xb, wx_ref[...], preferred_element_type=F32) + bx_ref[...])
    log_a = (-LRU_C) * r * _softplus(-lam_ref[...])
    a = jnp.exp(log_a)
    mult = jnp.sqrt(-jnp.tanh(log_a) * (a * a + 1.0))
    if reset_first:
        row = lax.broadcasted_iota(jnp.int32, (tt, D_LRU), 0)
        mult = jnp.where((row == 0) & (t == 0), 1.0, mult)
    a_buf[...] = a
    b_buf[...] = mult * i * xc

    row8 = lax.broadcasted_iota(jnp.int32, (SUBLANES, D_LRU), 0)

    def group(gi, h):
        r0 = pl.multiple_of(gi * SUBLANES, SUBLANES)
        a8 = a_buf[pl.ds(r0, SUBLANES), :]
        b8 = b_buf[pl.ds(r0, SUBLANES), :]
        for d in (1, 2, 4):
            keep = row8 >= d
            b8 = jnp.where(keep, a8 * pltpu.roll(b8, d, 0) + b8, b8)
            a8 = jnp.where(keep, a8 * pltpu.roll(a8, d, 0), a8)
        h8 = a8 * h + b8
        b_buf[pl.ds(r0, SUBLANES), :] = h8
        return h8[SUBLANES - 1:SUBLANES, :]

    h_last = lax.fori_loop(0, tt // SUBLANES, group, hcar[...])
    hcar[...] = h_last
    hh = b_buf[...]
    out_ref[...] = _rms(hh * jax.nn.gelu(yl_ref[...]), nrm_ref[...]).astype(out_ref.dtype)

    @pl.when(t == pl.num_programs(1) - 1)
    def _():
        conv1_ref[0] = cbuf[0:SUBLANES, :]
        h1_ref[0] = h_last


def _lru(proj, conv0, h0, shared_init, n_streams, t_len, tt, reset_first, p):
    nt = t_len // tt
    st = (lambda s, t: (0, 0, 0)) if shared_init else (lambda s, t: (s, 0, 0))
    vec = lambda: pl.BlockSpec((1, D_LRU), lambda s, t: (0, 0))
    n = n_streams * t_len
    return pl.pallas_call(
        functools.partial(_lru_kernel, tt=tt, reset_first=reset_first),
        grid=(n_streams, nt),
        in_specs=[pl.BlockSpec((tt, D_LRU), lambda s, t: (s * nt + t, 0)),
                  pl.BlockSpec((tt, D_LRU), lambda s, t: (s * nt + t, 1)),
                  pl.BlockSpec((1, SUBLANES, D_LRU), st),
                  pl.BlockSpec((1, 1, D_LRU), st),
                  pl.BlockSpec((CONV_W, D_LRU), lambda s, t: (0, 0)),
                  vec(),
                  pl.BlockSpec((D_LRU, D_LRU), lambda s, t: (0, 0)),
                  vec(),
                  pl.BlockSpec((D_LRU, D_LRU), lambda s, t: (0, 0)),
                  vec(), vec(), vec()],
        out_specs=[pl.BlockSpec((tt, D_LRU), lambda s, t: (s * nt + t, 0)),
                   pl.BlockSpec((1, SUBLANES, D_LRU), lambda s, t: (s, 0, 0)),
                   pl.BlockSpec((1, 1, D_LRU), lambda s, t: (s, 0, 0))],
        out_shape=[jax.ShapeDtypeStruct((n, D_LRU), BF16),
                   jax.ShapeDtypeStruct((n_streams, SUBLANES, D_LRU), F32),
                   jax.ShapeDtypeStruct((n_streams, 1, D_LRU), F32)],
        scratch_shapes=[pltpu.VMEM((tt + SUBLANES, D_LRU), F32),
                        pltpu.VMEM((tt, D_LRU), F32),
                        pltpu.VMEM((tt, D_LRU), F32),
                        pltpu.VMEM((1, D_LRU), F32)],
        compiler_params=_cparams(("arbitrary", "arbitrary")),
        name="rglru",
    )(proj, proj, conv0, h0, p["lru_conv_w"], p["lru_conv_b"], p["lru_wa_bd"], p["lru_b_a"], p["lru_wx_bd"],
      p["lru_b_x"], p["lru_lambda"], p["lru_norm"])


def _gdn_kernel(q_ref, k_ref, v_ref, z_ref, ab_ref, conv0_ref, s0_ref, cw_ref, par_ref, nrm_ref,
                o_ref, conv1_ref, s1_ref, cbuf, s_scr, *, c):
    t = pl.program_id(1)
    hd = GDN_HEADS * GDN_DK

    @pl.when(t == 0)
    def _():
        for j in range(3):
            cbuf[j, 0:SUBLANES, :] = conv0_ref[0, :, j * hd:(j + 1) * hd]
        s_scr[...] = s0_ref[0]

    def conv_silu(j, ref):
        y = _conv_step(cbuf.at[j], ref[...], cw_ref.at[:, j * hd:(j + 1) * hd], c)
        return y * _sigmoid(y)

    qa = conv_silu(0, q_ref)
    ka = conv_silu(1, k_ref)
    va = conv_silu(2, v_ref)

    ab = ab_ref[...]
    g_all = -jnp.exp(par_ref[0:1, :]) * _softplus(ab + par_ref[1:2, :])
    beta_all = _sigmoid(ab)
    row = lax.broadcasted_iota(jnp.int32, (c, LANES), 0)
    gc = g_all
    d = 1
    while d < c:
        gc = gc + jnp.where(row >= d, pltpu.roll(gc, d, 0), 0.0)
        d *= 2

    ri = lax.broadcasted_iota(jnp.int32, (c, c), 0)
    ci = lax.broadcasted_iota(jnp.int32, (c, c), 1)
    causal = ri >= ci
    strict = ri > ci
    eye = ri == ci

    for h in range(GDN_HEADS):
        sl = slice(h * GDN_DK, (h + 1) * GDN_DK)
        q = qa[:, sl]
        k = ka[:, sl]
        v = va[:, sl]
        q = q * lax.rsqrt(jnp.sum(q * q, axis=-1, keepdims=True) + EPS) * (GDN_DK ** -0.5)
        k = k * lax.rsqrt(jnp.sum(k * k, axis=-1, keepdims=True) + EPS)
        gcol = gc[:, h:h + 1]
        bcol = beta_all[:, GDN_HEADS + h:GDN_HEADS + h + 1]
        grow = jnp.sum(jnp.where(eye, gcol, 0.0), axis=0, keepdims=True)
        decay = jnp.where(causal, jnp.exp(jnp.where(causal, gcol - grow, 0.0)), 0.0)
        egc = jnp.exp(gcol)
        g_last = gc[c - 1:c, h:h + 1]
        kb = k * bcol
        n1 = jnp.where(strict, _mm_nt(kb, k) * decay, 0.0)
        x = jnp.concatenate([v * bcol, kb * egc], axis=1)
        x = x - _mm(n1, x)
        pw = n1
        m = 2
        while m < c:
            pw = _mm(pw, pw)
            x = x + _mm(pw, x)
            m *= 2
        u = x[:, :GDN_DV]
        w = x[:, GDN_DV:]
        attn = _mm_nt(q, k) * decay
        s = s_scr[h]
        v_new = u - _mm(w, s)
        o = _mm(q * egc, s) + _mm(attn, v_new)
        s_scr[h] = s * jnp.exp(g_last) + _mm_tn(k * jnp.exp(g_last - gcol), v_new)
        z = z_ref[:, sl]
        o_ref[:, sl] = (_rms(o, nrm_ref[...]) * (z * _sigmoid(z))).astype(o_ref.dtype)

    @pl.when(t == pl.num_programs(1) - 1)
    def _():
        for j in range(3):
            conv1_ref[0, :, j * hd:(j + 1) * hd] = cbuf[j, 0:SUBLANES, :]
        s1_ref[0] = s_scr[...]


def _gdn(proj, conv0, s0, shared_init, n_streams, t_len, c, p):
    nt = t_len // c
    hd = GDN_HEADS * GDN_DK
    st3 = (lambda s, t: (0, 0, 0)) if shared_init else (lambda s, t: (s, 0, 0))
    st4 = (lambda s, t: (0, 0, 0, 0)) if shared_init else (lambda s, t: (s, 0, 0, 0))
    n = n_streams * t_len
    col = lambda j: pl.BlockSpec((c, hd), lambda s, t: (s * nt + t, j))
    return pl.pallas_call(
        functools.partial(_gdn_kernel, c=c),
        grid=(n_streams, nt),
        in_specs=[col(2), col(3), col(4), col(5),
                  pl.BlockSpec((c, LANES), lambda s, t: (s * nt + t, COL_AB)),
                  pl.BlockSpec((1, SUBLANES, GDN_QKV), st3),
                  pl.BlockSpec((1, GDN_HEADS, GDN_DK, GDN_DV), st4),
                  pl.BlockSpec((CONV_W, GDN_QKV), lambda s, t: (0, 0)),
                  pl.BlockSpec((2, LANES), lambda s, t: (0, 0)),
                  pl.BlockSpec((1, GDN_DV), lambda s, t: (0, 0))],
        out_specs=[pl.BlockSpec((c, hd), lambda s, t: (s * nt + t, 0)),
                   pl.BlockSpec((1, SUBLANES, GDN_QKV), lambda s, t: (s, 0, 0)),
                   pl.BlockSpec((1, GDN_HEADS, GDN_DK, GDN_DV), lambda s, t: (s, 0, 0, 0))],
        out_shape=[jax.ShapeDtypeStruct((n, hd), BF16),
                   jax.ShapeDtypeStruct((n_streams, SUBLANES, GDN_QKV), F32),
                   jax.ShapeDtypeStruct((n_streams, GDN_HEADS, GDN_DK, GDN_DV), F32)],
        scratch_shapes=[pltpu.VMEM((3, c + SUBLANES, hd), F32),
                        pltpu.VMEM((GDN_HEADS, GDN_DK, GDN_DV), F32)],
        compiler_params=_cparams(("arbitrary", "arbitrary")),
        name="gdn",
    )(proj, proj, proj, proj, proj, conv0, s0, p["gdn_conv_w"], p["gdn_par"], p["gdn_norm"])


def _post_kernel(lru_p, gdn_p, x_p, lru_s, gdn_s, x_s, wo1_ref, wo2_ref, nrm_ref, rw_ref, rb_ref,
                 x1_ref, xn2_ref, route_ref, pw_ref, cnt_ref, cnt_scr, *, tt, n_p):
    i = pl.program_id(0)

    @pl.when(i == 0)
    def _():
        cnt_scr[...] = jnp.zeros_like(cnt_scr)

    is_p = i < n_p
    lru = jnp.where(is_p, lru_p[...], lru_s[...])
    gdn = jnp.where(is_p, gdn_p[...], gdn_s[...])
    x = jnp.where(is_p, x_p[...], x_s[...])
    m = (jnp.dot(lru, wo1_ref[...], preferred_element_type=F32)
         + jnp.dot(gdn, wo2_ref[...], preferred_element_type=F32))
    x1 = x + m
    x1_ref[...] = x1
    xn2 = _rms(x1, nrm_ref[...])
    _store_token_tiles(xn2_ref, 0, xn2)
    lane = lax.broadcasted_iota(jnp.int32, (tt, LANES), 1)
    logits = jnp.dot(xn2.astype(BF16), rw_ref[...], preferred_element_type=F32) + rb_ref[...]
    logits = jnp.where(lane < N_EXPERTS, logits, -jnp.inf)
    vals, idxs = [], []
    for _ in range(TOP_K):
        mx = jnp.max(logits, axis=-1, keepdims=True)
        ix = jnp.min(jnp.where(logits == mx, lane, LANES), axis=-1, keepdims=True)
        logits = jnp.where(lane == ix, -jnp.inf, logits)
        vals.append(mx)
        idxs.append(ix)
    es = [jnp.exp(v - vals[0]) for v in vals]
    den = es[0] + es[1] + es[2] + es[3]
    onehot = jnp.zeros((tt, LANES), F32)
    for ix in idxs:
        onehot = onehot + (lane == ix).astype(F32)
    ri = lax.broadcasted_iota(jnp.int32, (tt, tt), 0)
    ci = lax.broadcasted_iota(jnp.int32, (tt, tt), 1)
    before = jnp.dot((ri > ci).astype(BF16), onehot.astype(BF16), preferred_element_type=F32) + cnt_scr[...]
    route = jnp.zeros((tt, LANES), jnp.int32)
    pw = jnp.zeros((tt, LANES), F32)
    for kk in range(TOP_K):
        rank = jnp.sum(jnp.where(lane == idxs[kk], before, 0.0), axis=-1, keepdims=True).astype(jnp.int32)
        route = jnp.where(lane == kk, idxs[kk], route)
        route = jnp.where(lane == TOP_K + kk, rank, route)
        pw = jnp.where(lane == kk, es[kk] / den, pw)
    route_ref[...] = route
    pw_ref[...] = pw
    cnt_scr[...] = cnt_scr[...] + jnp.sum(onehot, axis=0, keepdims=True)
    cnt_ref[...] = cnt_scr[...]


def _post(lru_p, gdn_p, x_p, lru_s, gdn_s, x_s, tt, p):
    n_p = x_p.shape[0] // tt
    n_s = x_s.shape[0] // tt
    n = x_p.shape[0] + x_s.shape[0]
    prow = lambda w: pl.BlockSpec((tt, w), lambda i: (jnp.minimum(i, n_p - 1), 0))
    srow = lambda w: pl.BlockSpec((tt, w), lambda i: (jnp.maximum(i - n_p, 0), 0))
    row = lambda w: pl.BlockSpec((tt, w), lambda i: (i, 0))
    full = lambda a, b: pl.BlockSpec((a, b), lambda i: (0, 0))
    return pl.pallas_call(
        functools.partial(_post_kernel, tt=tt, n_p=n_p),
        grid=(n_p + n_s,),
        in_specs=[prow(D_LRU), prow(D_LRU), prow(D_MODEL), srow(D_LRU), srow(D_LRU), srow(D_MODEL),
                  full(D_LRU, D_MODEL), full(D_LRU, D_MODEL),
                  full(1, D_MODEL), full(D_MODEL, LANES), full(1, LANES)],
        out_specs=[row(D_MODEL), pl.BlockSpec((tt * TOKEN_ROWS, LANES), lambda i: (i, 0)), row(LANES), row(LANES),
                   full(1, LANES)],
        out_shape=[jax.ShapeDtypeStruct((n, D_MODEL), F32), jax.ShapeDtypeStruct((n * TOKEN_ROWS, LANES), F32),
                   jax.ShapeDtypeStruct((n, LANES), jnp.int32), jax.ShapeDtypeStruct((n, LANES), F32),
                   jax.ShapeDtypeStruct((1, LANES), F32)],
        scratch_shapes=[pltpu.VMEM((1, LANES), F32)],
        compiler_params=_cparams(("arbitrary",)),
        name="outproj_router",
    )(lru_p, gdn_p, x_p, lru_s, gdn_s, x_s, p["w_out1"], p["w_out2"], p["norm_ffn"], p["router_w"], p["router_b"])


def _expert_kernel(be_ref, first_ref, src_ref, srcn_ref, dstp_ref, dst_ref, x_hbm,
                   wg_ref, bg_ref, wu_ref, bu_ref, wd_ref, bd_ref, y_hbm,
                   xb0, xb1, ob0, ob1, wbf, gsem, ssem, *, blk):
    b = pl.program_id(0)
    n_blocks = pl.num_programs(0)
    xbufs, obufs = (xb0, xb1), (ob0, ob1)
    tile_rows = blk * TOKEN_ROWS

    def gather(idx_ref, s):
        for r in range(blk):
            row = pl.multiple_of(idx_ref[0, 0, r], TOKEN_ROWS)
            pltpu.make_async_copy(x_hbm.at[pl.ds(row, TOKEN_ROWS), :],
                                  xbufs[s].at[pl.ds(r * TOKEN_ROWS, TOKEN_ROWS), :], gsem.at[s]).start()

    def wait_gather(s):
        pltpu.make_async_copy(x_hbm.at[pl.ds(0, tile_rows), :], xbufs[s], gsem.at[s]).wait()

    def scatter(idx_ref, s):
        for r in range(blk):
            row = pl.multiple_of(idx_ref[0, 0, r], TOKEN_ROWS)
            pltpu.make_async_copy(obufs[s].at[pl.ds(r * TOKEN_ROWS, TOKEN_ROWS), :],
                                  y_hbm.at[pl.ds(row, TOKEN_ROWS), :], ssem.at[s]).start()

    def wait_scatter(s):
        pltpu.make_async_copy(obufs[s], y_hbm.at[pl.ds(0, tile_rows), :], ssem.at[s]).wait()

    @pl.when(b == 0)
    def _():
        gather(src_ref, 0)
        ob1[...] = jnp.zeros_like(ob1)

    @pl.when(first_ref[b] == 1)
    def _():
        wbf[0] = wg_ref[0].astype(BF16)
        wbf[1] = wu_ref[0].astype(BF16)
        wbf[2] = wd_ref[0].astype(BF16)

    def step(cur):
        oth = 1 - cur
        wait_gather(cur)

        @pl.when(b >= 1)
        def _():
            wait_scatter(cur)

        gather(srcn_ref, oth)
        scatter(dstp_ref, oth)
        xb = _load_token_tiles(xbufs[cur], 0, blk).astype(BF16)
        gt = jnp.minimum(jnp.dot(xb, wbf[0], preferred_element_type=F32) + bg_ref[0], SWIGLU_LIMIT)
        up = jnp.clip(jnp.dot(xb, wbf[1], preferred_element_type=F32) + bu_ref[0], -SWIGLU_LIMIT, SWIGLU_LIMIT)
        hid = (up + 1.0) * gt * _sigmoid(SWIGLU_ALPHA * gt)
        _store_token_tiles(obufs[cur], 0, jnp.dot(hid.astype(BF16), wbf[2], preferred_element_type=F32) + bd_ref[0])

        @pl.when(b == n_blocks - 1)
        def _():
            wait_gather(oth)
            scatter(dst_ref, cur)
            wait_scatter(oth)
            wait_scatter(cur)

    @pl.when(b % 2 == 0)
    def _():
        step(0)

    @pl.when(b % 2 == 1)
    def _():
        step(1)


def _experts(block_expert, first, src, dst, spare0, xn2, n_out_rows, p, blk):
    n_blocks = block_expert.shape[0]
    src3 = (src * TOKEN_ROWS).reshape(n_blocks, 1, blk)
    dst3 = (dst * TOKEN_ROWS).reshape(n_blocks, 1, blk)
    dstp3 = jnp.concatenate([(spare0 * TOKEN_ROWS).reshape(1, 1, blk), dst3[:-1]], axis=0)
    wspec = lambda: pl.BlockSpec((1, D_MODEL, D_MODEL), lambda b, be, fi: (be[b], 0, 0))
    bspec = lambda: pl.BlockSpec((1, 1, D_MODEL), lambda b, be, fi: (be[b], 0, 0))
    ispec = lambda f: pl.BlockSpec((1, 1, blk), f, memory_space=pltpu.SMEM)
    rows = blk * TOKEN_ROWS
    grid_spec = pltpu.PrefetchScalarGridSpec(
        num_scalar_prefetch=2,
        grid=(n_blocks,),
        in_specs=[ispec(lambda b, be, fi: (b, 0, 0)),
                  ispec(lambda b, be, fi: (jnp.minimum(b + 1, n_blocks - 1), 0, 0)),
                  ispec(lambda b, be, fi: (b, 0, 0)),
                  ispec(lambda b, be, fi: (b, 0, 0)),
                  pl.BlockSpec(memory_space=pl.ANY),
                  wspec(), bspec(), wspec(), bspec(), wspec(), bspec()],
        out_specs=pl.BlockSpec(memory_space=pl.ANY),
        scratch_shapes=[pltpu.VMEM((rows, LANES), F32), pltpu.VMEM((rows, LANES), F32),
                        pltpu.VMEM((rows, LANES), F32), pltpu.VMEM((rows, LANES), F32),
                        pltpu.VMEM((3, D_MODEL, D_MODEL), BF16),
                        pltpu.SemaphoreType.DMA((2,)), pltpu.SemaphoreType.DMA((2,))],
    )
    return pl.pallas_call(
        functools.partial(_expert_kernel, blk=blk),
        grid_spec=grid_spec,
        out_shape=jax.ShapeDtypeStruct((n_out_rows * TOKEN_ROWS, LANES), F32),
        compiler_params=_cparams(("arbitrary",)),
        name="experts",
    )(block_expert, first, src3, src3, dstp3, dst3, xn2, p["exp_w_gate"], p["exp_b_gate"], p["exp_w_up"],
      p["exp_b_up"], p["exp_w_down"], p["exp_b_down"])


def _combine_kernel(y_ref, x1_ref, pw_ref, nrm_ref, op_ref, os_ref, *, tt, n_p):
    i = pl.program_id(0)
    pw = pw_ref[...]
    moe = pw[:, 0:1] * _load_token_tiles(y_ref, 0, tt)
    for kk in range(1, TOP_K):
        moe = moe + pw[:, kk:kk + 1] * _load_token_tiles(y_ref, kk * tt, tt)
    res = _rms(x1_ref[...] + moe, nrm_ref[...])

    @pl.when(i < n_p)
    def _():
        op_ref[...] = res

    @pl.when(i >= n_p)
    def _():
        os_ref[...] = res


def _combine(yt, x1, pw, final_norm, n_prompt, tt):
    n = x1.shape[0]
    n_p = n_prompt // tt
    n_s = (n - n_prompt) // tt
    return pl.pallas_call(
        functools.partial(_combine_kernel, tt=tt, n_p=n_p),
        grid=(n_p + n_s,),
        in_specs=[pl.BlockSpec((TOP_K * tt * TOKEN_ROWS, LANES), lambda i: (i, 0)),
                  pl.BlockSpec((tt, D_MODEL), lambda i: (i, 0)),
                  pl.BlockSpec((tt, LANES), lambda i: (i, 0)),
                  pl.BlockSpec((1, D_MODEL), lambda i: (0, 0))],
        out_specs=[pl.BlockSpec((tt, D_MODEL), lambda i: (jnp.minimum(i, n_p - 1), 0)),
                   pl.BlockSpec((tt, D_MODEL), lambda i: (jnp.maximum(i - n_p, 0), 0))],
        out_shape=[jax.ShapeDtypeStruct((n_prompt, D_MODEL), F32),
                   jax.ShapeDtypeStruct((n - n_prompt, D_MODEL), F32)],
        compiler_params=_cparams(("arbitrary",)),
        name="combine",
    )(yt, x1, pw, final_norm)


def _pad_rows(a, rows):
    return jnp.pad(a, ((0, 0), (rows - a.shape[1], 0), (0, 0)))


def _tile(n, pref):
    t = pref
    while n % t:
        t //= 2
    return t


def _mixer_group(x2d, n_streams, t_len, lru_conv0, lru_h0, gdn_conv0, gdn_s0, shared_init, reset_first, chunk, p):
    proj = _inproj(x2d, p["norm_mix"], p["w_in"], _tile(x2d.shape[0], TOKEN_TILE))
    lru_out, lru_conv1, lru_h1 = _lru(proj, lru_conv0, lru_h0, shared_init, n_streams, t_len,
                                      _tile(t_len, TOKEN_TILE), reset_first, p)
    gdn_out, gdn_conv1, gdn_s1 = _gdn(proj, gdn_conv0, gdn_s0, shared_init, n_streams, t_len, chunk, p)
    return lru_out, gdn_out, (lru_conv1, lru_h1, gdn_conv1, gdn_s1)


def _routing(route, cnt, n_tok, tt, blk):
    idx, rank = route[:, :TOP_K], route[:, TOP_K:2 * TOP_K]
    counts = cnt[0, :N_EXPERTS].astype(jnp.int32)
    padded = (counts + blk - 1) // blk * blk
    pends = jnp.cumsum(padded)
    pstarts = pends - padded
    dest = (pstarts[idx] + rank).reshape(-1)
    m_pairs = n_tok * TOP_K
    n_blocks = (m_pairs + N_EXPERTS * (blk - 1) + blk - 1) // blk
    n_slots = n_blocks * blk
    t_id = jnp.arange(n_tok, dtype=jnp.int32)[:, None]
    k_id = jnp.arange(TOP_K, dtype=jnp.int32)[None, :]
    ypos = ((t_id // tt) * (TOP_K * tt) + k_id * tt + t_id % tt).reshape(-1)
    s_id = jnp.arange(n_slots, dtype=jnp.int32)
    spare = m_pairs + ((s_id // blk) % 2) * blk + s_id % blk
    dst = spare.at[dest].set(ypos, unique_indices=True)
    valid = dst < m_pairs
    src = jnp.where(valid, (dst // (TOP_K * tt)) * tt + dst % tt, 0)
    starts = jnp.arange(n_blocks, dtype=jnp.int32) * blk
    block_expert = jnp.minimum(jnp.sum((pends[None, :] <= starts[:, None]).astype(jnp.int32), axis=1),
                               N_EXPERTS - 1)
    first = jnp.concatenate([jnp.ones((1,), jnp.int32),
                             (block_expert[1:] != block_expert[:-1]).astype(jnp.int32)])
    spare0 = m_pairs + blk + jnp.arange(blk, dtype=jnp.int32)
    return block_expert, first, src, dst, spare0, m_pairs + 2 * blk


def kernel(x_prompt, x_sample, state_lru_conv, state_lru_h, state_gdn_conv, state_gdn_S, meta_tokens, norm_mix, w_in, lru_conv_w, lru_conv_b, lru_w_a, lru_b_a, lru_w_x, lru_b_x, lru_lambda, lru_norm, gdn_conv_w, gdn_A_log, gdn_dt_bias, gdn_norm, w_out, norm_ffn, router_w, router_b, exp_w_gate, exp_b_gate, exp_w_up, exp_b_up, exp_w_down, exp_b_down, final_norm):
    bp, tp, _ = x_prompt.shape
    bs, ts, _ = x_sample.shape
    n_meta = meta_tokens.shape[0]
    n_prompt, n_sample = bp * tp, bs * ts
    tt = TOKEN_TILE
    assert n_prompt % tt == 0 and n_sample % tt == 0

    def blockdiag(w):
        eye = jnp.eye(LRU_BLOCKS, dtype=w.dtype)
        return jnp.einsum("ncd,nm->ncmd", w, eye).reshape(D_LRU, D_LRU)

    lane_pad = lambda v: jnp.pad(v, (0, LANES - v.shape[0]))[None]
    p = dict(
        norm_mix=norm_mix[0][None],
        w_in=jnp.pad(w_in[0], ((0, 0), (0, D_IN_PAD - D_IN))).astype(BF16),
        lru_conv_w=lru_conv_w[0], lru_conv_b=lru_conv_b[0][None],
        lru_wa_bd=blockdiag(lru_w_a[0]).astype(BF16), lru_b_a=lru_b_a[0][None],
        lru_wx_bd=blockdiag(lru_w_x[0]).astype(BF16), lru_b_x=lru_b_x[0][None],
        lru_lambda=lru_lambda[0][None], lru_norm=lru_norm[0][None],
        gdn_conv_w=gdn_conv_w[0],
        gdn_par=jnp.concatenate([lane_pad(gdn_A_log[0]), lane_pad(gdn_dt_bias[0])], axis=0),
        gdn_norm=gdn_norm[0][None],
        w_out1=w_out[0][:D_LRU].astype(BF16), w_out2=w_out[0][D_LRU:].astype(BF16),
        norm_ffn=norm_ffn[0][None],
        router_w=jnp.pad(router_w[0], ((0, 0), (0, LANES - N_EXPERTS))).astype(BF16),
        router_b=lane_pad(router_b[0]),
        exp_w_gate=exp_w_gate[0], exp_b_gate=exp_b_gate[0][:, None, :],
        exp_w_up=exp_w_up[0], exp_b_up=exp_b_up[0][:, None, :],
        exp_w_down=exp_w_down[0], exp_b_down=exp_b_down[0][:, None, :],
    )

    zc = lambda c: jnp.zeros((1, SUBLANES, c), F32)
    _, _, m_state = _mixer_group(meta_tokens, 1, n_meta, zc(D_LRU), jnp.zeros((1, 1, D_LRU), F32), zc(GDN_QKV),
                                 jnp.zeros((1, GDN_HEADS, GDN_DK, GDN_DV), F32), False, True, n_meta, p)
    xp2 = x_prompt.reshape(n_prompt, D_MODEL)
    xs2 = x_sample.reshape(n_sample, D_MODEL)
    p_lru, p_gdn, p_state = _mixer_group(xp2, bp, tp, m_state[0], m_state[1], m_state[2], m_state[3], True, False,
                                         GDN_CHUNK, p)
    s_lru, s_gdn, s_state = _mixer_group(xs2, bs, ts, _pad_rows(state_lru_conv[0], SUBLANES),
                                         state_lru_h[0][:, None, :], _pad_rows(state_gdn_conv[0], SUBLANES),
                                         state_gdn_S[0], False, False, GDN_CHUNK, p)

    x1, xn2, route, pw, cnt = _post(p_lru, p_gdn, xp2, s_lru, s_gdn, xs2, tt, p)
    n_tok = n_prompt + n_sample
    block_expert, first, src, dst, spare0, n_out_rows = _routing(route, cnt, n_tok, tt, EXPERT_ROWS)
    yt = _experts(block_expert, first, src, dst, spare0, xn2, n_out_rows, p, EXPERT_ROWS)
    y_p, y_s = _combine(yt, x1, pw, final_norm[None], n_prompt, tt)

    def states(st, b):
        return (st[0][:, SUBLANES - 3:, :][None], st[1].reshape(1, b, D_LRU), st[2][:, SUBLANES - 3:, :][None],
                st[3][None])

    return (y_p.reshape(bp, tp, D_MODEL), y_s.reshape(bs, ts, D_MODEL)) + states(p_state, bp) + states(s_state, bs)
```

```python
import functools

import jax
import jax.numpy as jnp
from jax import lax
from jax.experimental import pallas as pl
from jax.experimental.pallas import tpu as pltpu

F32 = jnp.float32
BF16 = jnp.bfloat16

D_MODEL = 1024
D_LRU = 512
LRU_BLOCKS = 8
LRU_C = 8.0
CONV_W = 4
GDN_HEADS = 4
GDN_DK = 128
GDN_DV = 128
GDN_QKV = GDN_HEADS * (2 * GDN_DK + GDN_DV)
D_IN = 2 * D_LRU + GDN_QKV + GDN_HEADS * GDN_DV + 2 * GDN_HEADS
N_EXPERTS = 32
TOP_K = 4
SWIGLU_LIMIT = 7.0
SWIGLU_ALPHA = 1.702
EPS = 1e-6

LANES = 128
SUBLANES = 8
D_IN_PAD = 3200
COL_AB = (2 * D_LRU + GDN_QKV + GDN_HEADS * GDN_DV) // LANES
VMEM_LIMIT = 56 * 1024 * 1024

DMA_PRIORITIES = 2
TOKEN_TILE = 256
EXPERT_ROWS = 256
GDN_CHUNK = 64


def _cparams(sem):
    return pltpu.CompilerParams(dimension_semantics=sem, vmem_limit_bytes=VMEM_LIMIT)


def _rms(x, gain):
    return x * lax.rsqrt(jnp.mean(x * x, axis=-1, keepdims=True) + EPS) * gain


def _softplus(x):
    return jnp.maximum(x, 0.0) + jnp.log1p(jnp.exp(-jnp.abs(x)))


def _sigmoid(x):
    return 1.0 / (1.0 + jnp.exp(-x))


TOKEN_ROWS = D_MODEL // LANES


def _load_token_tiles(ref, tok0, n_tok):
    return jnp.concatenate(
        [ref[pl.ds(tok0 * TOKEN_ROWS + j, n_tok, stride=TOKEN_ROWS), :] for j in range(TOKEN_ROWS)], axis=1)


def _store_token_tiles(ref, tok0, val):
    for j in range(TOKEN_ROWS):
        ref[pl.ds(tok0 * TOKEN_ROWS + j, val.shape[0], stride=TOKEN_ROWS), :] = val[:, j * LANES:(j + 1) * LANES]


def _mm(a, b):
    return jnp.dot(a.astype(BF16), b.astype(BF16), preferred_element_type=F32)


def _mm_nt(a, b):
    return lax.dot_general(a.astype(BF16), b.astype(BF16), (((1,), (1,)), ((), ())), preferred_element_type=F32)


def _mm_tn(a, b):
    return lax.dot_general(a.astype(BF16), b.astype(BF16), (((0,), (0,)), ((), ())), preferred_element_type=F32)


def _inproj_kernel(x_ref, g_ref, w_ref, o_ref):
    xn = _rms(x_ref[...], g_ref[...])
    o_ref[...] = jnp.dot(xn.astype(BF16), w_ref[...], preferred_element_type=F32)


def _inproj(x2d, gain, w_pad, tm):
    n = x2d.shape[0]
    return pl.pallas_call(
        _inproj_kernel,
        grid=(n // tm,),
        in_specs=[pl.BlockSpec((tm, D_MODEL), lambda i: (i, 0)),
                  pl.BlockSpec((1, D_MODEL), lambda i: (0, 0)),
                  pl.BlockSpec((D_MODEL, D_IN_PAD), lambda i: (0, 0))],
        out_specs=pl.BlockSpec((tm, D_IN_PAD), lambda i: (i, 0)),
        out_shape=jax.ShapeDtypeStruct((n, D_IN_PAD), F32),
        compiler_params=_cparams(("arbitrary",)),
        name="inproj",
    )(x2d, gain, w_pad)


def _conv_step(buf, x, w_ref, tt):
    buf[SUBLANES:SUBLANES + tt, :] = x
    y = buf[SUBLANES - 3:SUBLANES - 3 + tt, :] * w_ref[0:1, :]
    for j in range(1, CONV_W):
        y = y + buf[SUBLANES - 3 + j:SUBLANES - 3 + j + tt, :] * w_ref[j:j + 1, :]
    tail = buf[tt:tt + SUBLANES, :]
    buf[0:SUBLANES, :] = tail
    return y


def _lru_kernel(xl_ref, yl_ref, conv0_ref, h0_ref, cw_ref, cb_ref, wa_ref, ba_ref, wx_ref, bx_ref, lam_ref, nrm_ref,
                out_ref, conv1_ref, h1_ref, cbuf, a_buf, b_buf, hcar, *, tt, reset_first):
    t = pl.program_id(1)

    @pl.when(t == 0)
    def _():
        cbuf[0:SUBLANES, :] = conv0_ref[0]
        hcar[...] = h0_ref[0]

    xc = _conv_step(cbuf, xl_ref[...], cw_ref, tt) + cb_ref[...]
    xb = xc.astype(BF16)
    r = _sigmoid(jnp.dot(xb, wa_ref[...], preferred_element_type=F32) + ba_ref[...])
    i = _sigmoid(jnp.dot(xb, wx_ref[...], preferred_element_type=F32) + bx_ref[...])
    log_a = (-LRU_C) * r * _softplus(-lam_ref[...])
    a = jnp.exp(log_a)
    mult = jnp.sqrt(-jnp.tanh(log_a) * (a * a + 1.0))
    if reset_first:
        row = lax.broadcasted_iota(jnp.int32, (tt, D_LRU), 0)
        mult = jnp.where((row == 0) & (t == 0), 1.0, mult)
    a_buf[...] = a
    b_buf[...] = mult * i * xc

    row8 = lax.broadcasted_iota(jnp.int32, (SUBLANES, D_LRU), 0)

    def group(gi, h):
        r0 = pl.multiple_of(gi * SUBLANES, SUBLANES)
        a8 = a_buf[pl.ds(r0, SUBLANES), :]
        b8 = b_buf[pl.ds(r0, SUBLANES), :]
        for d in (1, 2, 4):
            keep = row8 >= d
            b8 = jnp.where(keep, a8 * pltpu.roll(b8, d, 0) + b8, b8)
            a8 = jnp.where(keep, a8 * pltpu.roll(a8, d, 0), a8)
        h8 = a8 * h + b8
        b_buf[pl.ds(r0, SUBLANES), :] = h8
        return h8[SUBLANES - 1:SUBLANES, :]

    h_last = lax.fori_loop(0, tt // SUBLANES, group, hcar[...])
    hcar[...] = h_last
    hh = b_buf[...]
    out_ref[...] = _rms(hh * jax.nn.gelu(yl_ref[...]), nrm_ref[...]).astype(out_ref.dtype)

    @pl.when(t == pl.num_programs(1) - 1)
    def _():
        conv1_ref[0] = cbuf[0:SUBLANES, :]
        h1_ref[0] = h_last


def _lru(proj, conv0, h0, shared_init, n_streams, t_len, tt, reset_first, p):
    nt = t_len // tt
    st = (lambda s, t: (0, 0, 0)) if shared_init else (lambda s, t: (s, 0, 0))
    vec = lambda: pl.BlockSpec((1, D_LRU), lambda s, t: (0, 0))
    n = n_streams * t_len
    return pl.pallas_call(
        functools.partial(_lru_kernel, tt=tt, reset_first=reset_first),
        grid=(n_streams, nt),
        in_specs=[pl.BlockSpec((tt, D_LRU), lambda s, t: (s * nt + t, 0)),
                  pl.BlockSpec((tt, D_LRU), lambda s, t: (s * nt + t, 1)),
                  pl.BlockSpec((1, SUBLANES, D_LRU), st),
                  pl.BlockSpec((1, 1, D_LRU), st),
                  pl.BlockSpec((CONV_W, D_LRU), lambda s, t: (0, 0)),
                  vec(),
                  pl.BlockSpec((D_LRU, D_LRU), lambda s, t: (0, 0)),
                  vec(),
                  pl.BlockSpec((D_LRU, D_LRU), lambda s, t: (0, 0)),
                  vec(), vec(), vec()],
        out_specs=[pl.BlockSpec((tt, D_LRU), lambda s, t: (s * nt + t, 0)),
                   pl.BlockSpec((1, SUBLANES, D_LRU), lambda s, t: (s, 0, 0)),
                   pl.BlockSpec((1, 1, D_LRU), lambda s, t: (s, 0, 0))],
        out_shape=[jax.ShapeDtypeStruct((n, D_LRU), BF16),
                   jax.ShapeDtypeStruct((n_streams, SUBLANES, D_LRU), F32),
                   jax.ShapeDtypeStruct((n_streams, 1, D_LRU), F32)],
        scratch_shapes=[pltpu.VMEM((tt + SUBLANES, D_LRU), F32),
                        pltpu.VMEM((tt, D_LRU), F32),
                        pltpu.VMEM((tt, D_LRU), F32),
                        pltpu.VMEM((1, D_LRU), F32)],
        compiler_params=_cparams(("arbitrary", "arbitrary")),
        name="rglru",
    )(proj, proj, conv0, h0, p["lru_conv_w"], p["lru_conv_b"], p["lru_wa_bd"], p["lru_b_a"], p["lru_wx_bd"],
      p["lru_b_x"], p["lru_lambda"], p["lru_norm"])


def _gdn_kernel(q_ref, k_ref, v_ref, z_ref, ab_ref, conv0_ref, s0_ref, cw_ref, par_ref, nrm_ref,
                o_ref, conv1_ref, s1_ref, cbuf, s_scr, *, c):
    t = pl.program_id(1)
    hd = GDN_HEADS * GDN_DK

    @pl.when(t == 0)
    def _():
        for j in range(3):
            cbuf[j, 0:SUBLANES, :] = conv0_ref[0, :, j * hd:(j + 1) * hd]
        s_scr[...] = s0_ref[0]

    def conv_silu(j, ref):
        y = _conv_step(cbuf.at[j], ref[...], cw_ref.at[:, j * hd:(j + 1) * hd], c)
        return y * _sigmoid(y)

    qa = conv_silu(0, q_ref)
    ka = conv_silu(1, k_ref)
    va = conv_silu(2, v_ref)

    ab = ab_ref[...]
    g_all = -jnp.exp(par_ref[0:1, :]) * _softplus(ab + par_ref[1:2, :])
    beta_all = _sigmoid(ab)
    row = lax.broadcasted_iota(jnp.int32, (c, LANES), 0)
    gc = g_all
    d = 1
    while d < c:
        gc = gc + jnp.where(row >= d, pltpu.roll(gc, d, 0), 0.0)
        d *= 2

    ri = lax.broadcasted_iota(jnp.int32, (c, c), 0)
    ci = lax.broadcasted_iota(jnp.int32, (c, c), 1)
    causal = ri >= ci
    strict = ri > ci
    eye = ri == ci

    for h in range(GDN_HEADS):
        sl = slice(h * GDN_DK, (h + 1) * GDN_DK)
        q = qa[:, sl]
        k = ka[:, sl]
        v = va[:, sl]
        q = q * lax.rsqrt(jnp.sum(q * q, axis=-1, keepdims=True) + EPS) * (GDN_DK ** -0.5)
        k = k * lax.rsqrt(jnp.sum(k * k, axis=-1, keepdims=True) + EPS)
        gcol = gc[:, h:h + 1]
        bcol = beta_all[:, GDN_HEADS + h:GDN_HEADS + h + 1]
        grow = jnp.sum(jnp.where(eye, gcol, 0.0), axis=0, keepdims=True)
        decay = jnp.where(causal, jnp.exp(jnp.where(causal, gcol - grow, 0.0)), 0.0)
        egc = jnp.exp(gcol)
        g_last = gc[c - 1:c, h:h + 1]
        kb = k * bcol
        n1 = jnp.where(strict, _mm_nt(kb, k) * decay, 0.0)
        x = jnp.concatenate([v * bcol, kb * egc], axis=1)
        x = x - _mm(n1, x)
        pw = n1
        m = 2
        while m < c:
            pw = _mm(pw, pw)
            x = x + _mm(pw, x)
            m *= 2
        u = x[:, :GDN_DV]
        w = x[:, GDN_DV:]
        attn = _mm_nt(q, k) * decay
        s = s_scr[h]
        v_new = u - _mm(w, s)
        o = _mm(q * egc, s) + _mm(attn, v_new)
        s_scr[h] = s * jnp.exp(g_last) + _mm_tn(k * jnp.exp(g_last - gcol), v_new)
        z = z_ref[:, sl]
        o_ref[:, sl] = (_rms(o, nrm_ref[...]) * (z * _sigmoid(z))).astype(o_ref.dtype)

    @pl.when(t == pl.num_programs(1) - 1)
    def _():
        for j in range(3):
            conv1_ref[0, :, j * hd:(j + 1) * hd] = cbuf[j, 0:SUBLANES, :]
        s1_ref[0] = s_scr[...]


def _gdn(proj, conv0, s0, shared_init, n_streams, t_len, c, p):
    nt = t_len // c
    hd = GDN_HEADS * GDN_DK
    st3 = (lambda s, t: (0, 0, 0)) if shared_init else (lambda s, t: (s, 0, 0))
    st4 = (lambda s, t: (0, 0, 0, 0)) if shared_init else (lambda s, t: (s, 0, 0, 0))
    n = n_streams * t_len
    col = lambda j: pl.BlockSpec((c, hd), lambda s, t: (s * nt + t, j))
    return pl.pallas_call(
        functools.partial(_gdn_kernel, c=c),
        grid=(n_streams, nt),
        in_specs=[col(2), col(3), col(4), col(5),
                  pl.BlockSpec((c, LANES), lambda s, t: (s * nt + t, COL_AB)),
                  pl.BlockSpec((1, SUBLANES, GDN_QKV), st3),
                  pl.BlockSpec((1, GDN_HEADS, GDN_DK, GDN_DV), st4),
                  pl.BlockSpec((CONV_W, GDN_QKV), lambda s, t: (0, 0)),
                  pl.BlockSpec((2, LANES), lambda s, t: (0, 0)),
                  pl.BlockSpec((1, GDN_DV), lambda s, t: (0, 0))],
        out_specs=[pl.BlockSpec((c, hd), lambda s, t: (s * nt + t, 0)),
                   pl.BlockSpec((1, SUBLANES, GDN_QKV), lambda s, t: (s, 0, 0)),
                   pl.BlockSpec((1, GDN_HEADS, GDN_DK, GDN_DV), lambda s, t: (s, 0, 0, 0))],
        out_shape=[jax.ShapeDtypeStruct((n, hd), BF16),
                   jax.ShapeDtypeStruct((n_streams, SUBLANES, GDN_QKV), F32),
                   jax.ShapeDtypeStruct((n_streams, GDN_HEADS, GDN_DK, GDN_DV), F32)],
        scratch_shapes=[pltpu.VMEM((3, c + SUBLANES, hd), F32),
                        pltpu.VMEM((GDN_HEADS, GDN_DK, GDN_DV), F32)],
        compiler_params=_cparams(("arbitrary", "arbitrary")),
        name="gdn",
    )(proj, proj, proj, proj, proj, conv0, s0, p["gdn_conv_w"], p["gdn_par"], p["gdn_norm"])


def _post_kernel(lru_p, gdn_p, x_p, lru_s, gdn_s, x_s, wo1_ref, wo2_ref, nrm_ref, rw_ref, rb_ref,
                 x1_ref, xn2_ref, route_ref, pw_ref, cnt_ref, cnt_scr, *, tt, n_p):
    i = pl.program_id(0)

    @pl.when(i == 0)
    def _():
        cnt_scr[...] = jnp.zeros_like(cnt_scr)

    is_p = i < n_p
    lru = jnp.where(is_p, lru_p[...], lru_s[...])
    gdn = jnp.where(is_p, gdn_p[...], gdn_s[...])
    x = jnp.where(is_p, x_p[...], x_s[...])
    m = (jnp.dot(lru, wo1_ref[...], preferred_element_type=F32)
         + jnp.dot(gdn, wo2_ref[...], preferred_element_type=F32))
    x1 = x + m
    x1_ref[...] = x1
    xn2 = _rms(x1, nrm_ref[...])
    _store_token_tiles(xn2_ref, 0, xn2)
    lane = lax.broadcasted_iota(jnp.int32, (tt, LANES), 1)
    logits = jnp.dot(xn2.astype(BF16), rw_ref[...], preferred_element_type=F32) + rb_ref[...]
    logits = jnp.where(lane < N_EXPERTS, logits, -jnp.inf)
    vals, idxs = [], []
    for _ in range(TOP_K):
        mx = jnp.max(logits, axis=-1, keepdims=True)
        ix = jnp.min(jnp.where(logits == mx, lane, LANES), axis=-1, keepdims=True)
        logits = jnp.where(lane == ix, -jnp.inf, logits)
        vals.append(mx)
        idxs.append(ix)
    es = [jnp.exp(v - vals[0]) for v in vals]
    den = es[0] + es[1] + es[2] + es[3]
    onehot = jnp.zeros((tt, LANES), F32)
    for ix in idxs:
        onehot = onehot + (lane == ix).astype(F32)
    ri = lax.broadcasted_iota(jnp.int32, (tt, tt), 0)
    ci = lax.broadcasted_iota(jnp.int32, (tt, tt), 1)
    before = jnp.dot((ri > ci).astype(BF16), onehot.astype(BF16), preferred_element_type=F32) + cnt_scr[...]
    route = jnp.zeros((tt, LANES), jnp.int32)
    pw = jnp.zeros((tt, LANES), F32)
    for kk in range(TOP_K):
        rank = jnp.sum(jnp.where(lane == idxs[kk], before, 0.0), axis=-1, keepdims=True).astype(jnp.int32)
        route = jnp.where(lane == kk, idxs[kk], route)
        route = jnp.where(lane == TOP_K + kk, rank, route)
        pw = jnp.where(lane == kk, es[kk] / den, pw)
    route_ref[...] = route
    pw_ref[...] = pw
    cnt_scr[...] = cnt_scr[...] + jnp.sum(onehot, axis=0, keepdims=True)
    cnt_ref[...] = cnt_scr[...]


def _post(lru_p, gdn_p, x_p, lru_s, gdn_s, x_s, tt, p):
    n_p = x_p.shape[0] // tt
    n_s = x_s.shape[0] // tt
    n = x_p.shape[0] + x_s.shape[0]
    prow = lambda w: pl.BlockSpec((tt, w), lambda i: (jnp.minimum(i, n_p - 1), 0))
    srow = lambda w: pl.BlockSpec((tt, w), lambda i: (jnp.maximum(i - n_p, 0), 0))
    row = lambda w: pl.BlockSpec((tt, w), lambda i: (i, 0))
    full = lambda a, b: pl.BlockSpec((a, b), lambda i: (0, 0))
    return pl.pallas_call(
        functools.partial(_post_kernel, tt=tt, n_p=n_p),
        grid=(n_p + n_s,),
        in_specs=[prow(D_LRU), prow(D_LRU), prow(D_MODEL), srow(D_LRU), srow(D_LRU), srow(D_MODEL),
                  full(D_LRU, D_MODEL), full(D_LRU, D_MODEL),
                  full(1, D_MODEL), full(D_MODEL, LANES), full(1, LANES)],
        out_specs=[row(D_MODEL), pl.BlockSpec((tt * TOKEN_ROWS, LANES), lambda i: (i, 0)), row(LANES), row(LANES),
                   full(1, LANES)],
        out_shape=[jax.ShapeDtypeStruct((n, D_MODEL), F32), jax.ShapeDtypeStruct((n * TOKEN_ROWS, LANES), F32),
                   jax.ShapeDtypeStruct((n, LANES), jnp.int32), jax.ShapeDtypeStruct((n, LANES), F32),
                   jax.ShapeDtypeStruct((1, LANES), F32)],
        scratch_shapes=[pltpu.VMEM((1, LANES), F32)],
        compiler_params=_cparams(("arbitrary",)),
        name="outproj_router",
    )(lru_p, gdn_p, x_p, lru_s, gdn_s, x_s, p["w_out1"], p["w_out2"], p["norm_ffn"], p["router_w"], p["router_b"])


def _expert_kernel(be_ref, first_ref, src_ref, srcn_ref, dstp_ref, dst_ref, x_hbm,
                   wg_ref, bg_ref, wu_ref, bu_ref, wd_ref, bd_ref, y_hbm,
                   xb0, xb1, ob0, ob1, wbf, gsem, ssem, *, blk):
    b = pl.program_id(0)
    n_blocks = pl.num_programs(0)
    xbufs, obufs = (xb0, xb1), (ob0, ob1)
    tile_rows = blk * TOKEN_ROWS

    def gather(idx_ref, s):
        for r in range(blk):
            row = pl.multiple_of(idx_ref[0, 0, r], TOKEN_ROWS)
            pltpu.make_async_copy(x_hbm.at[pl.ds(row, TOKEN_ROWS), :],
                                  xbufs[s].at[pl.ds(r * TOKEN_ROWS, TOKEN_ROWS), :], gsem.at[s]
                                  ).start(priority=r % DMA_PRIORITIES)

    def wait_gather(s):
        pltpu.make_async_copy(x_hbm.at[pl.ds(0, tile_rows), :], xbufs[s], gsem.at[s]).wait()

    def scatter(idx_ref, s):
        for r in range(blk):
            row = pl.multiple_of(idx_ref[0, 0, r], TOKEN_ROWS)
            pltpu.make_async_copy(obufs[s].at[pl.ds(r * TOKEN_ROWS, TOKEN_ROWS), :],
                                  y_hbm.at[pl.ds(row, TOKEN_ROWS), :], ssem.at[s]
                                  ).start(priority=r % DMA_PRIORITIES)

    def wait_scatter(s):
        pltpu.make_async_copy(obufs[s], y_hbm.at[pl.ds(0, tile_rows), :], ssem.at[s]).wait()

    @pl.when(b == 0)
    def _():
        gather(src_ref, 0)
        ob1[...] = jnp.zeros_like(ob1)

    @pl.when(first_ref[b] == 1)
    def _():
        wbf[0] = wg_ref[0].astype(BF16)
        wbf[1] = wu_ref[0].astype(BF16)
        wbf[2] = wd_ref[0].astype(BF16)

    def step(cur):
        oth = 1 - cur
        wait_gather(cur)

        @pl.when(b >= 1)
        def _():
            wait_scatter(cur)

        gather(srcn_ref, oth)
        scatter(dstp_ref, oth)
        xb = _load_token_tiles(xbufs[cur], 0, blk).astype(BF16)
        gt = jnp.minimum(jnp.dot(xb, wbf[0], preferred_element_type=F32) + bg_ref[0], SWIGLU_LIMIT)
        up = jnp.clip(jnp.dot(xb, wbf[1], preferred_element_type=F32) + bu_ref[0], -SWIGLU_LIMIT, SWIGLU_LIMIT)
        hid = (up + 1.0) * gt * _sigmoid(SWIGLU_ALPHA * gt)
        _store_token_tiles(obufs[cur], 0, jnp.dot(hid.astype(BF16), wbf[2], preferred_element_type=F32) + bd_ref[0])

        @pl.when(b == n_blocks - 1)
        def _():
            wait_gather(oth)
            scatter(dst_ref, cur)
            wait_scatter(oth)
            wait_scatter(cur)

    @pl.when(b % 2 == 0)
    def _():
        step(0)

    @pl.when(b % 2 == 1)
    def _():
        step(1)


def _experts(block_expert, first, src, dst, spare0, xn2, n_out_rows, p, blk):
    n_blocks = block_expert.shape[0]
    src3 = (src * TOKEN_ROWS).reshape(n_blocks, 1, blk)
    dst3 = (dst * TOKEN_ROWS).reshape(n_blocks, 1, blk)
    dstp3 = jnp.concatenate([(spare0 * TOKEN_ROWS).reshape(1, 1, blk), dst3[:-1]], axis=0)
    wspec = lambda: pl.BlockSpec((1, D_MODEL, D_MODEL), lambda b, be, fi: (be[b], 0, 0))
    bspec = lambda: pl.BlockSpec((1, 1, D_MODEL), lambda b, be, fi: (be[b], 0, 0))
    ispec = lambda f: pl.BlockSpec((1, 1, blk), f, memory_space=pltpu.SMEM)
    rows = blk * TOKEN_ROWS
    grid_spec = pltpu.PrefetchScalarGridSpec(
        num_scalar_prefetch=2,
        grid=(n_blocks,),
        in_specs=[ispec(lambda b, be, fi: (b, 0, 0)),
                  ispec(lambda b, be, fi: (jnp.minimum(b + 1, n_blocks - 1), 0, 0)),
                  ispec(lambda b, be, fi: (b, 0, 0)),
                  ispec(lambda b, be, fi: (b, 0, 0)),
                  pl.BlockSpec(memory_space=pl.ANY),
                  wspec(), bspec(), wspec(), bspec(), wspec(), bspec()],
        out_specs=pl.BlockSpec(memory_space=pl.ANY),
        scratch_shapes=[pltpu.VMEM((rows, LANES), F32), pltpu.VMEM((rows, LANES), F32),
                        pltpu.VMEM((rows, LANES), F32), pltpu.VMEM((rows, LANES), F32),
                        pltpu.VMEM((3, D_MODEL, D_MODEL), BF16),
                        pltpu.SemaphoreType.DMA((2,)), pltpu.SemaphoreType.DMA((2,))],
    )
    return pl.pallas_call(
        functools.partial(_expert_kernel, blk=blk),
        grid_spec=grid_spec,
        out_shape=jax.ShapeDtypeStruct((n_out_rows * TOKEN_ROWS, LANES), F32),
        compiler_params=_cparams(("arbitrary",)),
        name="experts",
    )(block_expert, first, src3, src3, dstp3, dst3, xn2, p["exp_w_gate"], p["exp_b_gate"], p["exp_w_up"],
      p["exp_b_up"], p["exp_w_down"], p["exp_b_down"])


def _combine_kernel(y_ref, x1_ref, pw_ref, nrm_ref, op_ref, os_ref, *, tt, n_p):
    i = pl.program_id(0)
    pw = pw_ref[...]
    moe = pw[:, 0:1] * _load_token_tiles(y_ref, 0, tt)
    for kk in range(1, TOP_K):
        moe = moe + pw[:, kk:kk + 1] * _load_token_tiles(y_ref, kk * tt, tt)
    res = _rms(x1_ref[...] + moe, nrm_ref[...])

    @pl.when(i < n_p)
    def _():
        op_ref[...] = res

    @pl.when(i >= n_p)
    def _():
        os_ref[...] = res


def _combine(yt, x1, pw, final_norm, n_prompt, tt):
    n = x1.shape[0]
    n_p = n_prompt // tt
    n_s = (n - n_prompt) // tt
    return pl.pallas_call(
        functools.partial(_combine_kernel, tt=tt, n_p=n_p),
        grid=(n_p + n_s,),
        in_specs=[pl.BlockSpec((TOP_K * tt * TOKEN_ROWS, LANES), lambda i: (i, 0)),
                  pl.BlockSpec((tt, D_MODEL), lambda i: (i, 0)),
                  pl.BlockSpec((tt, LANES), lambda i: (i, 0)),
                  pl.BlockSpec((1, D_MODEL), lambda i: (0, 0))],
        out_specs=[pl.BlockSpec((tt, D_MODEL), lambda i: (jnp.minimum(i, n_p - 1), 0)),
                   pl.BlockSpec((tt, D_MODEL), lambda i: (jnp.maximum(i - n_p, 0), 0))],
        out_shape=[jax.ShapeDtypeStruct((n_prompt, D_MODEL), F32),
                   jax.ShapeDtypeStruct((n - n_prompt, D_MODEL), F32)],
        compiler_params=_cparams(("arbitrary",)),
        name="combine",
    )(yt, x1, pw, final_norm)


def _pad_rows(a, rows):
    return jnp.pad(a, ((0, 0), (rows - a.shape[1], 0), (0, 0)))


def _tile(n, pref):
    t = pref
    while n % t:
        t //= 2
    return t


def _mixer_group(x2d, n_streams, t_len, lru_conv0, lru_h0, gdn_conv0, gdn_s0, shared_init, reset_first, chunk, p):
    proj = _inproj(x2d, p["norm_mix"], p["w_in"], _tile(x2d.shape[0], TOKEN_TILE))
    lru_out, lru_conv1, lru_h1 = _lru(proj, lru_conv0, lru_h0, shared_init, n_streams, t_len,
                                      _tile(t_len, TOKEN_TILE), reset_first, p)
    gdn_out, gdn_conv1, gdn_s1 = _gdn(proj, gdn_conv0, gdn_s0, shared_init, n_streams, t_len, chunk, p)
    return lru_out, gdn_out, (lru_conv1, lru_h1, gdn_conv1, gdn_s1)


def _routing(route, cnt, n_tok, tt, blk):
    idx, rank = route[:, :TOP_K], route[:, TOP_K:2 * TOP_K]
    counts = cnt[0, :N_EXPERTS].astype(jnp.int32)
    padded = (counts + blk - 1) // blk * blk
    pends = jnp.cumsum(padded)
    pstarts = pends - padded
    dest = (pstarts[idx] + rank).reshape(-1)
    m_pairs = n_tok * TOP_K
    n_blocks = (m_pairs + N_EXPERTS * (blk - 1) + blk - 1) // blk
    n_slots = n_blocks * blk
    t_id = jnp.arange(n_tok, dtype=jnp.int32)[:, None]
    k_id = jnp.arange(TOP_K, dtype=jnp.int32)[None, :]
    ypos = ((t_id // tt) * (TOP_K * tt) + k_id * tt + t_id % tt).reshape(-1)
    s_id = jnp.arange(n_slots, dtype=jnp.int32)
    spare = m_pairs + ((s_id // blk) % 2) * blk + s_id % blk
    dst = spare.at[dest].set(ypos, unique_indices=True)
    valid = dst < m_pairs
    src = jnp.where(valid, (dst // (TOP_K * tt)) * tt + dst % tt, 0)
    starts = jnp.arange(n_blocks, dtype=jnp.int32) * blk
    block_expert = jnp.minimum(jnp.sum((pends[None, :] <= starts[:, None]).astype(jnp.int32), axis=1),
                               N_EXPERTS - 1)
    first = jnp.concatenate([jnp.ones((1,), jnp.int32),
                             (block_expert[1:] != block_expert[:-1]).astype(jnp.int32)])
    spare0 = m_pairs + blk + jnp.arange(blk, dtype=jnp.int32)
    return block_expert, first, src, dst, spare0, m_pairs + 2 * blk


def kernel(x_prompt, x_sample, state_lru_conv, state_lru_h, state_gdn_conv, state_gdn_S, meta_tokens, norm_mix, w_in, lru_conv_w, lru_conv_b, lru_w_a, lru_b_a, lru_w_x, lru_b_x, lru_lambda, lru_norm, gdn_conv_w, gdn_A_log, gdn_dt_bias, gdn_norm, w_out, norm_ffn, router_w, router_b, exp_w_gate, exp_b_gate, exp_w_up, exp_b_up, exp_w_down, exp_b_down, final_norm):
    bp, tp, _ = x_prompt.shape
    bs, ts, _ = x_sample.shape
    n_meta = meta_tokens.shape[0]
    n_prompt, n_sample = bp * tp, bs * ts
    tt = TOKEN_TILE
    assert n_prompt % tt == 0 and n_sample % tt == 0

    def blockdiag(w):
        eye = jnp.eye(LRU_BLOCKS, dtype=w.dtype)
        return jnp.einsum("ncd,nm->ncmd", w, eye).reshape(D_LRU, D_LRU)

    lane_pad = lambda v: jnp.pad(v, (0, LANES - v.shape[0]))[None]
    p = dict(
        norm_mix=norm_mix[0][None],
        w_in=jnp.pad(w_in[0], ((0, 0), (0, D_IN_PAD - D_IN))).astype(BF16),
        lru_conv_w=lru_conv_w[0], lru_conv_b=lru_conv_b[0][None],
        lru_wa_bd=blockdiag(lru_w_a[0]).astype(BF16), lru_b_a=lru_b_a[0][None],
        lru_wx_bd=blockdiag(lru_w_x[0]).astype(BF16), lru_b_x=lru_b_x[0][None],
        lru_lambda=lru_lambda[0][None], lru_norm=lru_norm[0][None],
        gdn_conv_w=gdn_conv_w[0],
        gdn_par=jnp.concatenate([lane_pad(gdn_A_log[0]), lane_pad(gdn_dt_bias[0])], axis=0),
        gdn_norm=gdn_norm[0][None],
        w_out1=w_out[0][:D_LRU].astype(BF16), w_out2=w_out[0][D_LRU:].astype(BF16),
        norm_ffn=norm_ffn[0][None],
        router_w=jnp.pad(router_w[0], ((0, 0), (0, LANES - N_EXPERTS))).astype(BF16),
        router_b=lane_pad(router_b[0]),
        exp_w_gate=exp_w_gate[0], exp_b_gate=exp_b_gate[0][:, None, :],
        exp_w_up=exp_w_up[0], exp_b_up=exp_b_up[0][:, None, :],
        exp_w_down=exp_w_down[0], exp_b_down=exp_b_down[0][:, None, :],
    )

    zc = lambda c: jnp.zeros((1, SUBLANES, c), F32)
    _, _, m_state = _mixer_group(meta_tokens, 1, n_meta, zc(D_LRU), jnp.zeros((1, 1, D_LRU), F32), zc(GDN_QKV),
                                 jnp.zeros((1, GDN_HEADS, GDN_DK, GDN_DV), F32), False, True, n_meta, p)
    xp2 = x_prompt.reshape(n_prompt, D_MODEL)
    xs2 = x_sample.reshape(n_sample, D_MODEL)
    p_lru, p_gdn, p_state = _mixer_group(xp2, bp, tp, m_state[0], m_state[1], m_state[2], m_state[3], True, False,
                                         GDN_CHUNK, p)
    s_lru, s_gdn, s_state = _mixer_group(xs2, bs, ts, _pad_rows(state_lru_conv[0], SUBLANES),
                                         state_lru_h[0][:, None, :], _pad_rows(state_gdn_conv[0], SUBLANES),
                                         state_gdn_S[0], False, False, GDN_CHUNK, p)

    x1, xn2, route, pw, cnt = _post(p_lru, p_gdn, xp2, s_lru, s_gdn, xs2, tt, p)
    n_tok = n_prompt + n_sample
    block_expert, first, src, dst, spare0, n_out_rows = _routing(route, cnt, n_tok, tt, EXPERT_ROWS)
    yt = _experts(block_expert, first, src, dst, spare0, xn2, n_out_rows, p, EXPERT_ROWS)
    y_p, y_s = _combine(yt, x1, pw, final_norm[None], n_prompt, tt)

    def states(st, b):
        return (st[0][:, SUBLANES - 3:, :][None], st[1].reshape(1, b, D_LRU), st[2][:, SUBLANES - 3:, :][None],
                st[3][None])

    return (y_p.reshape(bp, tp, D_MODEL), y_s.reshape(bs, ts, D_MODEL)) + states(p_state, bp) + states(s_state, bs)
```

```python
import functools

import jax
import jax.numpy as jnp
from jax import lax
from jax.experimental import pallas as pl
from jax.experimental.pallas import tpu as pltpu

F32 = jnp.float32
BF16 = jnp.bfloat16

D_MODEL = 1024
D_LRU = 512
LRU_BLOCKS = 8
LRU_C = 8.0
CONV_W = 4
GDN_HEADS = 4
GDN_DK = 128
GDN_DV = 128
GDN_QKV = GDN_HEADS * (2 * GDN_DK + GDN_DV)
D_IN = 2 * D_LRU + GDN_QKV + GDN_HEADS * GDN_DV + 2 * GDN_HEADS
N_EXPERTS = 32
TOP_K = 4
SWIGLU_LIMIT = 7.0
SWIGLU_ALPHA = 1.702
EPS = 1e-6

LANES = 128
SUBLANES = 8
D_IN_PAD = 3200
COL_AB = (2 * D_LRU + GDN_QKV + GDN_HEADS * GDN_DV) // LANES
VMEM_LIMIT = 56 * 1024 * 1024

DMA_PRIORITIES = 2
TOKEN_TILE = 256
EXPERT_ROWS = 256
GDN_CHUNK = 64
REC_STREAMS = 8
PRE_CHAINS = 8


def _cparams(sem):
    return pltpu.CompilerParams(dimension_semantics=sem, vmem_limit_bytes=VMEM_LIMIT)


def _rms(x, gain):
    return x * lax.rsqrt(jnp.mean(x * x, axis=-1, keepdims=True) + EPS) * gain


def _softplus(x):
    return jnp.maximum(x, 0.0) + jnp.log1p(jnp.exp(-jnp.abs(x)))


def _sigmoid(x):
    return 1.0 / (1.0 + jnp.exp(-x))


TOKEN_ROWS = D_MODEL // LANES


def _load_token_tiles(ref, tok0, n_tok):
    return jnp.concatenate(
        [ref[pl.ds(tok0 * TOKEN_ROWS + j, n_tok, stride=TOKEN_ROWS), :] for j in range(TOKEN_ROWS)], axis=1)


def _store_token_tiles(ref, tok0, val):
    for j in range(TOKEN_ROWS):
        ref[pl.ds(tok0 * TOKEN_ROWS + j, val.shape[0], stride=TOKEN_ROWS), :] = val[:, j * LANES:(j + 1) * LANES]


def _mm(a, b):
    return jnp.dot(a.astype(BF16), b.astype(BF16), preferred_element_type=F32)


def _mm_nt(a, b):
    return lax.dot_general(a.astype(BF16), b.astype(BF16), (((1,), (1,)), ((), ())), preferred_element_type=F32)


def _mm_tn(a, b):
    return lax.dot_general(a.astype(BF16), b.astype(BF16), (((0,), (0,)), ((), ())), preferred_element_type=F32)


def _inproj_kernel(x_ref, g_ref, w_ref, o_ref):
    xn = _rms(x_ref[...], g_ref[...])
    o_ref[...] = jnp.dot(xn.astype(BF16), w_ref[...], preferred_element_type=F32)


def _inproj(x2d, gain, w_pad, tm):
    n = x2d.shape[0]
    return pl.pallas_call(
        _inproj_kernel,
        grid=(n // tm,),
        in_specs=[pl.BlockSpec((tm, D_MODEL), lambda i: (i, 0)),
                  pl.BlockSpec((1, D_MODEL), lambda i: (0, 0)),
                  pl.BlockSpec((D_MODEL, D_IN_PAD), lambda i: (0, 0))],
        out_specs=pl.BlockSpec((tm, D_IN_PAD), lambda i: (i, 0)),
        out_shape=jax.ShapeDtypeStruct((n, D_IN_PAD), F32),
        compiler_params=_cparams(("arbitrary",)),
        name="inproj",
    )(x2d, gain, w_pad)


def _conv_step(buf, x, w_ref, tt):
    buf[SUBLANES:SUBLANES + tt, :] = x
    y = x * w_ref[CONV_W - 1:CONV_W, :]
    for j in range(CONV_W - 2, -1, -1):
        y = y + buf[SUBLANES - 3 + j:SUBLANES - 3 + j + tt, :] * w_ref[j:j + 1, :]
    tail = buf[tt:tt + SUBLANES, :]
    buf[0:SUBLANES, :] = tail
    return y


def _lru_kernel(xl_ref, yl_ref, conv0_ref, h0_ref, cw_ref, cb_ref, wa_ref, ba_ref, wx_ref, bx_ref, lam_ref, nrm_ref,
                out_ref, conv1_ref, h1_ref, cbuf, a_buf, b_buf, hcar, *, tt, reset_first):
    t = pl.program_id(1)

    @pl.when(t == 0)
    def _():
        cbuf[0:SUBLANES, :] = conv0_ref[0]
        hcar[...] = h0_ref[0]

    xc = _conv_step(cbuf, xl_ref[...], cw_ref, tt) + cb_ref[...]
    xb = xc.astype(BF16)
    r = _sigmoid(jnp.dot(xb, wa_ref[...], preferred_element_type=F32) + ba_ref[...])
    i = _sigmoid(jnp.dot(xb, wx_ref[...], preferred_element_type=F32) + bx_ref[...])
    log_a = (-LRU_C) * r * _softplus(-lam_ref[...])
    a = jnp.exp(log_a)
    mult = jnp.sqrt(-jnp.tanh(log_a) * (a * a + 1.0))
    if reset_first:
        row = lax.broadcasted_iota(jnp.int32, (tt, D_LRU), 0)
        mult = jnp.where((row == 0) & (t == 0), 1.0, mult)
    a_buf[...] = a
    b_buf[...] = mult * i * xc

    row8 = lax.broadcasted_iota(jnp.int32, (SUBLANES, D_LRU), 0)

    def group(gi, h):
        r0 = pl.multiple_of(gi * SUBLANES, SUBLANES)
        a8 = a_buf[pl.ds(r0, SUBLANES), :]
        b8 = b_buf[pl.ds(r0, SUBLANES), :]
        for d in (1, 2, 4):
            keep = row8 >= d
            b8 = jnp.where(keep, a8 * pltpu.roll(b8, d, 0) + b8, b8)
            a8 = jnp.where(keep, a8 * pltpu.roll(a8, d, 0), a8)
        h8 = a8 * h + b8
        b_buf[pl.ds(r0, SUBLANES), :] = h8
        return h8[SUBLANES - 1:SUBLANES, :]

    h_last = lax.fori_loop(0, tt // SUBLANES, group, hcar[...])
    hcar[...] = h_last
    hh = b_buf[...]
    out_ref[...] = _rms(hh * jax.nn.gelu(yl_ref[...]), nrm_ref[...]).astype(out_ref.dtype)

    @pl.when(t == pl.num_programs(1) - 1)
    def _():
        conv1_ref[0] = cbuf[0:SUBLANES, :]
        h1_ref[0] = h_last


def _lru(proj, conv0, h0, shared_init, n_streams, t_len, tt, reset_first, p):
    nt = t_len // tt
    st = (lambda s, t: (0, 0, 0)) if shared_init else (lambda s, t: (s, 0, 0))
    vec = lambda: pl.BlockSpec((1, D_LRU), lambda s, t: (0, 0))
    n = n_streams * t_len
    return pl.pallas_call(
        functools.partial(_lru_kernel, tt=tt, reset_first=reset_first),
        grid=(n_streams, nt),
        in_specs=[pl.BlockSpec((tt, D_LRU), lambda s, t: (s * nt + t, 0)),
                  pl.BlockSpec((tt, D_LRU), lambda s, t: (s * nt + t, 1)),
                  pl.BlockSpec((1, SUBLANES, D_LRU), st),
                  pl.BlockSpec((1, 1, D_LRU), st),
                  pl.BlockSpec((CONV_W, D_LRU), lambda s, t: (0, 0)),
                  vec(),
                  pl.BlockSpec((D_LRU, D_LRU), lambda s, t: (0, 0)),
                  vec(),
                  pl.BlockSpec((D_LRU, D_LRU), lambda s, t: (0, 0)),
                  vec(), vec(), vec()],
        out_specs=[pl.BlockSpec((tt, D_LRU), lambda s, t: (s * nt + t, 0)),
                   pl.BlockSpec((1, SUBLANES, D_LRU), lambda s, t: (s, 0, 0)),
                   pl.BlockSpec((1, 1, D_LRU), lambda s, t: (s, 0, 0))],
        out_shape=[jax.ShapeDtypeStruct((n, D_LRU), BF16),
                   jax.ShapeDtypeStruct((n_streams, SUBLANES, D_LRU), F32),
                   jax.ShapeDtypeStruct((n_streams, 1, D_LRU), F32)],
        scratch_shapes=[pltpu.VMEM((tt + SUBLANES, D_LRU), F32),
                        pltpu.VMEM((tt, D_LRU), F32),
                        pltpu.VMEM((tt, D_LRU), F32),
                        pltpu.VMEM((1, D_LRU), F32)],
        compiler_params=_cparams(("arbitrary", "arbitrary")),
        name="rglru",
    )(proj, proj, conv0, h0, p["lru_conv_w"], p["lru_conv_b"], p["lru_wa_bd"], p["lru_b_a"], p["lru_wx_bd"],
      p["lru_b_x"], p["lru_lambda"], p["lru_norm"])


def _gdn_kernel(q_ref, k_ref, v_ref, z_ref, ab_ref, conv0_ref, s0_ref, cw_ref, par_ref, nrm_ref,
                o_ref, conv1_ref, s1_ref, cbuf, s_scr, *, c):
    t = pl.program_id(1)
    hd = GDN_HEADS * GDN_DK

    @pl.when(t == 0)
    def _():
        for j in range(3):
            cbuf[j, 0:SUBLANES, :] = conv0_ref[0, :, j * hd:(j + 1) * hd]
        s_scr[...] = s0_ref[0]

    def conv_silu(j, ref):
        y = _conv_step(cbuf.at[j], ref[...], cw_ref.at[:, j * hd:(j + 1) * hd], c)
        return y * _sigmoid(y)

    qa = conv_silu(0, q_ref)
    ka = conv_silu(1, k_ref)
    va = conv_silu(2, v_ref)

    ab = ab_ref[...]
    g_all = -jnp.exp(par_ref[0:1, :]) * _softplus(ab + par_ref[1:2, :])
    beta_all = _sigmoid(ab)
    row = lax.broadcasted_iota(jnp.int32, (c, LANES), 0)
    gc = g_all
    d = 1
    while d < c:
        gc = gc + jnp.where(row >= d, pltpu.roll(gc, d, 0), 0.0)
        d *= 2

    ri = lax.broadcasted_iota(jnp.int32, (c, c), 0)
    ci = lax.broadcasted_iota(jnp.int32, (c, c), 1)
    causal = ri >= ci
    strict = ri > ci
    eye = ri == ci

    for h in range(GDN_HEADS):
        sl = slice(h * GDN_DK, (h + 1) * GDN_DK)
        q = qa[:, sl]
        k = ka[:, sl]
        v = va[:, sl]
        q = q * lax.rsqrt(jnp.sum(q * q, axis=-1, keepdims=True) + EPS) * (GDN_DK ** -0.5)
        k = k * lax.rsqrt(jnp.sum(k * k, axis=-1, keepdims=True) + EPS)
        gcol = gc[:, h:h + 1]
        bcol = beta_all[:, GDN_HEADS + h:GDN_HEADS + h + 1]
        grow = jnp.sum(jnp.where(eye, gcol, 0.0), axis=0, keepdims=True)
        decay = jnp.where(causal, jnp.exp(jnp.where(causal, gcol - grow, 0.0)), 0.0)
        egc = jnp.exp(gcol)
        g_last = gc[c - 1:c, h:h + 1]
        kb = k * bcol
        n1 = jnp.where(strict, _mm_nt(kb, k) * decay, 0.0)
        x = jnp.concatenate([v * bcol, kb * egc], axis=1)
        x = x - _mm(n1, x)
        pw = n1
        m = 2
        while m < c:
            pw = _mm(pw, pw)
            x = x + _mm(pw, x)
            m *= 2
        u = x[:, :GDN_DV]
        w = x[:, GDN_DV:]
        attn = _mm_nt(q, k) * decay
        s = s_scr[h]
        v_new = u - _mm(w, s)
        o = _mm(q * egc, s) + _mm(attn, v_new)
        s_scr[h] = s * jnp.exp(g_last) + _mm_tn(k * jnp.exp(g_last - gcol), v_new)
        z = z_ref[:, sl]
        o_ref[:, sl] = (_rms(o, nrm_ref[...]) * (z * _sigmoid(z))).astype(o_ref.dtype)

    @pl.when(t == pl.num_programs(1) - 1)
    def _():
        for j in range(3):
            conv1_ref[0, :, j * hd:(j + 1) * hd] = cbuf[j, 0:SUBLANES, :]
        s1_ref[0] = s_scr[...]


def _gdn(proj, conv0, s0, shared_init, n_streams, t_len, c, p):
    nt = t_len // c
    hd = GDN_HEADS * GDN_DK
    st3 = (lambda s, t: (0, 0, 0)) if shared_init else (lambda s, t: (s, 0, 0))
    st4 = (lambda s, t: (0, 0, 0, 0)) if shared_init else (lambda s, t: (s, 0, 0, 0))
    n = n_streams * t_len
    col = lambda j: pl.BlockSpec((c, hd), lambda s, t: (s * nt + t, j))
    return pl.pallas_call(
        functools.partial(_gdn_kernel, c=c),
        grid=(n_streams, nt),
        in_specs=[col(2), col(3), col(4), col(5),
                  pl.BlockSpec((c, LANES), lambda s, t: (s * nt + t, COL_AB)),
                  pl.BlockSpec((1, SUBLANES, GDN_QKV), st3),
                  pl.BlockSpec((1, GDN_HEADS, GDN_DK, GDN_DV), st4),
                  pl.BlockSpec((CONV_W, GDN_QKV), lambda s, t: (0, 0)),
                  pl.BlockSpec((2, LANES), lambda s, t: (0, 0)),
                  pl.BlockSpec((1, GDN_DV), lambda s, t: (0, 0))],
        out_specs=[pl.BlockSpec((c, hd), lambda s, t: (s * nt + t, 0)),
                   pl.BlockSpec((1, SUBLANES, GDN_QKV), lambda s, t: (s, 0, 0)),
                   pl.BlockSpec((1, GDN_HEADS, GDN_DK, GDN_DV), lambda s, t: (s, 0, 0, 0))],
        out_shape=[jax.ShapeDtypeStruct((n, hd), BF16),
                   jax.ShapeDtypeStruct((n_streams, SUBLANES, GDN_QKV), F32),
                   jax.ShapeDtypeStruct((n_streams, GDN_HEADS, GDN_DK, GDN_DV), F32)],
        scratch_shapes=[pltpu.VMEM((3, c + SUBLANES, hd), F32),
                        pltpu.VMEM((GDN_HEADS, GDN_DK, GDN_DV), F32)],
        compiler_params=_cparams(("arbitrary", "arbitrary")),
        name="gdn",
    )(proj, proj, proj, proj, proj, conv0, s0, p["gdn_conv_w"], p["gdn_par"], p["gdn_norm"])


def _pre_kernel(x_ref, nrm_ref, win_ref, lconv0_ref, h0_ref, gconv0_ref,
                lcw_ref, lcb_ref, wa_ref, ba_ref, wx_ref, bx_ref, lam_ref, lnrm_ref, gcw_ref, par_ref,
                lru_ref, u_ref, w_ref, qd_ref, kd_ref, at_ref, z_ref, gl_ref, lconv1_ref, h1_ref, gconv1_ref,
                lbuf, a_buf, b_buf, hcar, gbuf, *, tt, c, reset_first):
    t = pl.program_id(1)
    hd = GDN_HEADS * GDN_DK

    @pl.when(t == 0)
    def _():
        lbuf[0:SUBLANES, :] = lconv0_ref[0]
        hcar[...] = h0_ref[0]
        for j in range(3):
            gbuf[j, 0:SUBLANES, :] = gconv0_ref[0, :, j * hd:(j + 1) * hd]

    xn = _rms(x_ref[...], nrm_ref[...])
    proj = jnp.dot(xn.astype(BF16), win_ref[...], preferred_element_type=F32)

    xc = _conv_step(lbuf, proj[:, 0:D_LRU], lcw_ref, tt) + lcb_ref[...]
    xb = xc.astype(BF16)
    r = _sigmoid(jnp.dot(xb, wa_ref[...], preferred_element_type=F32) + ba_ref[...])
    i = _sigmoid(jnp.dot(xb, wx_ref[...], preferred_element_type=F32) + bx_ref[...])
    log_a = (-LRU_C) * r * _softplus(-lam_ref[...])
    a = jnp.exp(log_a)
    mult = jnp.sqrt(-jnp.tanh(log_a) * (a * a + 1.0))
    if reset_first:
        row = lax.broadcasted_iota(jnp.int32, (tt, D_LRU), 0)
        mult = jnp.where((row == 0) & (t == 0), 1.0, mult)
    a_buf[...] = a
    b_buf[...] = mult * i * xc
    row8 = lax.broadcasted_iota(jnp.int32, (SUBLANES, D_LRU), 0)

    def group(gi, h):
        r0 = pl.multiple_of(gi * SUBLANES, SUBLANES)
        a8 = a_buf[pl.ds(r0, SUBLANES), :]
        b8 = b_buf[pl.ds(r0, SUBLANES), :]
        for d in (1, 2, 4):
            keep = row8 >= d
            b8 = jnp.where(keep, a8 * pltpu.roll(b8, d, 0) + b8, b8)
            a8 = jnp.where(keep, a8 * pltpu.roll(a8, d, 0), a8)
        h8 = a8 * h + b8
        b_buf[pl.ds(r0, SUBLANES), :] = h8
        return h8[SUBLANES - 1:SUBLANES, :]

    h_last = lax.fori_loop(0, tt // SUBLANES, group, hcar[...])
    hcar[...] = h_last
    lru_ref[...] = _rms(b_buf[...] * jax.nn.gelu(proj[:, D_LRU:2 * D_LRU]), lnrm_ref[...]).astype(lru_ref.dtype)

    def conv_silu(j):
        y = _conv_step(gbuf.at[j], proj[:, 2 * D_LRU + j * hd:2 * D_LRU + (j + 1) * hd],
                       gcw_ref.at[:, j * hd:(j + 1) * hd], tt)
        return y * _sigmoid(y)

    qa = conv_silu(0)
    ka = conv_silu(1)
    va = conv_silu(2)
    z_ref[...] = proj[:, 2 * D_LRU + GDN_QKV:2 * D_LRU + GDN_QKV + hd]
    ab = proj[:, COL_AB * LANES:(COL_AB + 1) * LANES]
    g_all = -jnp.exp(par_ref[0:1, :]) * _softplus(ab + par_ref[1:2, :])
    beta_all = _sigmoid(ab)

    rowc = lax.broadcasted_iota(jnp.int32, (c, LANES), 0)
    ri = lax.broadcasted_iota(jnp.int32, (c, c), 0)
    ci = lax.broadcasted_iota(jnp.int32, (c, c), 1)
    causal = ri >= ci
    strict = ri > ci
    eye = ri == ci
    lane_pad = jnp.zeros((c, LANES - c), F32)

    pairs = [(j, h) for j in range(tt // c) for h in range(GDN_HEADS)]
    for p0 in range(0, len(pairs), PRE_CHAINS):
        group_pairs = pairs[p0:p0 + PRE_CHAINS]
        gcs = {}
        for j in sorted({j for j, _ in group_pairs}):
            gc = g_all[j * c:(j + 1) * c]
            d = 1
            while d < c:
                gc = gc + jnp.where(rowc >= d, pltpu.roll(gc, d, 0), 0.0)
                d *= 2
            gcs[j] = gc
        ns, xs = [], []
        for j, h in group_pairs:
            rows = slice(j * c, (j + 1) * c)
            sl = slice(h * GDN_DK, (h + 1) * GDN_DK)
            gc = gcs[j]
            q = qa[rows, sl]
            k = ka[rows, sl]
            q = q * lax.rsqrt(jnp.sum(q * q, axis=-1, keepdims=True) + EPS) * (GDN_DK ** -0.5)
            k = k * lax.rsqrt(jnp.sum(k * k, axis=-1, keepdims=True) + EPS)
            gcol = gc[:, h:h + 1]
            bcol = beta_all[rows, GDN_HEADS + h:GDN_HEADS + h + 1]
            grow = jnp.sum(jnp.where(eye, gcol, 0.0), axis=0, keepdims=True)
            decay = jnp.where(causal, jnp.exp(jnp.where(causal, gcol - grow, 0.0)), 0.0)
            egc = jnp.exp(gcol)
            g_last = gc[c - 1:c, h:h + 1]
            kb = k * bcol
            qd_ref[rows, sl] = (q * egc).astype(qd_ref.dtype)
            kd_ref[rows, sl] = (k * jnp.exp(g_last - gcol)).astype(kd_ref.dtype)
            gl_ref[j, :, sl] = jnp.broadcast_to(jnp.exp(g_last), (1, GDN_DV))
            at_ref[rows, sl] = jnp.concatenate([_mm_nt(q, k) * decay, lane_pad], axis=1).astype(at_ref.dtype)
            ns.append(jnp.where(strict, _mm_nt(kb, k) * decay, 0.0))
            xs.append(jnp.concatenate([va[rows, sl] * bcol, kb * egc], axis=1))
        xs = [x - _mm(n, x) for n, x in zip(ns, xs)]
        m = 2
        while m < c:
            ns = [_mm(n, n) for n in ns]
            xs = [x + _mm(n, x) for n, x in zip(ns, xs)]
            m *= 2
        for (j, h), x in zip(group_pairs, xs):
            rows = slice(j * c, (j + 1) * c)
            sl = slice(h * GDN_DK, (h + 1) * GDN_DK)
            u_ref[rows, sl] = x[:, :GDN_DV]
            w_ref[rows, sl] = x[:, GDN_DV:].astype(w_ref.dtype)

    @pl.when(t == pl.num_programs(1) - 1)
    def _():
        lconv1_ref[0] = lbuf[0:SUBLANES, :]
        h1_ref[0] = h_last
        for j in range(3):
            gconv1_ref[0, :, j * hd:(j + 1) * hd] = gbuf[j, 0:SUBLANES, :]


def _pre(x2d, lconv0, h0, gconv0, shared_init, n_streams, t_len, tt, c, reset_first, p):
    nt = t_len // tt
    hd = GDN_HEADS * GDN_DK
    n = n_streams * t_len
    st = (lambda s, t: (0, 0, 0)) if shared_init else (lambda s, t: (s, 0, 0))
    full = lambda a, b: pl.BlockSpec((a, b), lambda s, t: (0, 0))
    row = lambda w: pl.BlockSpec((tt, w), lambda s, t: (s * nt + t, 0))
    return pl.pallas_call(
        functools.partial(_pre_kernel, tt=tt, c=c, reset_first=reset_first),
        grid=(n_streams, nt),
        in_specs=[row(D_MODEL), full(1, D_MODEL), full(D_MODEL, D_IN_PAD),
                  pl.BlockSpec((1, SUBLANES, D_LRU), st), pl.BlockSpec((1, 1, D_LRU), st),
                  pl.BlockSpec((1, SUBLANES, GDN_QKV), st),
                  full(CONV_W, D_LRU), full(1, D_LRU), full(D_LRU, D_LRU), full(1, D_LRU), full(D_LRU, D_LRU),
                  full(1, D_LRU), full(1, D_LRU), full(1, D_LRU), full(CONV_W, GDN_QKV), full(2, LANES)],
        out_specs=[row(D_LRU), row(hd), row(hd), row(hd), row(hd), row(hd), row(hd),
                   pl.BlockSpec((tt // c, 1, hd), lambda s, t: (s * nt + t, 0, 0)),
                   pl.BlockSpec((1, SUBLANES, D_LRU), lambda s, t: (s, 0, 0)),
                   pl.BlockSpec((1, 1, D_LRU), lambda s, t: (s, 0, 0)),
                   pl.BlockSpec((1, SUBLANES, GDN_QKV), lambda s, t: (s, 0, 0))],
        out_shape=[jax.ShapeDtypeStruct((n, D_LRU), BF16),
                   jax.ShapeDtypeStruct((n, hd), F32),
                   jax.ShapeDtypeStruct((n, hd), BF16),
                   jax.ShapeDtypeStruct((n, hd), BF16),
                   jax.ShapeDtypeStruct((n, hd), BF16),
                   jax.ShapeDtypeStruct((n, hd), BF16),
                   jax.ShapeDtypeStruct((n, hd), F32),
                   jax.ShapeDtypeStruct((n // c, 1, hd), F32),
                   jax.ShapeDtypeStruct((n_streams, SUBLANES, D_LRU), F32),
                   jax.ShapeDtypeStruct((n_streams, 1, D_LRU), F32),
                   jax.ShapeDtypeStruct((n_streams, SUBLANES, GDN_QKV), F32)],
        scratch_shapes=[pltpu.VMEM((tt + SUBLANES, D_LRU), F32),
                        pltpu.VMEM((tt, D_LRU), F32),
                        pltpu.VMEM((tt, D_LRU), F32),
                        pltpu.VMEM((1, D_LRU), F32),
                        pltpu.VMEM((3, tt + SUBLANES, hd), F32)],
        compiler_params=_cparams(("arbitrary", "arbitrary")),
        name="mixer_pre",
    )(x2d, p["norm_mix"], p["w_in"], lconv0, h0, gconv0, p["lru_conv_w"], p["lru_conv_b"], p["lru_wa_bd"],
      p["lru_b_a"], p["lru_wx_bd"], p["lru_b_x"], p["lru_lambda"], p["lru_norm"], p["gdn_conv_w"], p["gdn_par"])


def _rec_kernel(u_ref, w_ref, qd_ref, kd_ref, at_ref, z_ref, gl_ref, s0_ref, nrm_ref, o_ref, s1_ref, s_scr,
                *, sb, c, shared_init):
    t = pl.program_id(1)

    @pl.when(t == 0)
    def _():
        for s in range(sb):
            s_scr[s] = s0_ref[0 if shared_init else s]

    chains = [(s, h, slice(h * GDN_DK, (h + 1) * GDN_DK)) for s in range(sb) for h in range(GDN_HEADS)]
    m1s = [jnp.dot(jnp.concatenate([w_ref[s, :, sl], qd_ref[s, :, sl]], axis=0), s_scr[s, h].astype(BF16),
                   preferred_element_type=F32) for s, h, sl in chains]
    vns = [(u_ref[s, :, sl] - m1[:c]).astype(BF16) for (s, h, sl), m1 in zip(chains, m1s)]
    for (s, h, sl), v_new in zip(chains, vns):
        s_scr[s, h] = s_scr[s, h] * gl_ref[s, 0, :, sl] + lax.dot_general(
            kd_ref[s, :, sl], v_new, (((0,), (0,)), ((), ())), preferred_element_type=F32)
    for (s, h, sl), m1, v_new in zip(chains, m1s, vns):
        o = m1[c:] + jnp.dot(at_ref[s, :, h * GDN_DK:h * GDN_DK + c], v_new, preferred_element_type=F32)
        z = z_ref[s, :, sl]
        o_ref[s, :, sl] = (_rms(o, nrm_ref[...]) * (z * _sigmoid(z))).astype(o_ref.dtype)

    @pl.when(t == pl.num_programs(1) - 1)
    def _():
        s1_ref[...] = s_scr[...]


def _rec(u, w, qd, kd, at, z, gl, s0, shared_init, n_streams, t_len, sb, c, p):
    hd = GDN_HEADS * GDN_DK
    nt = t_len // c
    v3 = lambda a: a.reshape(n_streams, t_len, hd)
    blk3 = lambda: pl.BlockSpec((sb, c, hd), lambda g, t: (g, t, 0))
    s_spec = (pl.BlockSpec((1, GDN_HEADS, GDN_DK, GDN_DV), lambda g, t: (0, 0, 0, 0)) if shared_init else
              pl.BlockSpec((sb, GDN_HEADS, GDN_DK, GDN_DV), lambda g, t: (g, 0, 0, 0)))
    out, s1 = pl.pallas_call(
        functools.partial(_rec_kernel, sb=sb, c=c, shared_init=shared_init),
        grid=(n_streams // sb, nt),
        in_specs=[blk3(), blk3(), blk3(), blk3(), blk3(), blk3(),
                  pl.BlockSpec((sb, 1, 1, hd), lambda g, t: (g, t, 0, 0)),
                  s_spec,
                  pl.BlockSpec((1, GDN_DV), lambda g, t: (0, 0))],
        out_specs=[blk3(), pl.BlockSpec((sb, GDN_HEADS, GDN_DK, GDN_DV), lambda g, t: (g, 0, 0, 0))],
        out_shape=[jax.ShapeDtypeStruct((n_streams, t_len, hd), BF16),
                   jax.ShapeDtypeStruct((n_streams, GDN_HEADS, GDN_DK, GDN_DV), F32)],
        scratch_shapes=[pltpu.VMEM((sb, GDN_HEADS, GDN_DK, GDN_DV), F32)],
        compiler_params=_cparams(("arbitrary", "arbitrary")),
        name="gdn_state",
    )(v3(u), v3(w), v3(qd), v3(kd), v3(at), v3(z), gl.reshape(n_streams, nt, 1, hd), s0, p["gdn_norm"])
    return out.reshape(n_streams * t_len, hd), s1


def _post_kernel(lru_p, gdn_p, x_p, lru_s, gdn_s, x_s, wo1_ref, wo2_ref, nrm_ref, rw_ref, rb_ref,
                 x1_ref, xn2_ref, route_ref, pw_ref, cnt_ref, cnt_scr, *, tt, n_p):
    i = pl.program_id(0)

    @pl.when(i == 0)
    def _():
        cnt_scr[...] = jnp.zeros_like(cnt_scr)

    is_p = i < n_p
    lru = jnp.where(is_p, lru_p[...], lru_s[...])
    gdn = jnp.where(is_p, gdn_p[...], gdn_s[...])
    x = jnp.where(is_p, x_p[...], x_s[...])
    m = (jnp.dot(lru, wo1_ref[...], preferred_element_type=F32)
         + jnp.dot(gdn, wo2_ref[...], preferred_element_type=F32))
    x1 = x + m
    x1_ref[...] = x1
    xn2 = _rms(x1, nrm_ref[...])
    _store_token_tiles(xn2_ref, 0, xn2)
    lane = lax.broadcasted_iota(jnp.int32, (tt, LANES), 1)
    logits = jnp.dot(xn2.astype(BF16), rw_ref[...], preferred_element_type=F32) + rb_ref[...]
    logits = jnp.where(lane < N_EXPERTS, logits, -jnp.inf)
    vals, idxs = [], []
    for _ in range(TOP_K):
        mx = jnp.max(logits, axis=-1, keepdims=True)
        ix = jnp.min(jnp.where(logits == mx, lane, LANES), axis=-1, keepdims=True)
        logits = jnp.where(lane == ix, -jnp.inf, logits)
        vals.append(mx)
        idxs.append(ix)
    es = [jnp.exp(v - vals[0]) for v in vals]
    den = es[0] + es[1] + es[2] + es[3]
    onehot = jnp.zeros((tt, LANES), F32)
    for ix in idxs:
        onehot = onehot + (lane == ix).astype(F32)
    ri = lax.broadcasted_iota(jnp.int32, (tt, tt), 0)
    ci = lax.broadcasted_iota(jnp.int32, (tt, tt), 1)
    before = jnp.dot((ri > ci).astype(BF16), onehot.astype(BF16), preferred_element_type=F32) + cnt_scr[...]
    route = jnp.zeros((tt, LANES), jnp.int32)
    pw = jnp.zeros((tt, LANES), F32)
    for kk in range(TOP_K):
        rank = jnp.sum(jnp.where(lane == idxs[kk], before, 0.0), axis=-1, keepdims=True).astype(jnp.int32)
        route = jnp.where(lane == kk, idxs[kk], route)
        route = jnp.where(lane == TOP_K + kk, rank, route)
        pw = jnp.where(lane == kk, es[kk] / den, pw)
    route_ref[...] = route
    pw_ref[...] = pw
    cnt_scr[...] = cnt_scr[...] + jnp.sum(onehot, axis=0, keepdims=True)
    cnt_ref[...] = cnt_scr[...]


def _post(lru_p, gdn_p, x_p, lru_s, gdn_s, x_s, tt, p):
    n_p = x_p.shape[0] // tt
    n_s = x_s.shape[0] // tt
    n = x_p.shape[0] + x_s.shape[0]
    prow = lambda w: pl.BlockSpec((tt, w), lambda i: (jnp.minimum(i, n_p - 1), 0))
    srow = lambda w: pl.BlockSpec((tt, w), lambda i: (jnp.maximum(i - n_p, 0), 0))
    row = lambda w: pl.BlockSpec((tt, w), lambda i: (i, 0))
    full = lambda a, b: pl.BlockSpec((a, b), lambda i: (0, 0))
    return pl.pallas_call(
        functools.partial(_post_kernel, tt=tt, n_p=n_p),
        grid=(n_p + n_s,),
        in_specs=[prow(D_LRU), prow(D_LRU), prow(D_MODEL), srow(D_LRU), srow(D_LRU), srow(D_MODEL),
                  full(D_LRU, D_MODEL), full(D_LRU, D_MODEL),
                  full(1, D_MODEL), full(D_MODEL, LANES), full(1, LANES)],
        out_specs=[row(D_MODEL), pl.BlockSpec((tt * TOKEN_ROWS, LANES), lambda i: (i, 0)), row(LANES), row(LANES),
                   full(1, LANES)],
        out_shape=[jax.ShapeDtypeStruct((n, D_MODEL), F32), jax.ShapeDtypeStruct((n * TOKEN_ROWS, LANES), F32),
                   jax.ShapeDtypeStruct((n, LANES), jnp.int32), jax.ShapeDtypeStruct((n, LANES), F32),
                   jax.ShapeDtypeStruct((1, LANES), F32)],
        scratch_shapes=[pltpu.VMEM((1, LANES), F32)],
        compiler_params=_cparams(("arbitrary",)),
        name="outproj_router",
    )(lru_p, gdn_p, x_p, lru_s, gdn_s, x_s, p["w_out1"], p["w_out2"], p["norm_ffn"], p["router_w"], p["router_b"])


def _expert_kernel(be_ref, first_ref, src_ref, srcn_ref, dstp_ref, dst_ref, x_hbm,
                   wg_ref, bg_ref, wu_ref, bu_ref, wd_ref, bd_ref, y_hbm,
                   xb0, xb1, ob0, ob1, wbf, gsem, ssem, *, blk):
    b = pl.program_id(0)
    n_blocks = pl.num_programs(0)
    xbufs, obufs = (xb0, xb1), (ob0, ob1)
    tile_rows = blk * TOKEN_ROWS

    def gather(idx_ref, s):
        for r in range(blk):
            row = pl.multiple_of(idx_ref[0, 0, r], TOKEN_ROWS)
            pltpu.make_async_copy(x_hbm.at[pl.ds(row, TOKEN_ROWS), :],
                                  xbufs[s].at[pl.ds(r * TOKEN_ROWS, TOKEN_ROWS), :], gsem.at[s]
                                  ).start(priority=r % DMA_PRIORITIES)

    def wait_gather(s):
        pltpu.make_async_copy(x_hbm.at[pl.ds(0, tile_rows), :], xbufs[s], gsem.at[s]).wait()

    def scatter(idx_ref, s):
        for r in range(blk):
            row = pl.multiple_of(idx_ref[0, 0, r], TOKEN_ROWS)
            pltpu.make_async_copy(obufs[s].at[pl.ds(r * TOKEN_ROWS, TOKEN_ROWS), :],
                                  y_hbm.at[pl.ds(row, TOKEN_ROWS), :], ssem.at[s]
                                  ).start(priority=r % DMA_PRIORITIES)

    def wait_scatter(s):
        pltpu.make_async_copy(obufs[s], y_hbm.at[pl.ds(0, tile_rows), :], ssem.at[s]).wait()

    @pl.when(b == 0)
    def _():
        gather(src_ref, 0)
        ob1[...] = jnp.zeros_like(ob1)

    @pl.when(first_ref[b] == 1)
    def _():
        wbf[0] = wg_ref[0].astype(BF16)
        wbf[1] = wu_ref[0].astype(BF16)
        wbf[2] = wd_ref[0].astype(BF16)

    def step(cur):
        oth = 1 - cur
        wait_gather(cur)

        @pl.when(b >= 1)
        def _():
            wait_scatter(cur)

        gather(srcn_ref, oth)
        scatter(dstp_ref, oth)
        xb = _load_token_tiles(xbufs[cur], 0, blk).astype(BF16)
        gt = jnp.minimum(jnp.dot(xb, wbf[0], preferred_element_type=F32) + bg_ref[0], SWIGLU_LIMIT)
        up = jnp.clip(jnp.dot(xb, wbf[1], preferred_element_type=F32) + bu_ref[0], -SWIGLU_LIMIT, SWIGLU_LIMIT)
        hid = (up + 1.0) * gt * _sigmoid(SWIGLU_ALPHA * gt)
        _store_token_tiles(obufs[cur], 0, jnp.dot(hid.astype(BF16), wbf[2], preferred_element_type=F32) + bd_ref[0])

        @pl.when(b == n_blocks - 1)
        def _():
            wait_gather(oth)
            scatter(dst_ref, cur)
            wait_scatter(oth)
            wait_scatter(cur)

    @pl.when(b % 2 == 0)
    def _():
        step(0)

    @pl.when(b % 2 == 1)
    def _():
        step(1)


def _experts(block_expert, first, src, dst, spare0, xn2, n_out_rows, p, blk):
    n_blocks = block_expert.shape[0]
    src3 = (src * TOKEN_ROWS).reshape(n_blocks, 1, blk)
    dst3 = (dst * TOKEN_ROWS).reshape(n_blocks, 1, blk)
    dstp3 = jnp.concatenate([(spare0 * TOKEN_ROWS).reshape(1, 1, blk), dst3[:-1]], axis=0)
    wspec = lambda: pl.BlockSpec((1, D_MODEL, D_MODEL), lambda b, be, fi: (be[b], 0, 0))
    bspec = lambda: pl.BlockSpec((1, 1, D_MODEL), lambda b, be, fi: (be[b], 0, 0))
    ispec = lambda f: pl.BlockSpec((1, 1, blk), f, memory_space=pltpu.SMEM)
    rows = blk * TOKEN_ROWS
    grid_spec = pltpu.PrefetchScalarGridSpec(
        num_scalar_prefetch=2,
        grid=(n_blocks,),
        in_specs=[ispec(lambda b, be, fi: (b, 0, 0)),
                  ispec(lambda b, be, fi: (jnp.minimum(b + 1, n_blocks - 1), 0, 0)),
                  ispec(lambda b, be, fi: (b, 0, 0)),
                  ispec(lambda b, be, fi: (b, 0, 0)),
                  pl.BlockSpec(memory_space=pl.ANY),
                  wspec(), bspec(), wspec(), bspec(), wspec(), bspec()],
        out_specs=pl.BlockSpec(memory_space=pl.ANY),
        scratch_shapes=[pltpu.VMEM((rows, LANES), F32), pltpu.VMEM((rows, LANES), F32),
                        pltpu.VMEM((rows, LANES), F32), pltpu.VMEM((rows, LANES), F32),
                        pltpu.VMEM((3, D_MODEL, D_MODEL), BF16),
                        pltpu.SemaphoreType.DMA((2,)), pltpu.SemaphoreType.DMA((2,))],
    )
    return pl.pallas_call(
        functools.partial(_expert_kernel, blk=blk),
        grid_spec=grid_spec,
        out_shape=jax.ShapeDtypeStruct((n_out_rows * TOKEN_ROWS, LANES), F32),
        compiler_params=_cparams(("arbitrary",)),
        name="experts",
    )(block_expert, first, src3, src3, dstp3, dst3, xn2, p["exp_w_gate"], p["exp_b_gate"], p["exp_w_up"],
      p["exp_b_up"], p["exp_w_down"], p["exp_b_down"])


def _combine_kernel(y_ref, x1_ref, pw_ref, nrm_ref, op_ref, os_ref, *, tt, n_p):
    i = pl.program_id(0)
    pw = pw_ref[...]
    moe = pw[:, 0:1] * _load_token_tiles(y_ref, 0, tt)
    for kk in range(1, TOP_K):
        moe = moe + pw[:, kk:kk + 1] * _load_token_tiles(y_ref, kk * tt, tt)
    res = _rms(x1_ref[...] + moe, nrm_ref[...])

    @pl.when(i < n_p)
    def _():
        op_ref[...] = res

    @pl.when(i >= n_p)
    def _():
        os_ref[...] = res


def _combine(yt, x1, pw, final_norm, n_prompt, tt):
    n = x1.shape[0]
    n_p = n_prompt // tt
    n_s = (n - n_prompt) // tt
    return pl.pallas_call(
        functools.partial(_combine_kernel, tt=tt, n_p=n_p),
        grid=(n_p + n_s,),
        in_specs=[pl.BlockSpec((TOP_K * tt * TOKEN_ROWS, LANES), lambda i: (i, 0)),
                  pl.BlockSpec((tt, D_MODEL), lambda i: (i, 0)),
                  pl.BlockSpec((tt, LANES), lambda i: (i, 0)),
                  pl.BlockSpec((1, D_MODEL), lambda i: (0, 0))],
        out_specs=[pl.BlockSpec((tt, D_MODEL), lambda i: (jnp.minimum(i, n_p - 1), 0)),
                   pl.BlockSpec((tt, D_MODEL), lambda i: (jnp.maximum(i - n_p, 0), 0))],
        out_shape=[jax.ShapeDtypeStruct((n_prompt, D_MODEL), F32),
                   jax.ShapeDtypeStruct((n - n_prompt, D_MODEL), F32)],
        compiler_params=_cparams(("arbitrary",)),
        name="combine",
    )(yt, x1, pw, final_norm)


def _pad_rows(a, rows):
    return jnp.pad(a, ((0, 0), (rows - a.shape[1], 0), (0, 0)))


def _tile(n, pref):
    t = pref
    while n % t:
        t //= 2
    return t


def _mixer_group(x2d, n_streams, t_len, lru_conv0, lru_h0, gdn_conv0, gdn_s0, shared_init, reset_first, chunk, p):
    tt = _tile(t_len, TOKEN_TILE)
    lru_out, u, w, qd, kd, at, z, gl, lru_conv1, lru_h1, gdn_conv1 = _pre(
        x2d, lru_conv0, lru_h0, gdn_conv0, shared_init, n_streams, t_len, tt, chunk, reset_first, p)
    sb = _tile(n_streams, REC_STREAMS)
    gdn_out, gdn_s1 = _rec(u, w, qd, kd, at, z, gl, gdn_s0, shared_init, n_streams, t_len, sb, chunk, p)
    return lru_out, gdn_out, (lru_conv1, lru_h1, gdn_conv1, gdn_s1)


def _routing(route, cnt, n_tok, tt, blk):
    idx, rank = route[:, :TOP_K], route[:, TOP_K:2 * TOP_K]
    counts = cnt[0, :N_EXPERTS].astype(jnp.int32)
    padded = (counts + blk - 1) // blk * blk
    pends = jnp.cumsum(padded)
    pstarts = pends - padded
    dest = (pstarts[idx] + rank).reshape(-1)
    m_pairs = n_tok * TOP_K
    n_blocks = (m_pairs + N_EXPERTS * (blk - 1) + blk - 1) // blk
    n_slots = n_blocks * blk
    t_id = jnp.arange(n_tok, dtype=jnp.int32)[:, None]
    k_id = jnp.arange(TOP_K, dtype=jnp.int32)[None, :]
    ypos = ((t_id // tt) * (TOP_K * tt) + k_id * tt + t_id % tt).reshape(-1)
    s_id = jnp.arange(n_slots, dtype=jnp.int32)
    spare = m_pairs + ((s_id // blk) % 2) * blk + s_id % blk
    dst = spare.at[dest].set(ypos, unique_indices=True)
    valid = dst < m_pairs
    src = jnp.where(valid, (dst // (TOP_K * tt)) * tt + dst % tt, 0)
    starts = jnp.arange(n_blocks, dtype=jnp.int32) * blk
    block_expert = jnp.minimum(jnp.sum((pends[None, :] <= starts[:, None]).astype(jnp.int32), axis=1),
                               N_EXPERTS - 1)
    first = jnp.concatenate([jnp.ones((1,), jnp.int32),
                             (block_expert[1:] != block_expert[:-1]).astype(jnp.int32)])
    spare0 = m_pairs + blk + jnp.arange(blk, dtype=jnp.int32)
    return block_expert, first, src, dst, spare0, m_pairs + 2 * blk


def kernel(x_prompt, x_sample, state_lru_conv, state_lru_h, state_gdn_conv, state_gdn_S, meta_tokens, norm_mix, w_in, lru_conv_w, lru_conv_b, lru_w_a, lru_b_a, lru_w_x, lru_b_x, lru_lambda, lru_norm, gdn_conv_w, gdn_A_log, gdn_dt_bias, gdn_norm, w_out, norm_ffn, router_w, router_b, exp_w_gate, exp_b_gate, exp_w_up, exp_b_up, exp_w_down, exp_b_down, final_norm):
    bp, tp, _ = x_prompt.shape
    bs, ts, _ = x_sample.shape
    n_meta = meta_tokens.shape[0]
    n_prompt, n_sample = bp * tp, bs * ts
    tt = TOKEN_TILE
    assert n_prompt % tt == 0 and n_sample % tt == 0

    def blockdiag(w):
        eye = jnp.eye(LRU_BLOCKS, dtype=w.dtype)
        return jnp.einsum("ncd,nm->ncmd", w, eye).reshape(D_LRU, D_LRU)

    lane_pad = lambda v: jnp.pad(v, (0, LANES - v.shape[0]))[None]
    p = dict(
        norm_mix=norm_mix[0][None],
        w_in=jnp.pad(w_in[0], ((0, 0), (0, D_IN_PAD - D_IN))).astype(BF16),
        lru_conv_w=lru_conv_w[0], lru_conv_b=lru_conv_b[0][None],
        lru_wa_bd=blockdiag(lru_w_a[0]).astype(BF16), lru_b_a=lru_b_a[0][None],
        lru_wx_bd=blockdiag(lru_w_x[0]).astype(BF16), lru_b_x=lru_b_x[0][None],
        lru_lambda=lru_lambda[0][None], lru_norm=lru_norm[0][None],
        gdn_conv_w=gdn_conv_w[0],
        gdn_par=jnp.concatenate([lane_pad(gdn_A_log[0]), lane_pad(gdn_dt_bias[0])], axis=0),
        gdn_norm=gdn_norm[0][None],
        w_out1=w_out[0][:D_LRU].astype(BF16), w_out2=w_out[0][D_LRU:].astype(BF16),
        norm_ffn=norm_ffn[0][None],
        router_w=jnp.pad(router_w[0], ((0, 0), (0, LANES - N_EXPERTS))).astype(BF16),
        router_b=lane_pad(router_b[0]),
        exp_w_gate=exp_w_gate[0], exp_b_gate=exp_b_gate[0][:, None, :],
        exp_w_up=exp_w_up[0], exp_b_up=exp_b_up[0][:, None, :],
        exp_w_down=exp_w_down[0], exp_b_down=exp_b_down[0][:, None, :],
    )

    zc = lambda c: jnp.zeros((1, SUBLANES, c), F32)
    _, _, m_state = _mixer_group(meta_tokens, 1, n_meta, zc(D_LRU), jnp.zeros((1, 1, D_LRU), F32), zc(GDN_QKV),
                                 jnp.zeros((1, GDN_HEADS, GDN_DK, GDN_DV), F32), False, True, n_meta, p)
    xp2 = x_prompt.reshape(n_prompt, D_MODEL)
    xs2 = x_sample.reshape(n_sample, D_MODEL)
    p_lru, p_gdn, p_state = _mixer_group(xp2, bp, tp, m_state[0], m_state[1], m_state[2], m_state[3], True, False,
                                         GDN_CHUNK, p)
    s_lru, s_gdn, s_state = _mixer_group(xs2, bs, ts, _pad_rows(state_lru_conv[0], SUBLANES),
                                         state_lru_h[0][:, None, :], _pad_rows(state_gdn_conv[0], SUBLANES),
                                         state_gdn_S[0], False, False, GDN_CHUNK, p)

    x1, xn2, route, pw, cnt = _post(p_lru, p_gdn, xp2, s_lru, s_gdn, xs2, tt, p)
    n_tok = n_prompt + n_sample
    block_expert, first, src, dst, spare0, n_out_rows = _routing(route, cnt, n_tok, tt, EXPERT_ROWS)
    yt = _experts(block_expert, first, src, dst, spare0, xn2, n_out_rows, p, EXPERT_ROWS)
    y_p, y_s = _combine(yt, x1, pw, final_norm[None], n_prompt, tt)

    def states(st, b):
        return (st[0][:, SUBLANES - 3:, :][None], st[1].reshape(1, b, D_LRU), st[2][:, SUBLANES - 3:, :][None],
                st[3][None])

    return (y_p.reshape(bp, tp, D_MODEL), y_s.reshape(bs, ts, D_MODEL)) + states(p_state, bp) + states(s_state, bs)
```

```python
import functools

import jax
import jax.numpy as jnp
from jax import lax
from jax.experimental import pallas as pl
from jax.experimental.pallas import tpu as pltpu

F32 = jnp.float32
BF16 = jnp.bfloat16

D_MODEL = 1024
D_LRU = 512
LRU_BLOCKS = 8
LRU_C = 8.0
CONV_W = 4
GDN_HEADS = 4
GDN_DK = 128
GDN_DV = 128
GDN_QKV = GDN_HEADS * (2 * GDN_DK + GDN_DV)
D_IN = 2 * D_LRU + GDN_QKV + GDN_HEADS * GDN_DV + 2 * GDN_HEADS
N_EXPERTS = 32
TOP_K = 4
SWIGLU_LIMIT = 7.0
SWIGLU_ALPHA = 1.702
EPS = 1e-6

LANES = 128
SUBLANES = 8
D_IN_PAD = 3200
COL_AB = (2 * D_LRU + GDN_QKV + GDN_HEADS * GDN_DV) // LANES
VMEM_LIMIT = 56 * 1024 * 1024

TOKEN_TILE = 256
EXPERT_ROWS = 256
GDN_CHUNK = 64
REC_STREAMS = 8
PRE_CHAINS = 8
TOKEN_ROWS = D_MODEL // LANES
RUN_BITS = tuple(1 << i for i in range(TOKEN_TILE.bit_length() - 1, -1, -1))


def _cparams(sem):
    return pltpu.CompilerParams(dimension_semantics=sem, vmem_limit_bytes=VMEM_LIMIT)


def _rms(x, gain):
    return x * lax.rsqrt(jnp.mean(x * x, axis=-1, keepdims=True) + EPS) * gain


def _softplus(x):
    return jnp.maximum(x, 0.0) + jnp.log1p(jnp.exp(-jnp.abs(x)))


def _sigmoid(x):
    return 1.0 / (1.0 + jnp.exp(-x))


def _load_token_tiles(ref, tok0, n_tok):
    return jnp.concatenate(
        [ref[pl.ds(tok0 * TOKEN_ROWS + j, n_tok, stride=TOKEN_ROWS), :] for j in range(TOKEN_ROWS)], axis=1)


def _store_token_tiles(ref, tok0, val):
    for j in range(TOKEN_ROWS):
        ref[pl.ds(tok0 * TOKEN_ROWS + j, val.shape[0], stride=TOKEN_ROWS), :] = val[:, j * LANES:(j + 1) * LANES]


def _mm(a, b):
    return jnp.dot(a.astype(BF16), b.astype(BF16), preferred_element_type=F32)


def _mm_nt(a, b):
    return lax.dot_general(a.astype(BF16), b.astype(BF16), (((1,), (1,)), ((), ())), preferred_element_type=F32)


def _conv_step(buf, x, w_ref, tt):
    buf[SUBLANES:SUBLANES + tt, :] = x
    y = x * w_ref[CONV_W - 1:CONV_W, :]
    for j in range(CONV_W - 2, -1, -1):
        y = y + buf[SUBLANES - 3 + j:SUBLANES - 3 + j + tt, :] * w_ref[j:j + 1, :]
    tail = buf[tt:tt + SUBLANES, :]
    buf[0:SUBLANES, :] = tail
    return y


def _pre_kernel(x_ref, nrm_ref, win_ref, lconv0_ref, h0_ref, gconv0_ref,
                lcw_ref, lcb_ref, wa_ref, ba_ref, wx_ref, bx_ref, lam_ref, lnrm_ref, gcw_ref, par_ref,
                lru_ref, u_ref, w_ref, qd_ref, kd_ref, at_ref, z_ref, gl_ref, lconv1_ref, h1_ref, gconv1_ref,
                lbuf, a_buf, b_buf, hcar, gbuf, *, tt, c, reset_first):
    t = pl.program_id(1)
    hd = GDN_HEADS * GDN_DK

    @pl.when(t == 0)
    def _():
        lbuf[0:SUBLANES, :] = lconv0_ref[0]
        hcar[...] = h0_ref[0]
        for j in range(3):
            gbuf[j, 0:SUBLANES, :] = gconv0_ref[0, :, j * hd:(j + 1) * hd]

    xn = _rms(x_ref[...], nrm_ref[...])
    proj = jnp.dot(xn.astype(BF16), win_ref[...], preferred_element_type=F32)

    xc = _conv_step(lbuf, proj[:, 0:D_LRU], lcw_ref, tt) + lcb_ref[...]
    xb = xc.astype(BF16)
    r = _sigmoid(jnp.dot(xb, wa_ref[...], preferred_element_type=F32) + ba_ref[...])
    i = _sigmoid(jnp.dot(xb, wx_ref[...], preferred_element_type=F32) + bx_ref[...])
    log_a = (-LRU_C) * r * _softplus(-lam_ref[...])
    a = jnp.exp(log_a)
    mult = jnp.sqrt(-jnp.tanh(log_a) * (a * a + 1.0))
    if reset_first:
        row = lax.broadcasted_iota(jnp.int32, (tt, D_LRU), 0)
        mult = jnp.where((row == 0) & (t == 0), 1.0, mult)
    a_buf[...] = a
    b_buf[...] = mult * i * xc
    row8 = lax.broadcasted_iota(jnp.int32, (SUBLANES, D_LRU), 0)

    def group(gi, h):
        r0 = pl.multiple_of(gi * SUBLANES, SUBLANES)
        a8 = a_buf[pl.ds(r0, SUBLANES), :]
        b8 = b_buf[pl.ds(r0, SUBLANES), :]
        for d in (1, 2, 4):
            keep = row8 >= d
            b8 = jnp.where(keep, a8 * pltpu.roll(b8, d, 0) + b8, b8)
            a8 = jnp.where(keep, a8 * pltpu.roll(a8, d, 0), a8)
        h8 = a8 * h + b8
        b_buf[pl.ds(r0, SUBLANES), :] = h8
        return h8[SUBLANES - 1:SUBLANES, :]

    h_last = lax.fori_loop(0, tt // SUBLANES, group, hcar[...])
    hcar[...] = h_last
    lru_ref[...] = _rms(b_buf[...] * jax.nn.gelu(proj[:, D_LRU:2 * D_LRU]), lnrm_ref[...]).astype(lru_ref.dtype)

    def conv_silu(j):
        y = _conv_step(gbuf.at[j], proj[:, 2 * D_LRU + j * hd:2 * D_LRU + (j + 1) * hd],
                       gcw_ref.at[:, j * hd:(j + 1) * hd], tt)
        return y * _sigmoid(y)

    qa = conv_silu(0)
    ka = conv_silu(1)
    va = conv_silu(2)
    z_ref[...] = proj[:, 2 * D_LRU + GDN_QKV:2 * D_LRU + GDN_QKV + hd]
    ab = proj[:, COL_AB * LANES:(COL_AB + 1) * LANES]
    g_all = -jnp.exp(par_ref[0:1, :]) * _softplus(ab + par_ref[1:2, :])
    beta_all = _sigmoid(ab)

    rowc = lax.broadcasted_iota(jnp.int32, (c, LANES), 0)
    ri = lax.broadcasted_iota(jnp.int32, (c, c), 0)
    ci = lax.broadcasted_iota(jnp.int32, (c, c), 1)
    causal = ri >= ci
    strict = ri > ci
    eye = ri == ci
    lane_pad = jnp.zeros((c, LANES - c), F32)

    pairs = [(j, h) for j in range(tt // c) for h in range(GDN_HEADS)]
    for p0 in range(0, len(pairs), PRE_CHAINS):
        group_pairs = pairs[p0:p0 + PRE_CHAINS]
        gcs = {}
        for j in sorted({j for j, _ in group_pairs}):
            gc = g_all[j * c:(j + 1) * c]
            d = 1
            while d < c:
                gc = gc + jnp.where(rowc >= d, pltpu.roll(gc, d, 0), 0.0)
                d *= 2
            gcs[j] = gc
        ns, xs = [], []
        for j, h in group_pairs:
            rows = slice(j * c, (j + 1) * c)
            sl = slice(h * GDN_DK, (h + 1) * GDN_DK)
            gc = gcs[j]
            q = qa[rows, sl]
            k = ka[rows, sl]
            q = q * lax.rsqrt(jnp.sum(q * q, axis=-1, keepdims=True) + EPS) * (GDN_DK ** -0.5)
            k = k * lax.rsqrt(jnp.sum(k * k, axis=-1, keepdims=True) + EPS)
            gcol = gc[:, h:h + 1]
            bcol = beta_all[rows, GDN_HEADS + h:GDN_HEADS + h + 1]
            grow = jnp.sum(jnp.where(eye, gcol, 0.0), axis=0, keepdims=True)
            decay = jnp.where(causal, jnp.exp(jnp.where(causal, gcol - grow, 0.0)), 0.0)
            egc = jnp.exp(gcol)
            g_last = gc[c - 1:c, h:h + 1]
            kb = k * bcol
            qd_ref[rows, sl] = (q * egc).astype(qd_ref.dtype)
            kd_ref[rows, sl] = (k * jnp.exp(g_last - gcol)).astype(kd_ref.dtype)
            gl_ref[j, :, sl] = jnp.broadcast_to(jnp.exp(g_last), (1, GDN_DV))
            at_ref[rows, sl] = jnp.concatenate([_mm_nt(q, k) * decay, lane_pad], axis=1).astype(at_ref.dtype)
            ns.append(jnp.where(strict, _mm_nt(kb, k) * decay, 0.0))
            xs.append(jnp.concatenate([va[rows, sl] * bcol, kb * egc], axis=1))
        xs = [x - _mm(n, x) for n, x in zip(ns, xs)]
        m = 2
        while m < c:
            ns = [_mm(n, n) for n in ns]
            xs = [x + _mm(n, x) for n, x in zip(ns, xs)]
            m *= 2
        for (j, h), x in zip(group_pairs, xs):
            rows = slice(j * c, (j + 1) * c)
            sl = slice(h * GDN_DK, (h + 1) * GDN_DK)
            u_ref[rows, sl] = x[:, :GDN_DV]
            w_ref[rows, sl] = x[:, GDN_DV:].astype(w_ref.dtype)

    @pl.when(t == pl.num_programs(1) - 1)
    def _():
        lconv1_ref[0] = lbuf[0:SUBLANES, :]
        h1_ref[0] = h_last
        for j in range(3):
            gconv1_ref[0, :, j * hd:(j + 1) * hd] = gbuf[j, 0:SUBLANES, :]


def _pre(x2d, lconv0, h0, gconv0, shared_init, n_streams, t_len, tt, c, reset_first, p):
    nt = t_len // tt
    hd = GDN_HEADS * GDN_DK
    n = n_streams * t_len
    st = (lambda s, t: (0, 0, 0)) if shared_init else (lambda s, t: (s, 0, 0))
    full = lambda a, b: pl.BlockSpec((a, b), lambda s, t: (0, 0))
    row = lambda w: pl.BlockSpec((tt, w), lambda s, t: (s * nt + t, 0))
    return pl.pallas_call(
        functools.partial(_pre_kernel, tt=tt, c=c, reset_first=reset_first),
        grid=(n_streams, nt),
        in_specs=[row(D_MODEL), full(1, D_MODEL), full(D_MODEL, D_IN_PAD),
                  pl.BlockSpec((1, SUBLANES, D_LRU), st), pl.BlockSpec((1, 1, D_LRU), st),
                  pl.BlockSpec((1, SUBLANES, GDN_QKV), st),
                  full(CONV_W, D_LRU), full(1, D_LRU), full(D_LRU, D_LRU), full(1, D_LRU), full(D_LRU, D_LRU),
                  full(1, D_LRU), full(1, D_LRU), full(1, D_LRU), full(CONV_W, GDN_QKV), full(2, LANES)],
        out_specs=[row(D_LRU), row(hd), row(hd), row(hd), row(hd), row(hd), row(hd),
                   pl.BlockSpec((tt // c, 1, hd), lambda s, t: (s * nt + t, 0, 0)),
                   pl.BlockSpec((1, SUBLANES, D_LRU), lambda s, t: (s, 0, 0)),
                   pl.BlockSpec((1, 1, D_LRU), lambda s, t: (s, 0, 0)),
                   pl.BlockSpec((1, SUBLANES, GDN_QKV), lambda s, t: (s, 0, 0))],
        out_shape=[jax.ShapeDtypeStruct((n, D_LRU), BF16),
                   jax.ShapeDtypeStruct((n, hd), F32),
                   jax.ShapeDtypeStruct((n, hd), BF16),
                   jax.ShapeDtypeStruct((n, hd), BF16),
                   jax.ShapeDtypeStruct((n, hd), BF16),
                   jax.ShapeDtypeStruct((n, hd), BF16),
                   jax.ShapeDtypeStruct((n, hd), F32),
                   jax.ShapeDtypeStruct((n // c, 1, hd), F32),
                   jax.ShapeDtypeStruct((n_streams, SUBLANES, D_LRU), F32),
                   jax.ShapeDtypeStruct((n_streams, 1, D_LRU), F32),
                   jax.ShapeDtypeStruct((n_streams, SUBLANES, GDN_QKV), F32)],
        scratch_shapes=[pltpu.VMEM((tt + SUBLANES, D_LRU), F32),
                        pltpu.VMEM((tt, D_LRU), F32),
                        pltpu.VMEM((tt, D_LRU), F32),
                        pltpu.VMEM((1, D_LRU), F32),
                        pltpu.VMEM((3, tt + SUBLANES, hd), F32)],
        compiler_params=_cparams(("arbitrary", "arbitrary")),
        name="mixer_pre",
    )(x2d, p["norm_mix"], p["w_in"], lconv0, h0, gconv0, p["lru_conv_w"], p["lru_conv_b"], p["lru_wa_bd"],
      p["lru_b_a"], p["lru_wx_bd"], p["lru_b_x"], p["lru_lambda"], p["lru_norm"], p["gdn_conv_w"], p["gdn_par"])


def _rec_kernel(u_ref, w_ref, qd_ref, kd_ref, at_ref, z_ref, gl_ref, s0_ref, nrm_ref, o_ref, s1_ref, s_scr,
                *, sb, c, shared_init):
    t = pl.program_id(1)

    @pl.when(t == 0)
    def _():
        for s in range(sb):
            s_scr[s] = s0_ref[0 if shared_init else s]

    chains = [(s, h, slice(h * GDN_DK, (h + 1) * GDN_DK)) for s in range(sb) for h in range(GDN_HEADS)]
    m1s = [jnp.dot(jnp.concatenate([w_ref[s, :, sl], qd_ref[s, :, sl]], axis=0), s_scr[s, h].astype(BF16),
                   preferred_element_type=F32) for s, h, sl in chains]
    vns = [(u_ref[s, :, sl] - m1[:c]).astype(BF16) for (s, h, sl), m1 in zip(chains, m1s)]
    for (s, h, sl), v_new in zip(chains, vns):
        s_scr[s, h] = s_scr[s, h] * gl_ref[s, 0, :, sl] + lax.dot_general(
            kd_ref[s, :, sl], v_new, (((0,), (0,)), ((), ())), preferred_element_type=F32)
    for (s, h, sl), m1, v_new in zip(chains, m1s, vns):
        o = m1[c:] + jnp.dot(at_ref[s, :, h * GDN_DK:h * GDN_DK + c], v_new, preferred_element_type=F32)
        z = z_ref[s, :, sl]
        o_ref[s, :, sl] = (_rms(o, nrm_ref[...]) * (z * _sigmoid(z))).astype(o_ref.dtype)

    @pl.when(t == pl.num_programs(1) - 1)
    def _():
        s1_ref[...] = s_scr[...]


def _rec(u, w, qd, kd, at, z, gl, s0, shared_init, n_streams, t_len, sb, c, p):
    hd = GDN_HEADS * GDN_DK
    nt = t_len // c
    v3 = lambda a: a.reshape(n_streams, t_len, hd)
    blk3 = lambda: pl.BlockSpec((sb, c, hd), lambda g, t: (g, t, 0))
    s_spec = (pl.BlockSpec((1, GDN_HEADS, GDN_DK, GDN_DV), lambda g, t: (0, 0, 0, 0)) if shared_init else
              pl.BlockSpec((sb, GDN_HEADS, GDN_DK, GDN_DV), lambda g, t: (g, 0, 0, 0)))
    out, s1 = pl.pallas_call(
        functools.partial(_rec_kernel, sb=sb, c=c, shared_init=shared_init),
        grid=(n_streams // sb, nt),
        in_specs=[blk3(), blk3(), blk3(), blk3(), blk3(), blk3(),
                  pl.BlockSpec((sb, 1, 1, hd), lambda g, t: (g, t, 0, 0)),
                  s_spec,
                  pl.BlockSpec((1, GDN_DV), lambda g, t: (0, 0))],
        out_specs=[blk3(), pl.BlockSpec((sb, GDN_HEADS, GDN_DK, GDN_DV), lambda g, t: (g, 0, 0, 0))],
        out_shape=[jax.ShapeDtypeStruct((n_streams, t_len, hd), BF16),
                   jax.ShapeDtypeStruct((n_streams, GDN_HEADS, GDN_DK, GDN_DV), F32)],
        scratch_shapes=[pltpu.VMEM((sb, GDN_HEADS, GDN_DK, GDN_DV), F32)],
        compiler_params=_cparams(("arbitrary", "arbitrary")),
        name="gdn_state",
    )(v3(u), v3(w), v3(qd), v3(kd), v3(at), v3(z), gl.reshape(n_streams, nt, 1, hd), s0, p["gdn_norm"])
    return out.reshape(n_streams * t_len, hd), s1


def _post_kernel(lru_p, gdn_p, x_p, lru_s, gdn_s, x_s, wo1_ref, wo2_ref, nrm_ref, rw_ref, rb_ref,
                 x1_ref, xn2_ref, route_ref, pw_ref, cnt_ref, *, tt, n_p):
    i = pl.program_id(0)
    is_p = i < n_p
    lru = jnp.where(is_p, lru_p[...], lru_s[...])
    gdn = jnp.where(is_p, gdn_p[...], gdn_s[...])
    x = jnp.where(is_p, x_p[...], x_s[...])
    m = (jnp.dot(lru, wo1_ref[...], preferred_element_type=F32)
         + jnp.dot(gdn, wo2_ref[...], preferred_element_type=F32))
    x1 = x + m
    x1_ref[...] = x1
    xn2 = _rms(x1, nrm_ref[...])
    _store_token_tiles(xn2_ref, 0, xn2)
    lane = lax.broadcasted_iota(jnp.int32, (tt, LANES), 1)
    logits = jnp.dot(xn2.astype(BF16), rw_ref[...], preferred_element_type=F32) + rb_ref[...]
    logits = jnp.where(lane < N_EXPERTS, logits, -jnp.inf)
    vals, idxs = [], []
    for _ in range(TOP_K):
        mx = jnp.max(logits, axis=-1, keepdims=True)
        ix = jnp.min(jnp.where(logits == mx, lane, LANES), axis=-1, keepdims=True)
        logits = jnp.where(lane == ix, -jnp.inf, logits)
        vals.append(mx)
        idxs.append(ix)
    es = [jnp.exp(v - vals[0]) for v in vals]
    den = es[0] + es[1] + es[2] + es[3]
    onehot = jnp.zeros((tt, LANES), F32)
    for ix in idxs:
        onehot = onehot + (lane == ix).astype(F32)
    ri = lax.broadcasted_iota(jnp.int32, (tt, tt), 0)
    ci = lax.broadcasted_iota(jnp.int32, (tt, tt), 1)
    cnt = jnp.sum(onehot, axis=0, keepdims=True)
    li = lax.broadcasted_iota(jnp.int32, (LANES, LANES), 0)
    lj = lax.broadcasted_iota(jnp.int32, (LANES, LANES), 1)
    off = jnp.dot(jnp.broadcast_to(cnt, (SUBLANES, LANES)).astype(BF16), (li < lj).astype(BF16),
                  preferred_element_type=F32)[0:1]
    before = jnp.dot((ri > ci).astype(BF16), onehot.astype(BF16), preferred_element_type=F32) + off
    route = jnp.zeros((tt, LANES), jnp.int32)
    pw = jnp.zeros((tt, LANES), F32)
    for kk in range(TOP_K):
        rank = jnp.sum(jnp.where(lane == idxs[kk], before, 0.0), axis=-1, keepdims=True).astype(jnp.int32)
        route = jnp.where(lane == kk, idxs[kk], route)
        route = jnp.where(lane == TOP_K + kk, rank, route)
        pw = jnp.where(lane == kk, es[kk] / den, pw)
    route_ref[...] = route
    pw_ref[...] = pw
    cnt_ref[0] = cnt.astype(jnp.int32)


def _post(lru_p, gdn_p, x_p, lru_s, gdn_s, x_s, tt, p):
    n_p = x_p.shape[0] // tt
    n_s = x_s.shape[0] // tt
    n = x_p.shape[0] + x_s.shape[0]
    prow = lambda w: pl.BlockSpec((tt, w), lambda i: (jnp.minimum(i, n_p - 1), 0))
    srow = lambda w: pl.BlockSpec((tt, w), lambda i: (jnp.maximum(i - n_p, 0), 0))
    row = lambda w: pl.BlockSpec((tt, w), lambda i: (i, 0))
    full = lambda a, b: pl.BlockSpec((a, b), lambda i: (0, 0))
    return pl.pallas_call(
        functools.partial(_post_kernel, tt=tt, n_p=n_p),
        grid=(n_p + n_s,),
        in_specs=[prow(D_LRU), prow(D_LRU), prow(D_MODEL), srow(D_LRU), srow(D_LRU), srow(D_MODEL),
                  full(D_LRU, D_MODEL), full(D_LRU, D_MODEL),
                  full(1, D_MODEL), full(D_MODEL, LANES), full(1, LANES)],
        out_specs=[row(D_MODEL), pl.BlockSpec((tt * TOKEN_ROWS, LANES), lambda i: (i, 0)), row(LANES), row(LANES),
                   pl.BlockSpec((1, 1, LANES), lambda i: (i, 0, 0))],
        out_shape=[jax.ShapeDtypeStruct((n, D_MODEL), F32), jax.ShapeDtypeStruct((n * TOKEN_ROWS, LANES), F32),
                   jax.ShapeDtypeStruct((n, LANES), jnp.int32), jax.ShapeDtypeStruct((n, LANES), F32),
                   jax.ShapeDtypeStruct((n_p + n_s, 1, LANES), jnp.int32)],
        compiler_params=_cparams(("arbitrary",)),
        name="outproj_router",
    )(lru_p, gdn_p, x_p, lru_s, gdn_s, x_s, p["w_out1"], p["w_out2"], p["norm_ffn"], p["router_w"], p["router_b"])


def _run_copies(cnt_ref, off_ref, base_ref, vbuf, hbm, sem, to_hbm):
    for e in range(N_EXPERTS):
        cnt = cnt_ref[0, 0, e]
        off = off_ref[0, 0, e]
        base = base_ref[0, 0, e]
        for bit in RUN_BITS:
            done = (cnt & (-2 * bit)) * TOKEN_ROWS

            @pl.when((cnt & bit) != 0)
            def _():
                v = vbuf.at[pl.ds(pl.multiple_of(off + done, TOKEN_ROWS), bit * TOKEN_ROWS), :]
                h = hbm.at[pl.ds(pl.multiple_of(base + done, TOKEN_ROWS), bit * TOKEN_ROWS), :]
                if to_hbm:
                    pltpu.make_async_copy(v, h, sem).start()
                else:
                    pltpu.make_async_copy(h, v, sem).start()


def _wait_run_copies(vbuf, hbm, sem, to_hbm):
    h = hbm.at[pl.ds(0, vbuf.shape[0]), :]
    if to_hbm:
        pltpu.make_async_copy(vbuf, h, sem).wait()
    else:
        pltpu.make_async_copy(h, vbuf, sem).wait()


def _dispatch_kernel(zrow_ref, zcnt_ref, nu_ref, pos_ref, cnt_ref, off_ref, base_ref, x_ref, xs_hbm,
                     dbuf0, dbuf1, zbuf, sems, zsem, *, tt, blk):
    i = pl.program_id(0)
    n = pl.num_programs(0)
    dbufs = (dbuf0, dbuf1)

    def pad_copies(wait):
        for e in range(N_EXPERTS):
            cnt = zcnt_ref[e]
            row = zrow_ref[e]
            for bit in RUN_BITS:
                done = (cnt & (-2 * bit)) * TOKEN_ROWS

                @pl.when((cnt & bit) != 0)
                def _():
                    cp = pltpu.make_async_copy(
                        zbuf.at[pl.ds(0, bit * TOKEN_ROWS), :],
                        xs_hbm.at[pl.ds(pl.multiple_of(row + done, TOKEN_ROWS), bit * TOKEN_ROWS), :], zsem.at[0])
                    if wait:
                        cp.wait()
                    else:
                        cp.start()
        n_blocks = xs_hbm.shape[0] // (blk * TOKEN_ROWS)
        for j in range(N_EXPERTS):
            @pl.when(nu_ref[0] + j < n_blocks)
            def _():
                row = pl.multiple_of((nu_ref[0] + j) * (blk * TOKEN_ROWS), TOKEN_ROWS)
                cp = pltpu.make_async_copy(zbuf, xs_hbm.at[pl.ds(row, blk * TOKEN_ROWS), :], zsem.at[0])
                if wait:
                    cp.wait()
                else:
                    cp.start()

    @pl.when(i == 0)
    def _():
        zbuf[...] = jnp.zeros_like(zbuf)
        pad_copies(False)

    def step(cur):
        dbuf = dbufs[cur]

        @pl.when(i >= 2)
        def _():
            _wait_run_copies(dbuf, xs_hbm, sems.at[cur], True)

        for t in range(tt):
            v = x_ref[t * TOKEN_ROWS:(t + 1) * TOKEN_ROWS, :]
            for kk in range(TOP_K):
                dbuf[pl.ds(pl.multiple_of(pos_ref[0, 0, t * TOP_K + kk], TOKEN_ROWS), TOKEN_ROWS), :] = v
        _run_copies(cnt_ref, off_ref, base_ref, dbuf, xs_hbm, sems.at[cur], True)

        @pl.when(i == n - 1)
        def _():
            _wait_run_copies(dbuf, xs_hbm, sems.at[cur], True)

            @pl.when(i >= 1)
            def _():
                _wait_run_copies(dbufs[1 - cur], xs_hbm, sems.at[1 - cur], True)

            pad_copies(True)

    @pl.when(i % 2 == 0)
    def _():
        step(0)

    @pl.when(i % 2 == 1)
    def _():
        step(1)


def _dispatch(zrow, zcnt, n_used, pos, cnt_t, off_t, base_t, xn2, n_slots, tt, blk):
    n_tiles = cnt_t.shape[0]
    smem = lambda w: pl.BlockSpec((1, 1, w), lambda i, zr, zc, nu: (i, 0, 0), memory_space=pltpu.SMEM)
    rows = TOP_K * tt * TOKEN_ROWS
    grid_spec = pltpu.PrefetchScalarGridSpec(
        num_scalar_prefetch=3,
        grid=(n_tiles,),
        in_specs=[smem(TOP_K * tt), smem(LANES), smem(LANES), smem(LANES),
                  pl.BlockSpec((tt * TOKEN_ROWS, LANES), lambda i, zr, zc, nu: (i, 0))],
        out_specs=pl.BlockSpec(memory_space=pl.ANY),
        scratch_shapes=[pltpu.VMEM((rows, LANES), F32), pltpu.VMEM((rows, LANES), F32),
                        pltpu.VMEM((blk * TOKEN_ROWS, LANES), F32),
                        pltpu.SemaphoreType.DMA((2,)), pltpu.SemaphoreType.DMA((1,))],
    )
    return pl.pallas_call(
        functools.partial(_dispatch_kernel, tt=tt, blk=blk),
        grid_spec=grid_spec,
        out_shape=jax.ShapeDtypeStruct((n_slots * TOKEN_ROWS, LANES), F32),
        compiler_params=_cparams(("arbitrary",)),
        name="moe_dispatch",
    )(zrow, zcnt, n_used, pos, cnt_t, off_t, base_t, xn2)


def _expert_kernel(be_ref, first_ref, nu_ref, x_ref, wg_ref, bg_ref, wu_ref, bu_ref, wd_ref, bd_ref, y_ref, wbf,
                   *, blk):
    b = pl.program_id(0)

    @pl.when((first_ref[b] == 1) & (b < nu_ref[0]))
    def _():
        wbf[0] = wg_ref[0].astype(BF16)
        wbf[1] = wu_ref[0].astype(BF16)
        wbf[2] = wd_ref[0].astype(BF16)

    @pl.when(b < nu_ref[0])
    def _():
        xb = _load_token_tiles(x_ref, 0, blk).astype(BF16)
        gt = jnp.minimum(jnp.dot(xb, wbf[0], preferred_element_type=F32) + bg_ref[0], SWIGLU_LIMIT)
        up = jnp.clip(jnp.dot(xb, wbf[1], preferred_element_type=F32) + bu_ref[0], -SWIGLU_LIMIT, SWIGLU_LIMIT)
        hid = (up + 1.0) * gt * _sigmoid(SWIGLU_ALPHA * gt)
        _store_token_tiles(y_ref, 0, jnp.dot(hid.astype(BF16), wbf[2], preferred_element_type=F32) + bd_ref[0])

    @pl.when(b >= nu_ref[0])
    def _():
        y_ref[...] = jnp.zeros_like(y_ref)


def _experts(block_expert, first, n_used, xs, p, blk):
    n_blocks = block_expert.shape[0]
    used = lambda b, nu: jnp.minimum(b, nu[0] - 1)
    wspec = lambda: pl.BlockSpec((1, D_MODEL, D_MODEL), lambda b, be, fi, nu: (be[used(b, nu)], 0, 0))
    bspec = lambda: pl.BlockSpec((1, 1, D_MODEL), lambda b, be, fi, nu: (be[used(b, nu)], 0, 0))
    rspec = lambda: pl.BlockSpec((blk * TOKEN_ROWS, LANES), lambda b, be, fi, nu: (used(b, nu), 0))
    grid_spec = pltpu.PrefetchScalarGridSpec(
        num_scalar_prefetch=3,
        grid=(n_blocks,),
        in_specs=[rspec(), wspec(), bspec(), wspec(), bspec(), wspec(), bspec()],
        out_specs=pl.BlockSpec((blk * TOKEN_ROWS, LANES), lambda b, be, fi, nu: (b, 0)),
        scratch_shapes=[pltpu.VMEM((3, D_MODEL, D_MODEL), BF16)],
    )
    return pl.pallas_call(
        functools.partial(_expert_kernel, blk=blk),
        grid_spec=grid_spec,
        out_shape=jax.ShapeDtypeStruct(xs.shape, F32),
        compiler_params=_cparams(("arbitrary",)),
        name="experts",
    )(block_expert, first, n_used, xs, p["exp_w_gate"], p["exp_b_gate"], p["exp_w_up"], p["exp_b_up"],
      p["exp_w_down"], p["exp_b_down"])


def _combine_kernel(pos_ref, pw_ref, cnt_ref, off_ref, base_ref, cntn_ref, offn_ref, basen_ref,
                    ys_hbm, x1_ref, nrm_ref, op_ref, os_ref, ybuf0, ybuf1, mbuf, sems, *, tt, n_p):
    i = pl.program_id(0)
    n = pl.num_programs(0)
    ybufs = (ybuf0, ybuf1)

    @pl.when(i == 0)
    def _():
        _run_copies(cnt_ref, off_ref, base_ref, ybuf0, ys_hbm, sems.at[0], False)

    def step(cur):
        @pl.when(i + 1 < n)
        def _():
            _run_copies(cntn_ref, offn_ref, basen_ref, ybufs[1 - cur], ys_hbm, sems.at[1 - cur], False)

        _wait_run_copies(ybufs[cur], ys_hbm, sems.at[cur], False)
        ybuf = ybufs[cur]
        for t in range(tt):
            acc = None
            for kk in range(TOP_K):
                j = t * TOP_K + kk
                row = ybuf[pl.ds(pl.multiple_of(pos_ref[0, 0, j], TOKEN_ROWS), TOKEN_ROWS), :]
                term = pw_ref[0, 0, j] * row
                acc = term if acc is None else acc + term
            mbuf[t * TOKEN_ROWS:(t + 1) * TOKEN_ROWS, :] = acc

    @pl.when(i % 2 == 0)
    def _():
        step(0)

    @pl.when(i % 2 == 1)
    def _():
        step(1)

    res = _rms(x1_ref[...] + _load_token_tiles(mbuf, 0, tt), nrm_ref[...])

    @pl.when(i < n_p)
    def _():
        op_ref[...] = res

    @pl.when(i >= n_p)
    def _():
        os_ref[...] = res


def _combine(pos, pw, cnt_t, off_t, base_t, ys, x1, final_norm, n_prompt, tt):
    n = x1.shape[0]
    n_p = n_prompt // tt
    n_s = (n - n_prompt) // tt
    nt = n_p + n_s
    smem = lambda w: pl.BlockSpec((1, 1, w), lambda i: (i, 0, 0), memory_space=pltpu.SMEM)
    smem_next = lambda w: pl.BlockSpec((1, 1, w), lambda i: (jnp.minimum(i + 1, nt - 1), 0, 0),
                                       memory_space=pltpu.SMEM)
    rows = TOP_K * tt * TOKEN_ROWS
    return pl.pallas_call(
        functools.partial(_combine_kernel, tt=tt, n_p=n_p),
        grid=(nt,),
        in_specs=[smem(TOP_K * tt), smem(TOP_K * tt), smem(LANES), smem(LANES), smem(LANES),
                  smem_next(LANES), smem_next(LANES), smem_next(LANES),
                  pl.BlockSpec(memory_space=pl.ANY),
                  pl.BlockSpec((tt, D_MODEL), lambda i: (i, 0)),
                  pl.BlockSpec((1, D_MODEL), lambda i: (0, 0))],
        out_specs=[pl.BlockSpec((tt, D_MODEL), lambda i: (jnp.minimum(i, n_p - 1), 0)),
                   pl.BlockSpec((tt, D_MODEL), lambda i: (jnp.maximum(i - n_p, 0), 0))],
        out_shape=[jax.ShapeDtypeStruct((n_prompt, D_MODEL), F32),
                   jax.ShapeDtypeStruct((n - n_prompt, D_MODEL), F32)],
        scratch_shapes=[pltpu.VMEM((rows, LANES), F32), pltpu.VMEM((rows, LANES), F32),
                        pltpu.VMEM((tt * TOKEN_ROWS, LANES), F32), pltpu.SemaphoreType.DMA((2,))],
        compiler_params=_cparams(("arbitrary",)),
        name="moe_combine",
    )(pos, pw, cnt_t, off_t, base_t, cnt_t, off_t, base_t, ys, x1, final_norm)


def _pad_rows(a, rows):
    return jnp.pad(a, ((0, 0), (rows - a.shape[1], 0), (0, 0)))


def _tile(n, pref):
    t = pref
    while n % t:
        t //= 2
    return t


def _mixer_group(x2d, n_streams, t_len, lru_conv0, lru_h0, gdn_conv0, gdn_s0, shared_init, reset_first, chunk, p):
    tt = _tile(t_len, TOKEN_TILE)
    lru_out, u, w, qd, kd, at, z, gl, lru_conv1, lru_h1, gdn_conv1 = _pre(
        x2d, lru_conv0, lru_h0, gdn_conv0, shared_init, n_streams, t_len, tt, chunk, reset_first, p)
    sb = _tile(n_streams, REC_STREAMS)
    gdn_out, gdn_s1 = _rec(u, w, qd, kd, at, z, gl, gdn_s0, shared_init, n_streams, t_len, sb, chunk, p)
    return lru_out, gdn_out, (lru_conv1, lru_h1, gdn_conv1, gdn_s1)


def _routing(cnt_tiles, n_tok, blk):
    cnt_t = cnt_tiles[:, 0, :]
    counts = jnp.sum(cnt_t, axis=0)
    padded = (counts + blk - 1) // blk * blk
    pends = jnp.cumsum(padded)
    pstarts = pends - padded
    off_t = jnp.cumsum(cnt_t, axis=1) - cnt_t
    base_t = pstarts[None, :] + jnp.cumsum(cnt_t, axis=0) - cnt_t
    n_blocks = (n_tok * TOP_K + N_EXPERTS * (blk - 1) + blk - 1) // blk
    starts = jnp.arange(n_blocks, dtype=jnp.int32) * blk
    block_expert = jnp.minimum(jnp.sum((pends[None, :N_EXPERTS] <= starts[:, None]).astype(jnp.int32), axis=1),
                               N_EXPERTS - 1)
    first = jnp.concatenate([jnp.ones((1,), jnp.int32),
                             (block_expert[1:] != block_expert[:-1]).astype(jnp.int32)])
    n_used = (pends[N_EXPERTS - 1] // blk).astype(jnp.int32)[None]
    r3 = lambda a: (a * TOKEN_ROWS).astype(jnp.int32)[:, None, :]
    zrow = ((pstarts + counts) * TOKEN_ROWS).astype(jnp.int32)[:N_EXPERTS]
    zcnt = (padded - counts).astype(jnp.int32)[:N_EXPERTS]
    return (cnt_t.astype(jnp.int32)[:, None, :], r3(off_t), r3(base_t), block_expert, first, n_used, zrow, zcnt,
            n_blocks * blk)


def kernel(x_prompt, x_sample, state_lru_conv, state_lru_h, state_gdn_conv, state_gdn_S, meta_tokens, norm_mix, w_in, lru_conv_w, lru_conv_b, lru_w_a, lru_b_a, lru_w_x, lru_b_x, lru_lambda, lru_norm, gdn_conv_w, gdn_A_log, gdn_dt_bias, gdn_norm, w_out, norm_ffn, router_w, router_b, exp_w_gate, exp_b_gate, exp_w_up, exp_b_up, exp_w_down, exp_b_down, final_norm):
    bp, tp, _ = x_prompt.shape
    bs, ts, _ = x_sample.shape
    n_meta = meta_tokens.shape[0]
    n_prompt, n_sample = bp * tp, bs * ts
    tt = TOKEN_TILE
    assert n_prompt % tt == 0 and n_sample % tt == 0 and EXPERT_ROWS <= TOKEN_TILE

    def blockdiag(w):
        eye = jnp.eye(LRU_BLOCKS, dtype=w.dtype)
        return jnp.einsum("ncd,nm->ncmd", w, eye).reshape(D_LRU, D_LRU)

    lane_pad = lambda v: jnp.pad(v, (0, LANES - v.shape[0]))[None]
    p = dict(
        norm_mix=norm_mix[0][None],
        w_in=jnp.pad(w_in[0], ((0, 0), (0, D_IN_PAD - D_IN))).astype(BF16),
        lru_conv_w=lru_conv_w[0], lru_conv_b=lru_conv_b[0][None],
        lru_wa_bd=blockdiag(lru_w_a[0]).astype(BF16), lru_b_a=lru_b_a[0][None],
        lru_wx_bd=blockdiag(lru_w_x[0]).astype(BF16), lru_b_x=lru_b_x[0][None],
        lru_lambda=lru_lambda[0][None], lru_norm=lru_norm[0][None],
        gdn_conv_w=gdn_conv_w[0],
        gdn_par=jnp.concatenate([lane_pad(gdn_A_log[0]), lane_pad(gdn_dt_bias[0])], axis=0),
        gdn_norm=gdn_norm[0][None],
        w_out1=w_out[0][:D_LRU].astype(BF16), w_out2=w_out[0][D_LRU:].astype(BF16),
        norm_ffn=norm_ffn[0][None],
        router_w=jnp.pad(router_w[0], ((0, 0), (0, LANES - N_EXPERTS))).astype(BF16),
        router_b=lane_pad(router_b[0]),
        exp_w_gate=exp_w_gate[0], exp_b_gate=exp_b_gate[0][:, None, :],
        exp_w_up=exp_w_up[0], exp_b_up=exp_b_up[0][:, None, :],
        exp_w_down=exp_w_down[0], exp_b_down=exp_b_down[0][:, None, :],
    )

    zc = lambda c: jnp.zeros((1, SUBLANES, c), F32)
    _, _, m_state = _mixer_group(meta_tokens, 1, n_meta, zc(D_LRU), jnp.zeros((1, 1, D_LRU), F32), zc(GDN_QKV),
                                 jnp.zeros((1, GDN_HEADS, GDN_DK, GDN_DV), F32), False, True, n_meta, p)
    xp2 = x_prompt.reshape(n_prompt, D_MODEL)
    xs2 = x_sample.reshape(n_sample, D_MODEL)
    p_lru, p_gdn, p_state = _mixer_group(xp2, bp, tp, m_state[0], m_state[1], m_state[2], m_state[3], True, False,
                                         GDN_CHUNK, p)
    s_lru, s_gdn, s_state = _mixer_group(xs2, bs, ts, _pad_rows(state_lru_conv[0], SUBLANES),
                                         state_lru_h[0][:, None, :], _pad_rows(state_gdn_conv[0], SUBLANES),
                                         state_gdn_S[0], False, False, GDN_CHUNK, p)

    x1, xn2, route, pw, cnt_tiles = _post(p_lru, p_gdn, xp2, s_lru, s_gdn, xs2, tt, p)
    n_tok = n_prompt + n_sample
    cnt_t, off_t, base_t, block_expert, first, n_used, zrow, zcnt, n_slots = _routing(cnt_tiles, n_tok, EXPERT_ROWS)
    n_tiles = n_tok // tt
    pos = (route[:, TOP_K:2 * TOP_K] * TOKEN_ROWS).reshape(n_tiles, 1, TOP_K * tt)
    pw_s = pw[:, :TOP_K].reshape(n_tiles, 1, TOP_K * tt)
    xs = _dispatch(zrow, zcnt, n_used, pos, cnt_t, off_t, base_t, xn2, n_slots, tt, EXPERT_ROWS)
    ys = _experts(block_expert, first, n_used, xs, p, EXPERT_ROWS)
    y_p, y_s = _combine(pos, pw_s, cnt_t, off_t, base_t, ys, x1, final_norm[None], n_prompt, tt)

    def states(st, b):
        return (st[0][:, SUBLANES - 3:, :][None], st[1].reshape(1, b, D_LRU), st[2][:, SUBLANES - 3:, :][None],
                st[3][None])

    return (y_p.reshape(bp, tp, D_MODEL), y_s.reshape(bs, ts, D_MODEL)) + states(p_state, bp) + states(s_state, bs)
```

```python
import functools

import jax
import jax.numpy as jnp
from jax import lax
from jax.experimental import pallas as pl
from jax.experimental.pallas import tpu as pltpu

F32 = jnp.float32
BF16 = jnp.bfloat16

D_MODEL = 1024
D_LRU = 512
LRU_BLOCKS = 8
LRU_C = 8.0
CONV_W = 4
GDN_HEADS = 4
GDN_DK = 128
GDN_DV = 128
GDN_QKV = GDN_HEADS * (2 * GDN_DK + GDN_DV)
D_IN = 2 * D_LRU + GDN_QKV + GDN_HEADS * GDN_DV + 2 * GDN_HEADS
N_EXPERTS = 32
TOP_K = 4
SWIGLU_LIMIT = 7.0
SWIGLU_ALPHA = 1.702
EPS = 1e-6

LANES = 128
SUBLANES = 8
D_IN_PAD = 3200
COL_AB = (2 * D_LRU + GDN_QKV + GDN_HEADS * GDN_DV) // LANES
VMEM_LIMIT = 56 * 1024 * 1024

TOKEN_TILE = 256
EXPERT_ROWS = 512
GDN_CHUNK = 64
REC_STREAMS = 8
PRE_CHAINS = 8
PROJ_PIECE = 256
PROJ_EARLY_POINTS = 8
TOKEN_ROWS = D_MODEL // LANES
RUN_BITS = tuple(1 << i for i in range(TOKEN_TILE.bit_length() - 1, -1, -1))


def _cparams(sem):
    return pltpu.CompilerParams(dimension_semantics=sem, vmem_limit_bytes=VMEM_LIMIT)


def _rms(x, gain):
    return x * lax.rsqrt(jnp.mean(x * x, axis=-1, keepdims=True) + EPS) * gain


def _softplus(x):
    return jnp.maximum(x, 0.0) + jnp.log1p(jnp.exp(-jnp.abs(x)))


def _sigmoid(x):
    return 1.0 / (1.0 + jnp.exp(-x))


def _load_token_tiles(ref, tok0, n_tok):
    return jnp.concatenate(
        [ref[pl.ds(tok0 * TOKEN_ROWS + j, n_tok, stride=TOKEN_ROWS), :] for j in range(TOKEN_ROWS)], axis=1)


def _store_token_tiles(ref, tok0, val):
    for j in range(TOKEN_ROWS):
        ref[pl.ds(tok0 * TOKEN_ROWS + j, val.shape[0], stride=TOKEN_ROWS), :] = val[:, j * LANES:(j + 1) * LANES]


def _mm(a, b):
    return jnp.dot(a.astype(BF16), b.astype(BF16), preferred_element_type=F32)


def _mm_nt(a, b):
    return lax.dot_general(a.astype(BF16), b.astype(BF16), (((1,), (1,)), ((), ())), preferred_element_type=F32)


def _conv_step(buf, x, w_ref, tt):
    buf[SUBLANES:SUBLANES + tt, :] = x
    y = x * w_ref[CONV_W - 1:CONV_W, :]
    for j in range(CONV_W - 2, -1, -1):
        y = y + buf[SUBLANES - 3 + j:SUBLANES - 3 + j + tt, :] * w_ref[j:j + 1, :]
    tail = buf[tt:tt + SUBLANES, :]
    buf[0:SUBLANES, :] = tail
    return y


def _pre_kernel(x_ref, xnext_ref, nrm_ref, win_ref, lconv0_ref, h0_ref, gconv0_ref,
                lcw_ref, lcb_ref, wa_ref, ba_ref, wx_ref, bx_ref, lam_ref, lnrm_ref, gcw_ref, par_ref,
                lru_ref, u_ref, w_ref, qd_ref, kd_ref, at_ref, z_ref, gl_ref, lconv1_ref, h1_ref, gconv1_ref,
                lbuf, a_buf, b_buf, hcar, gbuf, pbuf, *, tt, c, reset_first):
    t = pl.program_id(1)
    hd = GDN_HEADS * GDN_DK

    @pl.when(t == 0)
    def _():
        lbuf[0:SUBLANES, :] = lconv0_ref[0]
        hcar[...] = h0_ref[0]
        for j in range(3):
            gbuf[j, 0:SUBLANES, :] = gconv0_ref[0, :, j * hd:(j + 1) * hd]

    @pl.when((pl.program_id(0) == 0) & (t == 0))
    def _():
        pbuf[...] = jnp.dot(_rms(x_ref[...], nrm_ref[...]).astype(BF16), win_ref[...], preferred_element_type=F32)

    th = tt
    projs = [pbuf]
    xnext = _rms(xnext_ref[...], nrm_ref[...]).astype(BF16)
    pieces = [(lo, min(lo + PROJ_PIECE, D_IN_PAD)) for lo in range(0, D_IN_PAD, PROJ_PIECE)]
    z_col = 2 * D_LRU + GDN_QKV
    assert z_col % PROJ_PIECE == 0
    pieces = [pc for pc in pieces if pc[0] >= z_col] + [pc for pc in pieces if pc[0] < z_col]
    n_levels = max(c.bit_length() - 1, 1)
    n_solve_points = -(-(tt // c) * GDN_HEADS // PRE_CHAINS) * n_levels
    n_late = max(len(pieces) - PROJ_EARLY_POINTS, 0)
    plan = [1] * PROJ_EARLY_POINTS + [(i + 1) * n_late // n_solve_points - i * n_late // n_solve_points
                                      for i in range(n_solve_points)]
    points_done = [0]
    pieces_done = [0]

    def next_proj_pieces():
        k = plan[points_done[0]]
        points_done[0] += 1
        for lo, hi in pieces[pieces_done[0]:pieces_done[0] + k]:
            pbuf[:, lo:hi] = jnp.dot(xnext, win_ref[:, lo:hi], preferred_element_type=F32)
        pieces_done[0] += k

    rowc = lax.broadcasted_iota(jnp.int32, (c, LANES), 0)
    ri = lax.broadcasted_iota(jnp.int32, (c, c), 0)
    ci = lax.broadcasted_iota(jnp.int32, (c, c), 1)
    causal = ri >= ci
    strict = ri > ci
    eye = ri == ci
    lane_pad = jnp.zeros((c, LANES - c), F32)
    row8 = lax.broadcasted_iota(jnp.int32, (SUBLANES, D_LRU), 0)

    def group(gi, h):
        g0 = pl.multiple_of(gi * SUBLANES, SUBLANES)
        a8 = a_buf[pl.ds(g0, SUBLANES), :]
        b8 = b_buf[pl.ds(g0, SUBLANES), :]
        for d in (1, 2, 4):
            keep = row8 >= d
            b8 = jnp.where(keep, a8 * pltpu.roll(b8, d, 0) + b8, b8)
            a8 = jnp.where(keep, a8 * pltpu.roll(a8, d, 0), a8)
        h8 = a8 * h + b8
        b_buf[pl.ds(g0, SUBLANES), :] = h8
        return h8[SUBLANES - 1:SUBLANES, :]

    h_last = hcar[...]
    for part, proj in enumerate(projs):
        r0 = part * th

        z_ref[r0:r0 + th, :] = proj[:, 2 * D_LRU + GDN_QKV:2 * D_LRU + GDN_QKV + hd]
        ab = proj[:, COL_AB * LANES:(COL_AB + 1) * LANES]
        g_all = -jnp.exp(par_ref[0:1, :]) * _softplus(ab + par_ref[1:2, :])
        beta_all = _sigmoid(ab)
        next_proj_pieces()

        xc = _conv_step(lbuf, proj[:, 0:D_LRU], lcw_ref, th) + lcb_ref[...]
        next_proj_pieces()
        xb = xc.astype(BF16)
        r = _sigmoid(jnp.dot(xb, wa_ref[...], preferred_element_type=F32) + ba_ref[...])
        i = _sigmoid(jnp.dot(xb, wx_ref[...], preferred_element_type=F32) + bx_ref[...])
        next_proj_pieces()
        log_a = (-LRU_C) * r * _softplus(-lam_ref[...])
        a = jnp.exp(log_a)
        mult = jnp.sqrt(-jnp.tanh(log_a) * (a * a + 1.0))
        if reset_first and part == 0:
            row = lax.broadcasted_iota(jnp.int32, (th, D_LRU), 0)
            mult = jnp.where((row == 0) & (t == 0), 1.0, mult)
        a_buf[0:th, :] = a
        b_buf[0:th, :] = mult * i * xc
        next_proj_pieces()
        h_last = lax.fori_loop(0, th // SUBLANES, group, h_last)
        lru_ref[r0:r0 + th, :] = _rms(b_buf[0:th, :] * jax.nn.gelu(proj[:, D_LRU:2 * D_LRU]),
                                      lnrm_ref[...]).astype(lru_ref.dtype)
        next_proj_pieces()

        def conv_silu(j, proj=proj):
            y = _conv_step(gbuf.at[j], proj[:, 2 * D_LRU + j * hd:2 * D_LRU + (j + 1) * hd],
                           gcw_ref.at[:, j * hd:(j + 1) * hd], th)
            return y * _sigmoid(y)

        qa = conv_silu(0)
        next_proj_pieces()
        ka = conv_silu(1)
        next_proj_pieces()
        va = conv_silu(2)
        next_proj_pieces()

        pairs = [(j, h) for j in range(th // c) for h in range(GDN_HEADS)]
        for p0 in range(0, len(pairs), PRE_CHAINS):
            group_pairs = pairs[p0:p0 + PRE_CHAINS]
            gcs = {}
            for j in sorted({j for j, _ in group_pairs}):
                gc = g_all[j * c:(j + 1) * c]
                d = 1
                while d < c:
                    gc = gc + jnp.where(rowc >= d, pltpu.roll(gc, d, 0), 0.0)
                    d *= 2
                gcs[j] = gc
            ns, xs = [], []
            for j, h in group_pairs:
                rows = slice(j * c, (j + 1) * c)
                orow = slice(r0 + j * c, r0 + (j + 1) * c)
                sl = slice(h * GDN_DK, (h + 1) * GDN_DK)
                gc = gcs[j]
                q = qa[rows, sl]
                k = ka[rows, sl]
                q = q * lax.rsqrt(jnp.sum(q * q, axis=-1, keepdims=True) + EPS) * (GDN_DK ** -0.5)
                k = k * lax.rsqrt(jnp.sum(k * k, axis=-1, keepdims=True) + EPS)
                gcol = gc[:, h:h + 1]
                bcol = beta_all[rows, GDN_HEADS + h:GDN_HEADS + h + 1]
                grow = jnp.sum(jnp.where(eye, gcol, 0.0), axis=0, keepdims=True)
                decay = jnp.where(causal, jnp.exp(jnp.where(causal, gcol - grow, 0.0)), 0.0)
                egc = jnp.exp(gcol)
                g_last = gc[c - 1:c, h:h + 1]
                kb = k * bcol
                qd_ref[orow, sl] = (q * egc).astype(qd_ref.dtype)
                kd_ref[orow, sl] = (k * jnp.exp(g_last - gcol)).astype(kd_ref.dtype)
                gl_ref[r0 // c + j, :, sl] = jnp.broadcast_to(jnp.exp(g_last), (1, GDN_DV))
                at_ref[orow, sl] = jnp.concatenate([_mm_nt(q, k) * decay, lane_pad], axis=1).astype(at_ref.dtype)
                ns.append(jnp.where(strict, _mm_nt(kb, k) * decay, 0.0))
                xs.append(jnp.concatenate([va[rows, sl] * bcol, kb * egc], axis=1))
            xs = [x - _mm(n, x) for n, x in zip(ns, xs)]
            next_proj_pieces()
            m = 2
            while m < c:
                ns = [_mm(n, n) for n in ns]
                xs = [x + _mm(n, x) for n, x in zip(ns, xs)]
                next_proj_pieces()
                m *= 2
            for (j, h), x in zip(group_pairs, xs):
                orow = slice(r0 + j * c, r0 + (j + 1) * c)
                sl = slice(h * GDN_DK, (h + 1) * GDN_DK)
                u_ref[orow, sl] = x[:, :GDN_DV]
                w_ref[orow, sl] = x[:, GDN_DV:].astype(w_ref.dtype)
    hcar[...] = h_last
    assert points_done[0] == len(plan) and pieces_done[0] == len(pieces)

    @pl.when(t == pl.num_programs(1) - 1)
    def _():
        lconv1_ref[0] = lbuf[0:SUBLANES, :]
        h1_ref[0] = h_last
        for j in range(3):
            gconv1_ref[0, :, j * hd:(j + 1) * hd] = gbuf[j, 0:SUBLANES, :]


def _pre(x2d, lconv0, h0, gconv0, shared_init, n_streams, t_len, tt, c, reset_first, p):
    nt = t_len // tt
    hd = GDN_HEADS * GDN_DK
    n = n_streams * t_len
    st = (lambda s, t: (0, 0, 0)) if shared_init else (lambda s, t: (s, 0, 0))
    full = lambda a, b: pl.BlockSpec((a, b), lambda s, t: (0, 0))
    row = lambda w: pl.BlockSpec((tt, w), lambda s, t: (s * nt + t, 0))
    return pl.pallas_call(
        functools.partial(_pre_kernel, tt=tt, c=c, reset_first=reset_first),
        grid=(n_streams, nt),
        in_specs=[row(D_MODEL),
                  pl.BlockSpec((tt, D_MODEL), lambda s, t: (jnp.minimum(s * nt + t + 1, n_streams * nt - 1), 0)),
                  full(1, D_MODEL), full(D_MODEL, D_IN_PAD),
                  pl.BlockSpec((1, SUBLANES, D_LRU), st), pl.BlockSpec((1, 1, D_LRU), st),
                  pl.BlockSpec((1, SUBLANES, GDN_QKV), st),
                  full(CONV_W, D_LRU), full(1, D_LRU), full(D_LRU, D_LRU), full(1, D_LRU), full(D_LRU, D_LRU),
                  full(1, D_LRU), full(1, D_LRU), full(1, D_LRU), full(CONV_W, GDN_QKV), full(2, LANES)],
        out_specs=[row(D_LRU), row(hd), row(hd), row(hd), row(hd), row(hd), row(hd),
                   pl.BlockSpec((tt // c, 1, hd), lambda s, t: (s * nt + t, 0, 0)),
                   pl.BlockSpec((1, SUBLANES, D_LRU), lambda s, t: (s, 0, 0)),
                   pl.BlockSpec((1, 1, D_LRU), lambda s, t: (s, 0, 0)),
                   pl.BlockSpec((1, SUBLANES, GDN_QKV), lambda s, t: (s, 0, 0))],
        out_shape=[jax.ShapeDtypeStruct((n, D_LRU), BF16),
                   jax.ShapeDtypeStruct((n, hd), F32),
                   jax.ShapeDtypeStruct((n, hd), BF16),
                   jax.ShapeDtypeStruct((n, hd), BF16),
                   jax.ShapeDtypeStruct((n, hd), BF16),
                   jax.ShapeDtypeStruct((n, hd), BF16),
                   jax.ShapeDtypeStruct((n, hd), F32),
                   jax.ShapeDtypeStruct((n // c, 1, hd), F32),
                   jax.ShapeDtypeStruct((n_streams, SUBLANES, D_LRU), F32),
                   jax.ShapeDtypeStruct((n_streams, 1, D_LRU), F32),
                   jax.ShapeDtypeStruct((n_streams, SUBLANES, GDN_QKV), F32)],
        scratch_shapes=[pltpu.VMEM((tt + SUBLANES, D_LRU), F32),
                        pltpu.VMEM((tt, D_LRU), F32),
                        pltpu.VMEM((tt, D_LRU), F32),
                        pltpu.VMEM((1, D_LRU), F32),
                        pltpu.VMEM((3, tt + SUBLANES, hd), F32),
                        pltpu.VMEM((tt, D_IN_PAD), F32)],
        compiler_params=_cparams(("arbitrary", "arbitrary")),
        name="mixer_pre",
    )(x2d, x2d, p["norm_mix"], p["w_in"], lconv0, h0, gconv0, p["lru_conv_w"], p["lru_conv_b"], p["lru_wa_bd"],
      p["lru_b_a"], p["lru_wx_bd"], p["lru_b_x"], p["lru_lambda"], p["lru_norm"], p["gdn_conv_w"], p["gdn_par"])


def _rec_kernel(u_ref, w_ref, qd_ref, kd_ref, at_ref, z_ref, gl_ref, s0_ref, nrm_ref, o_ref, s1_ref, s_scr,
                *, sb, c, shared_init):
    t = pl.program_id(1)

    @pl.when(t == 0)
    def _():
        for s in range(sb):
            s_scr[s] = s0_ref[0 if shared_init else s]

    chains = [(s, h, slice(h * GDN_DK, (h + 1) * GDN_DK)) for s in range(sb) for h in range(GDN_HEADS)]
    m1s = [jnp.dot(jnp.concatenate([w_ref[s, :, sl], qd_ref[s, :, sl]], axis=0), s_scr[s, h].astype(BF16),
                   preferred_element_type=F32) for s, h, sl in chains]
    vns = [(u_ref[s, :, sl] - m1[:c]).astype(BF16) for (s, h, sl), m1 in zip(chains, m1s)]
    for (s, h, sl), v_new in zip(chains, vns):
        s_scr[s, h] = s_scr[s, h] * gl_ref[s, 0, :, sl] + lax.dot_general(
            kd_ref[s, :, sl], v_new, (((0,), (0,)), ((), ())), preferred_element_type=F32)
    for (s, h, sl), m1, v_new in zip(chains, m1s, vns):
        o = m1[c:] + jnp.dot(at_ref[s, :, h * GDN_DK:h * GDN_DK + c], v_new, preferred_element_type=F32)
        z = z_ref[s, :, sl]
        o_ref[s, :, sl] = (_rms(o, nrm_ref[...]) * (z * _sigmoid(z))).astype(o_ref.dtype)

    @pl.when(t == pl.num_programs(1) - 1)
    def _():
        s1_ref[...] = s_scr[...]


def _rec(u, w, qd, kd, at, z, gl, s0, shared_init, n_streams, t_len, sb, c, p):
    hd = GDN_HEADS * GDN_DK
    nt = t_len // c
    v3 = lambda a: a.reshape(n_streams, t_len, hd)
    blk3 = lambda: pl.BlockSpec((sb, c, hd), lambda g, t: (g, t, 0))
    s_spec = (pl.BlockSpec((1, GDN_HEADS, GDN_DK, GDN_DV), lambda g, t: (0, 0, 0, 0)) if shared_init else
              pl.BlockSpec((sb, GDN_HEADS, GDN_DK, GDN_DV), lambda g, t: (g, 0, 0, 0)))
    out, s1 = pl.pallas_call(
        functools.partial(_rec_kernel, sb=sb, c=c, shared_init=shared_init),
        grid=(n_streams // sb, nt),
        in_specs=[blk3(), blk3(), blk3(), blk3(), blk3(), blk3(),
                  pl.BlockSpec((sb, 1, 1, hd), lambda g, t: (g, t, 0, 0)),
                  s_spec,
                  pl.BlockSpec((1, GDN_DV), lambda g, t: (0, 0))],
        out_specs=[blk3(), pl.BlockSpec((sb, GDN_HEADS, GDN_DK, GDN_DV), lambda g, t: (g, 0, 0, 0))],
        out_shape=[jax.ShapeDtypeStruct((n_streams, t_len, hd), BF16),
                   jax.ShapeDtypeStruct((n_streams, GDN_HEADS, GDN_DK, GDN_DV), F32)],
        scratch_shapes=[pltpu.VMEM((sb, GDN_HEADS, GDN_DK, GDN_DV), F32)],
        compiler_params=_cparams(("arbitrary", "arbitrary")),
        name="gdn_state",
    )(v3(u), v3(w), v3(qd), v3(kd), v3(at), v3(z), gl.reshape(n_streams, nt, 1, hd), s0, p["gdn_norm"])
    return out.reshape(n_streams * t_len, hd), s1


def _post_kernel(lru_p, gdn_p, x_p, lru_s, gdn_s, x_s, wo1_ref, wo2_ref, nrm_ref, rw_ref, rb_ref,
                 x1_ref, xn2_ref, route_ref, pw_ref, cnt_ref, *, tt, n_p):
    i = pl.program_id(0)
    is_p = i < n_p
    lru = jnp.where(is_p, lru_p[...], lru_s[...])
    gdn = jnp.where(is_p, gdn_p[...], gdn_s[...])
    x = jnp.where(is_p, x_p[...], x_s[...])
    m = (jnp.dot(lru, wo1_ref[...], preferred_element_type=F32)
         + jnp.dot(gdn, wo2_ref[...], preferred_element_type=F32))
    x1 = x + m
    x1_ref[...] = x1
    xn2 = _rms(x1, nrm_ref[...])
    _store_token_tiles(xn2_ref, 0, xn2)
    lane = lax.broadcasted_iota(jnp.int32, (tt, LANES), 1)
    logits = jnp.dot(xn2.astype(BF16), rw_ref[...], preferred_element_type=F32) + rb_ref[...]
    logits = jnp.where(lane < N_EXPERTS, logits, -jnp.inf)
    vals, idxs = [], []
    for _ in range(TOP_K):
        mx = jnp.max(logits, axis=-1, keepdims=True)
        ix = jnp.min(jnp.where(logits == mx, lane, LANES), axis=-1, keepdims=True)
        logits = jnp.where(lane == ix, -jnp.inf, logits)
        vals.append(mx)
        idxs.append(ix)
    es = [jnp.exp(v - vals[0]) for v in vals]
    den = es[0] + es[1] + es[2] + es[3]
    onehot = jnp.zeros((tt, LANES), F32)
    for ix in idxs:
        onehot = onehot + (lane == ix).astype(F32)
    ri = lax.broadcasted_iota(jnp.int32, (tt, tt), 0)
    ci = lax.broadcasted_iota(jnp.int32, (tt, tt), 1)
    cnt = jnp.sum(onehot, axis=0, keepdims=True)
    li = lax.broadcasted_iota(jnp.int32, (LANES, LANES), 0)
    lj = lax.broadcasted_iota(jnp.int32, (LANES, LANES), 1)
    off = jnp.dot(jnp.broadcast_to(cnt, (SUBLANES, LANES)).astype(BF16), (li < lj).astype(BF16),
                  preferred_element_type=F32)[0:1]
    before = jnp.dot((ri > ci).astype(BF16), onehot.astype(BF16), preferred_element_type=F32) + off
    route = jnp.zeros((tt, LANES), jnp.int32)
    pw = jnp.zeros((tt, LANES), F32)
    for kk in range(TOP_K):
        rank = jnp.sum(jnp.where(lane == idxs[kk], before, 0.0), axis=-1, keepdims=True).astype(jnp.int32)
        route = jnp.where(lane == kk, idxs[kk], route)
        route = jnp.where(lane == TOP_K + kk, rank, route)
        pw = jnp.where(lane == kk, es[kk] / den, pw)
    route_ref[...] = route
    pw_ref[...] = pw
    cnt_ref[0] = cnt.astype(jnp.int32)


def _post(lru_p, gdn_p, x_p, lru_s, gdn_s, x_s, tt, p):
    n_p = x_p.shape[0] // tt
    n_s = x_s.shape[0] // tt
    n = x_p.shape[0] + x_s.shape[0]
    prow = lambda w: pl.BlockSpec((tt, w), lambda i: (jnp.minimum(i, n_p - 1), 0))
    srow = lambda w: pl.BlockSpec((tt, w), lambda i: (jnp.maximum(i - n_p, 0), 0))
    row = lambda w: pl.BlockSpec((tt, w), lambda i: (i, 0))
    full = lambda a, b: pl.BlockSpec((a, b), lambda i: (0, 0))
    return pl.pallas_call(
        functools.partial(_post_kernel, tt=tt, n_p=n_p),
        grid=(n_p + n_s,),
        in_specs=[prow(D_LRU), prow(D_LRU), prow(D_MODEL), srow(D_LRU), srow(D_LRU), srow(D_MODEL),
                  full(D_LRU, D_MODEL), full(D_LRU, D_MODEL),
                  full(1, D_MODEL), full(D_MODEL, LANES), full(1, LANES)],
        out_specs=[row(D_MODEL), pl.BlockSpec((tt * TOKEN_ROWS, LANES), lambda i: (i, 0)), row(LANES), row(LANES),
                   pl.BlockSpec((1, 1, LANES), lambda i: (i, 0, 0))],
        out_shape=[jax.ShapeDtypeStruct((n, D_MODEL), F32), jax.ShapeDtypeStruct((n * TOKEN_ROWS, LANES), F32),
                   jax.ShapeDtypeStruct((n, LANES), jnp.int32), jax.ShapeDtypeStruct((n, LANES), F32),
                   jax.ShapeDtypeStruct((n_p + n_s, 1, LANES), jnp.int32)],
        compiler_params=_cparams(("arbitrary",)),
        name="outproj_router",
    )(lru_p, gdn_p, x_p, lru_s, gdn_s, x_s, p["w_out1"], p["w_out2"], p["norm_ffn"], p["router_w"], p["router_b"])


def _run_copies(cnt_ref, off_ref, base_ref, vbuf, hbm, sem, to_hbm):
    for e in range(N_EXPERTS):
        cnt = cnt_ref[0, 0, e]
        off = off_ref[0, 0, e]
        base = base_ref[0, 0, e]
        for bit in RUN_BITS:
            done = (cnt & (-2 * bit)) * TOKEN_ROWS

            @pl.when((cnt & bit) != 0)
            def _():
                v = vbuf.at[pl.ds(pl.multiple_of(off + done, TOKEN_ROWS), bit * TOKEN_ROWS), :]
                h = hbm.at[pl.ds(pl.multiple_of(base + done, TOKEN_ROWS), bit * TOKEN_ROWS), :]
                if to_hbm:
                    pltpu.make_async_copy(v, h, sem).start()
                else:
                    pltpu.make_async_copy(h, v, sem).start()


def _wait_run_copies(vbuf, hbm, sem, to_hbm):
    h = hbm.at[pl.ds(0, vbuf.shape[0]), :]
    if to_hbm:
        pltpu.make_async_copy(vbuf, h, sem).wait()
    else:
        pltpu.make_async_copy(h, vbuf, sem).wait()


def _dispatch_kernel(zrow_ref, zcnt_ref, nu_ref, pos_ref, cnt_ref, off_ref, base_ref, x_ref, xs_hbm,
                     dbuf0, dbuf1, zbuf, sems, zsem, *, tt, blk):
    i = pl.program_id(0)
    n = pl.num_programs(0)
    dbufs = (dbuf0, dbuf1)

    def pad_copies(wait):
        for e in range(N_EXPERTS):
            cnt = zcnt_ref[e]
            row = zrow_ref[e]
            for bit in RUN_BITS:
                done = (cnt & (-2 * bit)) * TOKEN_ROWS

                @pl.when((cnt & bit) != 0)
                def _():
                    cp = pltpu.make_async_copy(
                        zbuf.at[pl.ds(0, bit * TOKEN_ROWS), :],
                        xs_hbm.at[pl.ds(pl.multiple_of(row + done, TOKEN_ROWS), bit * TOKEN_ROWS), :], zsem.at[0])
                    if wait:
                        cp.wait()
                    else:
                        cp.start()
        n_blocks = xs_hbm.shape[0] // (blk * TOKEN_ROWS)
        for j in range(N_EXPERTS):
            @pl.when(nu_ref[0] + j < n_blocks)
            def _():
                row = pl.multiple_of((nu_ref[0] + j) * (blk * TOKEN_ROWS), TOKEN_ROWS)
                cp = pltpu.make_async_copy(zbuf, xs_hbm.at[pl.ds(row, blk * TOKEN_ROWS), :], zsem.at[0])
                if wait:
                    cp.wait()
                else:
                    cp.start()

    @pl.when(i == 0)
    def _():
        zbuf[...] = jnp.zeros_like(zbuf)
        pad_copies(False)

    def step(cur):
        dbuf = dbufs[cur]

        @pl.when(i >= 2)
        def _():
            _wait_run_copies(dbuf, xs_hbm, sems.at[cur], True)

        for t in range(tt):
            v = x_ref[t * TOKEN_ROWS:(t + 1) * TOKEN_ROWS, :]
            for kk in range(TOP_K):
                dbuf[pl.ds(pl.multiple_of(pos_ref[0, 0, t * TOP_K + kk], TOKEN_ROWS), TOKEN_ROWS), :] = v
        _run_copies(cnt_ref, off_ref, base_ref, dbuf, xs_hbm, sems.at[cur], True)

        @pl.when(i == n - 1)
        def _():
            _wait_run_copies(dbuf, xs_hbm, sems.at[cur], True)

            @pl.when(i >= 1)
            def _():
                _wait_run_copies(dbufs[1 - cur], xs_hbm, sems.at[1 - cur], True)

            pad_copies(True)

    @pl.when(i % 2 == 0)
    def _():
        step(0)

    @pl.when(i % 2 == 1)
    def _():
        step(1)


def _dispatch(zrow, zcnt, n_used, pos, cnt_t, off_t, base_t, xn2, n_slots, tt, blk):
    n_tiles = cnt_t.shape[0]
    smem = lambda w: pl.BlockSpec((1, 1, w), lambda i, zr, zc, nu: (i, 0, 0), memory_space=pltpu.SMEM)
    rows = TOP_K * tt * TOKEN_ROWS
    grid_spec = pltpu.PrefetchScalarGridSpec(
        num_scalar_prefetch=3,
        grid=(n_tiles,),
        in_specs=[smem(TOP_K * tt), smem(LANES), smem(LANES), smem(LANES),
                  pl.BlockSpec((tt * TOKEN_ROWS, LANES), lambda i, zr, zc, nu: (i, 0))],
        out_specs=pl.BlockSpec(memory_space=pl.ANY),
        scratch_shapes=[pltpu.VMEM((rows, LANES), F32), pltpu.VMEM((rows, LANES), F32),
                        pltpu.VMEM((blk * TOKEN_ROWS, LANES), F32),
                        pltpu.SemaphoreType.DMA((2,)), pltpu.SemaphoreType.DMA((1,))],
    )
    return pl.pallas_call(
        functools.partial(_dispatch_kernel, tt=tt, blk=blk),
        grid_spec=grid_spec,
        out_shape=jax.ShapeDtypeStruct((n_slots * TOKEN_ROWS, LANES), F32),
        compiler_params=_cparams(("arbitrary",)),
        name="moe_dispatch",
    )(zrow, zcnt, n_used, pos, cnt_t, off_t, base_t, xn2)


def _expert_kernel(be_ref, first_ref, nu_ref, x_ref, wg_ref, bg_ref, wu_ref, bu_ref, wd_ref, bd_ref, y_ref, wbf,
                   *, blk):
    b = pl.program_id(0)

    @pl.when((first_ref[b] == 1) & (b < nu_ref[0]))
    def _():
        wbf[0] = wg_ref[0].astype(BF16)
        wbf[1] = wu_ref[0].astype(BF16)
        wbf[2] = wd_ref[0].astype(BF16)

    @pl.when(b < nu_ref[0])
    def _():
        xb = _load_token_tiles(x_ref, 0, blk).astype(BF16)
        gt = jnp.minimum(jnp.dot(xb, wbf[0], preferred_element_type=F32) + bg_ref[0], SWIGLU_LIMIT)
        up = jnp.clip(jnp.dot(xb, wbf[1], preferred_element_type=F32) + bu_ref[0], -SWIGLU_LIMIT, SWIGLU_LIMIT)
        hid = (up + 1.0) * gt * _sigmoid(SWIGLU_ALPHA * gt)
        _store_token_tiles(y_ref, 0, jnp.dot(hid.astype(BF16), wbf[2], preferred_element_type=F32) + bd_ref[0])

    @pl.when(b >= nu_ref[0])
    def _():
        y_ref[...] = jnp.zeros_like(y_ref)


def _experts(block_expert, first, n_used, xs, p, blk):
    n_blocks = block_expert.shape[0]
    used = lambda b, nu: jnp.minimum(b, nu[0] - 1)
    wspec = lambda: pl.BlockSpec((1, D_MODEL, D_MODEL), lambda b, be, fi, nu: (be[used(b, nu)], 0, 0))
    bspec = lambda: pl.BlockSpec((1, 1, D_MODEL), lambda b, be, fi, nu: (be[used(b, nu)], 0, 0))
    rspec = lambda: pl.BlockSpec((blk * TOKEN_ROWS, LANES), lambda b, be, fi, nu: (used(b, nu), 0))
    grid_spec = pltpu.PrefetchScalarGridSpec(
        num_scalar_prefetch=3,
        grid=(n_blocks,),
        in_specs=[rspec(), wspec(), bspec(), wspec(), bspec(), wspec(), bspec()],
        out_specs=pl.BlockSpec((blk * TOKEN_ROWS, LANES), lambda b, be, fi, nu: (b, 0)),
        scratch_shapes=[pltpu.VMEM((3, D_MODEL, D_MODEL), BF16)],
    )
    return pl.pallas_call(
        functools.partial(_expert_kernel, blk=blk),
        grid_spec=grid_spec,
        out_shape=jax.ShapeDtypeStruct(xs.shape, F32),
        compiler_params=_cparams(("arbitrary",)),
        name="experts",
    )(block_expert, first, n_used, xs, p["exp_w_gate"], p["exp_b_gate"], p["exp_w_up"], p["exp_b_up"],
      p["exp_w_down"], p["exp_b_down"])


def _combine_kernel(pos_ref, pw_ref, cnt_ref, off_ref, base_ref, cntn_ref, offn_ref, basen_ref,
                    ys_hbm, x1_ref, nrm_ref, op_ref, os_ref, ybuf0, ybuf1, mbuf, sems, *, tt, n_p):
    i = pl.program_id(0)
    n = pl.num_programs(0)
    ybufs = (ybuf0, ybuf1)

    @pl.when(i == 0)
    def _():
        _run_copies(cnt_ref, off_ref, base_ref, ybuf0, ys_hbm, sems.at[0], False)

    def step(cur):
        @pl.when(i + 1 < n)
        def _():
            _run_copies(cntn_ref, offn_ref, basen_ref, ybufs[1 - cur], ys_hbm, sems.at[1 - cur], False)

        _wait_run_copies(ybufs[cur], ys_hbm, sems.at[cur], False)
        ybuf = ybufs[cur]
        for t in range(tt):
            acc = None
            for kk in range(TOP_K):
                j = t * TOP_K + kk
                row = ybuf[pl.ds(pl.multiple_of(pos_ref[0, 0, j], TOKEN_ROWS), TOKEN_ROWS), :]
                term = pw_ref[0, 0, j] * row
                acc = term if acc is None else acc + term
            mbuf[t * TOKEN_ROWS:(t + 1) * TOKEN_ROWS, :] = acc

    @pl.when(i % 2 == 0)
    def _():
        step(0)

    @pl.when(i % 2 == 1)
    def _():
        step(1)

    res = _rms(x1_ref[...] + _load_token_tiles(mbuf, 0, tt), nrm_ref[...])

    @pl.when(i < n_p)
    def _():
        op_ref[...] = res

    @pl.when(i >= n_p)
    def _():
        os_ref[...] = res


def _combine(pos, pw, cnt_t, off_t, base_t, ys, x1, final_norm, n_prompt, tt):
    n = x1.shape[0]
    n_p = n_prompt // tt
    n_s = (n - n_prompt) // tt
    nt = n_p + n_s
    smem = lambda w: pl.BlockSpec((1, 1, w), lambda i: (i, 0, 0), memory_space=pltpu.SMEM)
    smem_next = lambda w: pl.BlockSpec((1, 1, w), lambda i: (jnp.minimum(i + 1, nt - 1), 0, 0),
                                       memory_space=pltpu.SMEM)
    rows = TOP_K * tt * TOKEN_ROWS
    return pl.pallas_call(
        functools.partial(_combine_kernel, tt=tt, n_p=n_p),
        grid=(nt,),
        in_specs=[smem(TOP_K * tt), smem(TOP_K * tt), smem(LANES), smem(LANES), smem(LANES),
                  smem_next(LANES), smem_next(LANES), smem_next(LANES),
                  pl.BlockSpec(memory_space=pl.ANY),
                  pl.BlockSpec((tt, D_MODEL), lambda i: (i, 0)),
                  pl.BlockSpec((1, D_MODEL), lambda i: (0, 0))],
        out_specs=[pl.BlockSpec((tt, D_MODEL), lambda i: (jnp.minimum(i, n_p - 1), 0)),
                   pl.BlockSpec((tt, D_MODEL), lambda i: (jnp.maximum(i - n_p, 0), 0))],
        out_shape=[jax.ShapeDtypeStruct((n_prompt, D_MODEL), F32),
                   jax.ShapeDtypeStruct((n - n_prompt, D_MODEL), F32)],
        scratch_shapes=[pltpu.VMEM((rows, LANES), F32), pltpu.VMEM((rows, LANES), F32),
                        pltpu.VMEM((tt * TOKEN_ROWS, LANES), F32), pltpu.SemaphoreType.DMA((2,))],
        compiler_params=_cparams(("arbitrary",)),
        name="moe_combine",
    )(pos, pw, cnt_t, off_t, base_t, cnt_t, off_t, base_t, ys, x1, final_norm)


def _pad_rows(a, rows):
    return jnp.pad(a, ((0, 0), (rows - a.shape[1], 0), (0, 0)))


def _tile(n, pref):
    t = pref
    while n % t:
        t //= 2
    return t


def _mixer_group(x2d, n_streams, t_len, lru_conv0, lru_h0, gdn_conv0, gdn_s0, shared_init, reset_first, chunk, p):
    tt = _tile(t_len, TOKEN_TILE)
    lru_out, u, w, qd, kd, at, z, gl, lru_conv1, lru_h1, gdn_conv1 = _pre(
        x2d, lru_conv0, lru_h0, gdn_conv0, shared_init, n_streams, t_len, tt, chunk, reset_first, p)
    sb = _tile(n_streams, REC_STREAMS)
    gdn_out, gdn_s1 = _rec(u, w, qd, kd, at, z, gl, gdn_s0, shared_init, n_streams, t_len, sb, chunk, p)
    return lru_out, gdn_out, (lru_conv1, lru_h1, gdn_conv1, gdn_s1)


def _routing(cnt_tiles, n_tok, blk):
    cnt_t = cnt_tiles[:, 0, :]
    counts = jnp.sum(cnt_t, axis=0)
    padded = (counts + blk - 1) // blk * blk
    pends = jnp.cumsum(padded)
    pstarts = pends - padded
    off_t = jnp.cumsum(cnt_t, axis=1) - cnt_t
    base_t = pstarts[None, :] + jnp.cumsum(cnt_t, axis=0) - cnt_t
    n_blocks = (n_tok * TOP_K + N_EXPERTS * (blk - 1) + blk - 1) // blk
    starts = jnp.arange(n_blocks, dtype=jnp.int32) * blk
    block_expert = jnp.minimum(jnp.sum((pends[None, :N_EXPERTS] <= starts[:, None]).astype(jnp.int32), axis=1),
                               N_EXPERTS - 1)
    first = jnp.concatenate([jnp.ones((1,), jnp.int32),
                             (block_expert[1:] != block_expert[:-1]).astype(jnp.int32)])
    n_used = (pends[N_EXPERTS - 1] // blk).astype(jnp.int32)[None]
    r3 = lambda a: (a * TOKEN_ROWS).astype(jnp.int32)[:, None, :]
    zrow = ((pstarts + counts) * TOKEN_ROWS).astype(jnp.int32)[:N_EXPERTS]
    zcnt = (padded - counts).astype(jnp.int32)[:N_EXPERTS]
    return (cnt_t.astype(jnp.int32)[:, None, :], r3(off_t), r3(base_t), block_expert, first, n_used, zrow, zcnt,
            n_blocks * blk)


def kernel(x_prompt, x_sample, state_lru_conv, state_lru_h, state_gdn_conv, state_gdn_S, meta_tokens, norm_mix, w_in, lru_conv_w, lru_conv_b, lru_w_a, lru_b_a, lru_w_x, lru_b_x, lru_lambda, lru_norm, gdn_conv_w, gdn_A_log, gdn_dt_bias, gdn_norm, w_out, norm_ffn, router_w, router_b, exp_w_gate, exp_b_gate, exp_w_up, exp_b_up, exp_w_down, exp_b_down, final_norm):
    bp, tp, _ = x_prompt.shape
    bs, ts, _ = x_sample.shape
    n_meta = meta_tokens.shape[0]
    n_prompt, n_sample = bp * tp, bs * ts
    tt = TOKEN_TILE
    assert n_prompt % tt == 0 and n_sample % tt == 0 and EXPERT_ROWS <= 2 * TOKEN_TILE

    def blockdiag(w):
        eye = jnp.eye(LRU_BLOCKS, dtype=w.dtype)
        return jnp.einsum("ncd,nm->ncmd", w, eye).reshape(D_LRU, D_LRU)

    lane_pad = lambda v: jnp.pad(v, (0, LANES - v.shape[0]))[None]
    p = dict(
        norm_mix=norm_mix[0][None],
        w_in=jnp.pad(w_in[0], ((0, 0), (0, D_IN_PAD - D_IN))).astype(BF16),
        lru_conv_w=lru_conv_w[0], lru_conv_b=lru_conv_b[0][None],
        lru_wa_bd=blockdiag(lru_w_a[0]).astype(BF16), lru_b_a=lru_b_a[0][None],
        lru_wx_bd=blockdiag(lru_w_x[0]).astype(BF16), lru_b_x=lru_b_x[0][None],
        lru_lambda=lru_lambda[0][None], lru_norm=lru_norm[0][None],
        gdn_conv_w=gdn_conv_w[0],
        gdn_par=jnp.concatenate([lane_pad(gdn_A_log[0]), lane_pad(gdn_dt_bias[0])], axis=0),
        gdn_norm=gdn_norm[0][None],
        w_out1=w_out[0][:D_LRU].astype(BF16), w_out2=w_out[0][D_LRU:].astype(BF16),
        norm_ffn=norm_ffn[0][None],
        router_w=jnp.pad(router_w[0], ((0, 0), (0, LANES - N_EXPERTS))).astype(BF16),
        router_b=lane_pad(router_b[0]),
        exp_w_gate=exp_w_gate[0], exp_b_gate=exp_b_gate[0][:, None, :],
        exp_w_up=exp_w_up[0], exp_b_up=exp_b_up[0][:, None, :],
        exp_w_down=exp_w_down[0], exp_b_down=exp_b_down[0][:, None, :],
    )

    zc = lambda c: jnp.zeros((1, SUBLANES, c), F32)
    _, _, m_state = _mixer_group(meta_tokens, 1, n_meta, zc(D_LRU), jnp.zeros((1, 1, D_LRU), F32), zc(GDN_QKV),
                                 jnp.zeros((1, GDN_HEADS, GDN_DK, GDN_DV), F32), False, True, n_meta, p)
    xp2 = x_prompt.reshape(n_prompt, D_MODEL)
    xs2 = x_sample.reshape(n_sample, D_MODEL)
    p_lru, p_gdn, p_state = _mixer_group(xp2, bp, tp, m_state[0], m_state[1], m_state[2], m_state[3], True, False,
                                         GDN_CHUNK, p)
    s_lru, s_gdn, s_state = _mixer_group(xs2, bs, ts, _pad_rows(state_lru_conv[0], SUBLANES),
                                         state_lru_h[0][:, None, :], _pad_rows(state_gdn_conv[0], SUBLANES),
                                         state_gdn_S[0], False, False, GDN_CHUNK, p)

    x1, xn2, route, pw, cnt_tiles = _post(p_lru, p_gdn, xp2, s_lru, s_gdn, xs2, tt, p)
    n_tok = n_prompt + n_sample
    cnt_t, off_t, base_t, block_expert, first, n_used, zrow, zcnt, n_slots = _routing(cnt_tiles, n_tok, EXPERT_ROWS)
    n_tiles = n_tok // tt
    pos = (route[:, TOP_K:2 * TOP_K] * TOKEN_ROWS).reshape(n_tiles, 1, TOP_K * tt)
    pw_s = pw[:, :TOP_K].reshape(n_tiles, 1, TOP_K * tt)
    xs = _dispatch(zrow, zcnt, n_used, pos, cnt_t, off_t, base_t, xn2, n_slots, tt, EXPERT_ROWS)
    ys = _experts(block_expert, first, n_used, xs, p, EXPERT_ROWS)
    y_p, y_s = _combine(pos, pw_s, cnt_t, off_t, base_t, ys, x1, final_norm[None], n_prompt, tt)

    def states(st, b):
        return (st[0][:, SUBLANES - 3:, :][None], st[1].reshape(1, b, D_LRU), st[2][:, SUBLANES - 3:, :][None],
                st[3][None])

    return (y_p.reshape(bp, tp, D_MODEL), y_s.reshape(bs, ts, D_MODEL)) + states(p_state, bp) + states(s_state, bs)
```

```python
import functools

import jax
import jax.numpy as jnp
from jax import lax
from jax.experimental import pallas as pl
from jax.experimental.pallas import tpu as pltpu

F32 = jnp.float32
BF16 = jnp.bfloat16

D_MODEL = 1024
D_LRU = 512
LRU_BLOCKS = 8
LRU_C = 8.0
CONV_W = 4
GDN_HEADS = 4
GDN_DK = 128
GDN_DV = 128
GDN_QKV = GDN_HEADS * (2 * GDN_DK + GDN_DV)
D_IN = 2 * D_LRU + GDN_QKV + GDN_HEADS * GDN_DV + 2 * GDN_HEADS
N_EXPERTS = 32
TOP_K = 4
SWIGLU_LIMIT = 7.0
SWIGLU_ALPHA = 1.702
EPS = 1e-6

LANES = 128
SUBLANES = 8
D_IN_PAD = 3200
COL_AB = (2 * D_LRU + GDN_QKV + GDN_HEADS * GDN_DV) // LANES
VMEM_LIMIT = 56 * 1024 * 1024

TOKEN_TILE = 256
EXPERT_ROWS = 512
GDN_CHUNK = 64
REC_STREAMS = 8
PRE_CHAINS = 8
PROJ_PIECE = 256
PROJ_EARLY_POINTS = 8
TOKEN_ROWS = D_MODEL // LANES
RUN_BITS = tuple(1 << i for i in range(TOKEN_TILE.bit_length() - 1, -1, -1))


def _cparams(sem):
    return pltpu.CompilerParams(dimension_semantics=sem, vmem_limit_bytes=VMEM_LIMIT)


def _rms(x, gain):
    return x * lax.rsqrt(jnp.mean(x * x, axis=-1, keepdims=True) + EPS) * gain


def _softplus(x):
    return jnp.maximum(x, 0.0) + jnp.log1p(jnp.exp(-jnp.abs(x)))


def _sigmoid(x):
    return 1.0 / (1.0 + jnp.exp(-x))


def _load_token_tiles(ref, tok0, n_tok):
    return jnp.concatenate(
        [ref[pl.ds(tok0 * TOKEN_ROWS + j, n_tok, stride=TOKEN_ROWS), :] for j in range(TOKEN_ROWS)], axis=1)


def _store_token_tiles(ref, tok0, val):
    for j in range(TOKEN_ROWS):
        ref[pl.ds(tok0 * TOKEN_ROWS + j, val.shape[0], stride=TOKEN_ROWS), :] = val[:, j * LANES:(j + 1) * LANES]


def _mm(a, b):
    return jnp.dot(a.astype(BF16), b.astype(BF16), preferred_element_type=F32)


def _mm_nt(a, b):
    return lax.dot_general(a.astype(BF16), b.astype(BF16), (((1,), (1,)), ((), ())), preferred_element_type=F32)


def _conv_step(buf, x, w_ref, tt):
    buf[SUBLANES:SUBLANES + tt, :] = x
    y = x * w_ref[CONV_W - 1:CONV_W, :]
    for j in range(CONV_W - 2, -1, -1):
        y = y + buf[SUBLANES - 3 + j:SUBLANES - 3 + j + tt, :] * w_ref[j:j + 1, :]
    tail = buf[tt:tt + SUBLANES, :]
    buf[0:SUBLANES, :] = tail
    return y


def _pre_kernel(x_ref, xnext_ref, nrm_ref, win_ref, lconv0_ref, h0_ref, gconv0_ref,
                lcw_ref, lcb_ref, wa_ref, ba_ref, wx_ref, bx_ref, lam_ref, lnrm_ref, gcw_ref, par_ref,
                lru_ref, u_ref, w_ref, qd_ref, kd_ref, at_ref, z_ref, gl_ref, lconv1_ref, h1_ref, gconv1_ref,
                lbuf, a_buf, b_buf, hcar, gbuf, pbuf, *, tt, c, reset_first):
    t = pl.program_id(1)
    hd = GDN_HEADS * GDN_DK

    @pl.when(t == 0)
    def _():
        lbuf[0:SUBLANES, :] = lconv0_ref[0]
        hcar[...] = h0_ref[0]
        for j in range(3):
            gbuf[j, 0:SUBLANES, :] = gconv0_ref[0, :, j * hd:(j + 1) * hd]

    @pl.when((pl.program_id(0) == 0) & (t == 0))
    def _():
        pbuf[...] = jnp.dot(_rms(x_ref[...], nrm_ref[...]).astype(BF16), win_ref[...], preferred_element_type=F32)

    th = tt
    projs = [pbuf]
    xnext = _rms(xnext_ref[...], nrm_ref[...]).astype(BF16)
    pieces = [(lo, min(lo + PROJ_PIECE, D_IN_PAD)) for lo in range(0, D_IN_PAD, PROJ_PIECE)]
    z_col = 2 * D_LRU + GDN_QKV
    assert z_col % PROJ_PIECE == 0
    pieces = [pc for pc in pieces if pc[0] >= z_col] + [pc for pc in pieces if pc[0] < z_col]
    n_levels = max(c.bit_length() - 1, 1)
    n_solve_points = -(-(tt // c) * GDN_HEADS // PRE_CHAINS) * n_levels
    n_late = max(len(pieces) - PROJ_EARLY_POINTS, 0)
    plan = [1] * PROJ_EARLY_POINTS + [(i + 1) * n_late // n_solve_points - i * n_late // n_solve_points
                                      for i in range(n_solve_points)]
    points_done = [0]
    pieces_done = [0]

    def next_proj_pieces():
        k = plan[points_done[0]]
        points_done[0] += 1
        for lo, hi in pieces[pieces_done[0]:pieces_done[0] + k]:
            pbuf[:, lo:hi] = jnp.dot(xnext, win_ref[:, lo:hi], preferred_element_type=F32)
        pieces_done[0] += k

    rowc = lax.broadcasted_iota(jnp.int32, (c, LANES), 0)
    ri = lax.broadcasted_iota(jnp.int32, (c, c), 0)
    ci = lax.broadcasted_iota(jnp.int32, (c, c), 1)
    causal = ri >= ci
    strict = ri > ci
    eye = ri == ci
    lane_pad = jnp.zeros((c, LANES - c), F32)
    row8 = lax.broadcasted_iota(jnp.int32, (SUBLANES, D_LRU), 0)

    def group(gi, h):
        g0 = pl.multiple_of(gi * SUBLANES, SUBLANES)
        a8 = a_buf[pl.ds(g0, SUBLANES), :]
        b8 = b_buf[pl.ds(g0, SUBLANES), :]
        for d in (1, 2, 4):
            keep = row8 >= d
            b8 = jnp.where(keep, a8 * pltpu.roll(b8, d, 0) + b8, b8)
            a8 = jnp.where(keep, a8 * pltpu.roll(a8, d, 0), a8)
        h8 = a8 * h + b8
        b_buf[pl.ds(g0, SUBLANES), :] = h8
        return h8[SUBLANES - 1:SUBLANES, :]

    h_last = hcar[...]
    for part, proj in enumerate(projs):
        r0 = part * th

        z_ref[r0:r0 + th, :] = proj[:, 2 * D_LRU + GDN_QKV:2 * D_LRU + GDN_QKV + hd]
        ab = proj[:, COL_AB * LANES:(COL_AB + 1) * LANES]
        g_all = -jnp.exp(par_ref[0:1, :]) * _softplus(ab + par_ref[1:2, :])
        beta_all = _sigmoid(ab)
        next_proj_pieces()

        xc = _conv_step(lbuf, proj[:, 0:D_LRU], lcw_ref, th) + lcb_ref[...]
        next_proj_pieces()
        xb = xc.astype(BF16)
        r = _sigmoid(jnp.dot(xb, wa_ref[...], preferred_element_type=F32) + ba_ref[...])
        i = _sigmoid(jnp.dot(xb, wx_ref[...], preferred_element_type=F32) + bx_ref[...])
        next_proj_pieces()
        log_a = (-LRU_C) * r * _softplus(-lam_ref[...])
        a = jnp.exp(log_a)
        mult = jnp.sqrt(-jnp.tanh(log_a) * (a * a + 1.0))
        if reset_first and part == 0:
            row = lax.broadcasted_iota(jnp.int32, (th, D_LRU), 0)
            mult = jnp.where((row == 0) & (t == 0), 1.0, mult)
        a_buf[0:th, :] = a
        b_buf[0:th, :] = mult * i * xc
        next_proj_pieces()
        h_last = lax.fori_loop(0, th // SUBLANES, group, h_last, unroll=True)
        lru_ref[r0:r0 + th, :] = _rms(b_buf[0:th, :] * jax.nn.gelu(proj[:, D_LRU:2 * D_LRU]),
                                      lnrm_ref[...]).astype(lru_ref.dtype)
        next_proj_pieces()

        def conv_silu(j, proj=proj):
            y = _conv_step(gbuf.at[j], proj[:, 2 * D_LRU + j * hd:2 * D_LRU + (j + 1) * hd],
                           gcw_ref.at[:, j * hd:(j + 1) * hd], th)
            return y * _sigmoid(y)

        qa = conv_silu(0)
        next_proj_pieces()
        ka = conv_silu(1)
        next_proj_pieces()
        va = conv_silu(2)
        next_proj_pieces()

        pairs = [(j, h) for j in range(th // c) for h in range(GDN_HEADS)]
        for p0 in range(0, len(pairs), PRE_CHAINS):
            group_pairs = pairs[p0:p0 + PRE_CHAINS]
            gcs = {}
            for j in sorted({j for j, _ in group_pairs}):
                gc = g_all[j * c:(j + 1) * c]
                d = 1
                while d < c:
                    gc = gc + jnp.where(rowc >= d, pltpu.roll(gc, d, 0), 0.0)
                    d *= 2
                gcs[j] = gc
            ns, xs = [], []
            for j, h in group_pairs:
                rows = slice(j * c, (j + 1) * c)
                orow = slice(r0 + j * c, r0 + (j + 1) * c)
                sl = slice(h * GDN_DK, (h + 1) * GDN_DK)
                gc = gcs[j]
                q = qa[rows, sl]
                k = ka[rows, sl]
                q = q * lax.rsqrt(jnp.sum(q * q, axis=-1, keepdims=True) + EPS) * (GDN_DK ** -0.5)
                k = k * lax.rsqrt(jnp.sum(k * k, axis=-1, keepdims=True) + EPS)
                gcol = gc[:, h:h + 1]
                bcol = beta_all[rows, GDN_HEADS + h:GDN_HEADS + h + 1]
                grow = jnp.sum(jnp.where(eye, gcol, 0.0), axis=0, keepdims=True)
                decay = jnp.where(causal, jnp.exp(jnp.where(causal, gcol - grow, 0.0)), 0.0)
                egc = jnp.exp(gcol)
                g_last = gc[c - 1:c, h:h + 1]
                kb = k * bcol
                qd_ref[orow, sl] = (q * egc).astype(qd_ref.dtype)
                kd_ref[orow, sl] = (k * jnp.exp(g_last - gcol)).astype(kd_ref.dtype)
                gl_ref[r0 // c + j, :, sl] = jnp.broadcast_to(jnp.exp(g_last), (1, GDN_DV))
                at_ref[orow, sl] = jnp.concatenate([_mm_nt(q, k) * decay, lane_pad], axis=1).astype(at_ref.dtype)
                ns.append(jnp.where(strict, _mm_nt(kb, k) * decay, 0.0))
                xs.append(jnp.concatenate([va[rows, sl] * bcol, kb * egc], axis=1))
            xs = [x - _mm(n, x) for n, x in zip(ns, xs)]
            next_proj_pieces()
            m = 2
            while m < c:
                ns = [_mm(n, n) for n in ns]
                xs = [x + _mm(n, x) for n, x in zip(ns, xs)]
                next_proj_pieces()
                m *= 2
            for (j, h), x in zip(group_pairs, xs):
                orow = slice(r0 + j * c, r0 + (j + 1) * c)
                sl = slice(h * GDN_DK, (h + 1) * GDN_DK)
                u_ref[orow, sl] = x[:, :GDN_DV]
                w_ref[orow, sl] = x[:, GDN_DV:].astype(w_ref.dtype)
    hcar[...] = h_last
    assert points_done[0] == len(plan) and pieces_done[0] == len(pieces)

    @pl.when(t == pl.num_programs(1) - 1)
    def _():
        lconv1_ref[0] = lbuf[0:SUBLANES, :]
        h1_ref[0] = h_last
        for j in range(3):
            gconv1_ref[0, :, j * hd:(j + 1) * hd] = gbuf[j, 0:SUBLANES, :]


def _pre(x2d, lconv0, h0, gconv0, shared_init, n_streams, t_len, tt, c, reset_first, p):
    nt = t_len // tt
    hd = GDN_HEADS * GDN_DK
    n = n_streams * t_len
    st = (lambda s, t: (0, 0, 0)) if shared_init else (lambda s, t: (s, 0, 0))
    full = lambda a, b: pl.BlockSpec((a, b), lambda s, t: (0, 0))
    row = lambda w: pl.BlockSpec((tt, w), lambda s, t: (s * nt + t, 0))
    return pl.pallas_call(
        functools.partial(_pre_kernel, tt=tt, c=c, reset_first=reset_first),
        grid=(n_streams, nt),
        in_specs=[row(D_MODEL),
                  pl.BlockSpec((tt, D_MODEL), lambda s, t: (jnp.minimum(s * nt + t + 1, n_streams * nt - 1), 0)),
                  full(1, D_MODEL), full(D_MODEL, D_IN_PAD),
                  pl.BlockSpec((1, SUBLANES, D_LRU), st), pl.BlockSpec((1, 1, D_LRU), st),
                  pl.BlockSpec((1, SUBLANES, GDN_QKV), st),
                  full(CONV_W, D_LRU), full(1, D_LRU), full(D_LRU, D_LRU), full(1, D_LRU), full(D_LRU, D_LRU),
                  full(1, D_LRU), full(1, D_LRU), full(1, D_LRU), full(CONV_W, GDN_QKV), full(2, LANES)],
        out_specs=[row(D_LRU), row(hd), row(hd), row(hd), row(hd), row(hd), row(hd),
                   pl.BlockSpec((tt // c, 1, hd), lambda s, t: (s * nt + t, 0, 0)),
                   pl.BlockSpec((1, SUBLANES, D_LRU), lambda s, t: (s, 0, 0)),
                   pl.BlockSpec((1, 1, D_LRU), lambda s, t: (s, 0, 0)),
                   pl.BlockSpec((1, SUBLANES, GDN_QKV), lambda s, t: (s, 0, 0))],
        out_shape=[jax.ShapeDtypeStruct((n, D_LRU), BF16),
                   jax.ShapeDtypeStruct((n, hd), F32),
                   jax.ShapeDtypeStruct((n, hd), BF16),
                   jax.ShapeDtypeStruct((n, hd), BF16),
                   jax.ShapeDtypeStruct((n, hd), BF16),
                   jax.ShapeDtypeStruct((n, hd), BF16),
                   jax.ShapeDtypeStruct((n, hd), F32),
                   jax.ShapeDtypeStruct((n // c, 1, hd), F32),
                   jax.ShapeDtypeStruct((n_streams, SUBLANES, D_LRU), F32),
                   jax.ShapeDtypeStruct((n_streams, 1, D_LRU), F32),
                   jax.ShapeDtypeStruct((n_streams, SUBLANES, GDN_QKV), F32)],
        scratch_shapes=[pltpu.VMEM((tt + SUBLANES, D_LRU), F32),
                        pltpu.VMEM((tt, D_LRU), F32),
                        pltpu.VMEM((tt, D_LRU), F32),
                        pltpu.VMEM((1, D_LRU), F32),
                        pltpu.VMEM((3, tt + SUBLANES, hd), F32),
                        pltpu.VMEM((tt, D_IN_PAD), F32)],
        compiler_params=_cparams(("arbitrary", "arbitrary")),
        name="mixer_pre",
    )(x2d, x2d, p["norm_mix"], p["w_in"], lconv0, h0, gconv0, p["lru_conv_w"], p["lru_conv_b"], p["lru_wa_bd"],
      p["lru_b_a"], p["lru_wx_bd"], p["lru_b_x"], p["lru_lambda"], p["lru_norm"], p["gdn_conv_w"], p["gdn_par"])


def _rec_kernel(u_ref, w_ref, qd_ref, kd_ref, at_ref, z_ref, gl_ref, s0_ref, nrm_ref, o_ref, s1_ref, s_scr,
                *, sb, c, shared_init):
    t = pl.program_id(1)

    @pl.when(t == 0)
    def _():
        for s in range(sb):
            s_scr[s] = s0_ref[0 if shared_init else s]

    chains = [(s, h, slice(h * GDN_DK, (h + 1) * GDN_DK)) for s in range(sb) for h in range(GDN_HEADS)]
    m1s = [jnp.dot(jnp.concatenate([w_ref[s, :, sl], qd_ref[s, :, sl]], axis=0), s_scr[s, h].astype(BF16),
                   preferred_element_type=F32) for s, h, sl in chains]
    vns = [(u_ref[s, :, sl] - m1[:c]).astype(BF16) for (s, h, sl), m1 in zip(chains, m1s)]
    for (s, h, sl), v_new in zip(chains, vns):
        s_scr[s, h] = s_scr[s, h] * gl_ref[s, 0, :, sl] + lax.dot_general(
            kd_ref[s, :, sl], v_new, (((0,), (0,)), ((), ())), preferred_element_type=F32)
    for (s, h, sl), m1, v_new in zip(chains, m1s, vns):
        o = m1[c:] + jnp.dot(at_ref[s, :, h * GDN_DK:h * GDN_DK + c], v_new, preferred_element_type=F32)
        z = z_ref[s, :, sl]
        o_ref[s, :, sl] = (_rms(o, nrm_ref[...]) * (z * _sigmoid(z))).astype(o_ref.dtype)

    @pl.when(t == pl.num_programs(1) - 1)
    def _():
        s1_ref[...] = s_scr[...]


def _rec(u, w, qd, kd, at, z, gl, s0, shared_init, n_streams, t_len, sb, c, p):
    hd = GDN_HEADS * GDN_DK
    nt = t_len // c
    v3 = lambda a: a.reshape(n_streams, t_len, hd)
    blk3 = lambda: pl.BlockSpec((sb, c, hd), lambda g, t: (g, t, 0))
    s_spec = (pl.BlockSpec((1, GDN_HEADS, GDN_DK, GDN_DV), lambda g, t: (0, 0, 0, 0)) if shared_init else
              pl.BlockSpec((sb, GDN_HEADS, GDN_DK, GDN_DV), lambda g, t: (g, 0, 0, 0)))
    out, s1 = pl.pallas_call(
        functools.partial(_rec_kernel, sb=sb, c=c, shared_init=shared_init),
        grid=(n_streams // sb, nt),
        in_specs=[blk3(), blk3(), blk3(), blk3(), blk3(), blk3(),
                  pl.BlockSpec((sb, 1, 1, hd), lambda g, t: (g, t, 0, 0)),
                  s_spec,
                  pl.BlockSpec((1, GDN_DV), lambda g, t: (0, 0))],
        out_specs=[blk3(), pl.BlockSpec((sb, GDN_HEADS, GDN_DK, GDN_DV), lambda g, t: (g, 0, 0, 0))],
        out_shape=[jax.ShapeDtypeStruct((n_streams, t_len, hd), BF16),
                   jax.ShapeDtypeStruct((n_streams, GDN_HEADS, GDN_DK, GDN_DV), F32)],
        scratch_shapes=[pltpu.VMEM((sb, GDN_HEADS, GDN_DK, GDN_DV), F32)],
        compiler_params=_cparams(("arbitrary", "arbitrary")),
        name="gdn_state",
    )(v3(u), v3(w), v3(qd), v3(kd), v3(at), v3(z), gl.reshape(n_streams, nt, 1, hd), s0, p["gdn_norm"])
    return out.reshape(n_streams * t_len, hd), s1


def _post_kernel(lru_p, gdn_p, x_p, lru_s, gdn_s, x_s, wo1_ref, wo2_ref, nrm_ref, rwt_ref, rb_ref,
                 x1_ref, xn2_ref, route_ref, pw_ref, cnt_ref, *, tt, n_p):
    i = pl.program_id(0)
    is_p = i < n_p
    lru = jnp.where(is_p, lru_p[...], lru_s[...])
    gdn = jnp.where(is_p, gdn_p[...], gdn_s[...])
    x = jnp.where(is_p, x_p[...], x_s[...])
    m = (jnp.dot(lru, wo1_ref[...], preferred_element_type=F32)
         + jnp.dot(gdn, wo2_ref[...], preferred_element_type=F32))
    x1 = x + m
    x1_ref[...] = x1
    xn2 = _rms(x1, nrm_ref[...])
    _store_token_tiles(xn2_ref, 0, xn2)
    logits = lax.dot_general(rwt_ref[...], xn2.astype(BF16), (((1,), (1,)), ((), ())),
                             preferred_element_type=F32) + rb_ref[:, 0:1]
    e_id = lax.broadcasted_iota(jnp.int32, (N_EXPERTS, tt), 0)
    vals, idxs = [], []
    for _ in range(TOP_K):
        mx = jnp.max(logits, axis=0, keepdims=True)
        ix = jnp.min(jnp.where(logits == mx, e_id, N_EXPERTS), axis=0, keepdims=True)
        logits = jnp.where(e_id == ix, -jnp.inf, logits)
        vals.append(mx)
        idxs.append(ix)
    es = [jnp.exp(v - vals[0]) for v in vals]
    den = es[0] + es[1] + es[2] + es[3]
    onehot = jnp.zeros((N_EXPERTS, tt), F32)
    for ix in idxs:
        onehot = onehot + (e_id == ix).astype(F32)
    ri = lax.broadcasted_iota(jnp.int32, (tt, tt), 0)
    ci = lax.broadcasted_iota(jnp.int32, (tt, tt), 1)
    earlier = jnp.dot(onehot.astype(BF16), (ri < ci).astype(BF16), preferred_element_type=F32)
    cnt = jnp.sum(onehot, axis=1, keepdims=True)
    ei = lax.broadcasted_iota(jnp.int32, (N_EXPERTS, N_EXPERTS), 0)
    ej = lax.broadcasted_iota(jnp.int32, (N_EXPERTS, N_EXPERTS), 1)
    off = jnp.dot((ej < ei).astype(BF16), jnp.broadcast_to(cnt, (N_EXPERTS, LANES)).astype(BF16),
                  preferred_element_type=F32)[:, 0:1]
    place = earlier + off
    r8 = lax.broadcasted_iota(jnp.int32, (SUBLANES, tt), 0)
    route = jnp.zeros((SUBLANES, tt), jnp.int32)
    pw = jnp.zeros((SUBLANES, tt), F32)
    for kk in range(TOP_K):
        rank = jnp.sum(jnp.where(e_id == idxs[kk], place, 0.0), axis=0, keepdims=True).astype(jnp.int32)
        route = jnp.where(r8 == kk, idxs[kk], route)
        route = jnp.where(r8 == TOP_K + kk, rank, route)
        pw = jnp.where(r8 == kk, es[kk] / den, pw)
    route_ref[...] = route
    pw_ref[...] = pw
    le = lax.broadcasted_iota(jnp.int32, (N_EXPERTS, LANES), 0)
    ll = lax.broadcasted_iota(jnp.int32, (N_EXPERTS, LANES), 1)
    cnt_ref[0] = jnp.sum(jnp.where(le == ll, cnt, 0.0), axis=0, keepdims=True).astype(jnp.int32)


def _post(lru_p, gdn_p, x_p, lru_s, gdn_s, x_s, tt, p):
    n_p = x_p.shape[0] // tt
    n_s = x_s.shape[0] // tt
    n = x_p.shape[0] + x_s.shape[0]
    prow = lambda w: pl.BlockSpec((tt, w), lambda i: (jnp.minimum(i, n_p - 1), 0))
    srow = lambda w: pl.BlockSpec((tt, w), lambda i: (jnp.maximum(i - n_p, 0), 0))
    row = lambda w: pl.BlockSpec((tt, w), lambda i: (i, 0))
    full = lambda a, b: pl.BlockSpec((a, b), lambda i: (0, 0))
    return pl.pallas_call(
        functools.partial(_post_kernel, tt=tt, n_p=n_p),
        grid=(n_p + n_s,),
        in_specs=[prow(D_LRU), prow(D_LRU), prow(D_MODEL), srow(D_LRU), srow(D_LRU), srow(D_MODEL),
                  full(D_LRU, D_MODEL), full(D_LRU, D_MODEL),
                  full(1, D_MODEL), full(N_EXPERTS, D_MODEL), full(N_EXPERTS, LANES)],
        out_specs=[row(D_MODEL), pl.BlockSpec((tt * TOKEN_ROWS, LANES), lambda i: (i, 0)),
                   pl.BlockSpec((SUBLANES, tt), lambda i: (i, 0)), pl.BlockSpec((SUBLANES, tt), lambda i: (i, 0)),
                   pl.BlockSpec((1, 1, LANES), lambda i: (i, 0, 0))],
        out_shape=[jax.ShapeDtypeStruct((n, D_MODEL), F32), jax.ShapeDtypeStruct((n * TOKEN_ROWS, LANES), F32),
                   jax.ShapeDtypeStruct(((n_p + n_s) * SUBLANES, tt), jnp.int32),
                   jax.ShapeDtypeStruct(((n_p + n_s) * SUBLANES, tt), F32),
                   jax.ShapeDtypeStruct((n_p + n_s, 1, LANES), jnp.int32)],
        compiler_params=_cparams(("arbitrary",)),
        name="outproj_router",
    )(lru_p, gdn_p, x_p, lru_s, gdn_s, x_s, p["w_out1"], p["w_out2"], p["norm_ffn"], p["router_w"], p["router_b"])


def _run_copies(cnt_ref, off_ref, base_ref, vbuf, hbm, sem, to_hbm):
    for e in range(N_EXPERTS):
        cnt = cnt_ref[0, 0, e]
        off = off_ref[0, 0, e]
        base = base_ref[0, 0, e]
        for bit in RUN_BITS:
            done = (cnt & (-2 * bit)) * TOKEN_ROWS

            @pl.when((cnt & bit) != 0)
            def _():
                v = vbuf.at[pl.ds(pl.multiple_of(off + done, TOKEN_ROWS), bit * TOKEN_ROWS), :]
                h = hbm.at[pl.ds(pl.multiple_of(base + done, TOKEN_ROWS), bit * TOKEN_ROWS), :]
                if to_hbm:
                    pltpu.make_async_copy(v, h, sem).start()
                else:
                    pltpu.make_async_copy(h, v, sem).start()


def _wait_run_copies(vbuf, hbm, sem, to_hbm):
    h = hbm.at[pl.ds(0, vbuf.shape[0]), :]
    if to_hbm:
        pltpu.make_async_copy(vbuf, h, sem).wait()
    else:
        pltpu.make_async_copy(h, vbuf, sem).wait()


def _dispatch_kernel(zrow_ref, zcnt_ref, nu_ref, pos_ref, cnt_ref, off_ref, base_ref, x_ref, xs_hbm,
                     dbuf0, dbuf1, zbuf, sems, zsem, *, tt, blk):
    i = pl.program_id(0)
    n = pl.num_programs(0)
    dbufs = (dbuf0, dbuf1)

    def pad_copies(wait):
        for e in range(N_EXPERTS):
            cnt = zcnt_ref[e]
            row = zrow_ref[e]
            for bit in RUN_BITS:
                done = (cnt & (-2 * bit)) * TOKEN_ROWS

                @pl.when((cnt & bit) != 0)
                def _():
                    cp = pltpu.make_async_copy(
                        zbuf.at[pl.ds(0, bit * TOKEN_ROWS), :],
                        xs_hbm.at[pl.ds(pl.multiple_of(row + done, TOKEN_ROWS), bit * TOKEN_ROWS), :], zsem.at[0])
                    if wait:
                        cp.wait()
                    else:
                        cp.start()
        n_blocks = xs_hbm.shape[0] // (blk * TOKEN_ROWS)
        for j in range(N_EXPERTS):
            @pl.when(nu_ref[0] + j < n_blocks)
            def _():
                row = pl.multiple_of((nu_ref[0] + j) * (blk * TOKEN_ROWS), TOKEN_ROWS)
                cp = pltpu.make_async_copy(zbuf, xs_hbm.at[pl.ds(row, blk * TOKEN_ROWS), :], zsem.at[0])
                if wait:
                    cp.wait()
                else:
                    cp.start()

    @pl.when(i == 0)
    def _():
        zbuf[...] = jnp.zeros_like(zbuf)
        pad_copies(False)

    def step(cur):
        dbuf = dbufs[cur]

        @pl.when(i >= 2)
        def _():
            _wait_run_copies(dbuf, xs_hbm, sems.at[cur], True)

        for t in range(tt):
            v = x_ref[t * TOKEN_ROWS:(t + 1) * TOKEN_ROWS, :]
            for kk in range(TOP_K):
                dbuf[pl.ds(pl.multiple_of(pos_ref[0, 0, kk * tt + t], TOKEN_ROWS), TOKEN_ROWS), :] = v
        _run_copies(cnt_ref, off_ref, base_ref, dbuf, xs_hbm, sems.at[cur], True)

        @pl.when(i == n - 1)
        def _():
            _wait_run_copies(dbuf, xs_hbm, sems.at[cur], True)

            @pl.when(i >= 1)
            def _():
                _wait_run_copies(dbufs[1 - cur], xs_hbm, sems.at[1 - cur], True)

            pad_copies(True)

    @pl.when(i % 2 == 0)
    def _():
        step(0)

    @pl.when(i % 2 == 1)
    def _():
        step(1)


def _dispatch(zrow, zcnt, n_used, pos, cnt_t, off_t, base_t, xn2, n_slots, tt, blk):
    n_tiles = cnt_t.shape[0]
    smem = lambda w: pl.BlockSpec((1, 1, w), lambda i, zr, zc, nu: (i, 0, 0), memory_space=pltpu.SMEM)
    rows = TOP_K * tt * TOKEN_ROWS
    grid_spec = pltpu.PrefetchScalarGridSpec(
        num_scalar_prefetch=3,
        grid=(n_tiles,),
        in_specs=[smem(TOP_K * tt), smem(LANES), smem(LANES), smem(LANES),
                  pl.BlockSpec((tt * TOKEN_ROWS, LANES), lambda i, zr, zc, nu: (i, 0))],
        out_specs=pl.BlockSpec(memory_space=pl.ANY),
        scratch_shapes=[pltpu.VMEM((rows, LANES), F32), pltpu.VMEM((rows, LANES), F32),
                        pltpu.VMEM((blk * TOKEN_ROWS, LANES), F32),
                        pltpu.SemaphoreType.DMA((2,)), pltpu.SemaphoreType.DMA((1,))],
    )
    return pl.pallas_call(
        functools.partial(_dispatch_kernel, tt=tt, blk=blk),
        grid_spec=grid_spec,
        out_shape=jax.ShapeDtypeStruct((n_slots * TOKEN_ROWS, LANES), F32),
        compiler_params=_cparams(("arbitrary",)),
        name="moe_dispatch",
    )(zrow, zcnt, n_used, pos, cnt_t, off_t, base_t, xn2)


def _expert_kernel(be_ref, first_ref, nu_ref, x_ref, wg_ref, bg_ref, wu_ref, bu_ref, wd_ref, bd_ref, y_ref, wbf,
                   *, blk):
    b = pl.program_id(0)

    @pl.when((first_ref[b] == 1) & (b < nu_ref[0]))
    def _():
        wbf[0] = wg_ref[0].astype(BF16)
        wbf[1] = wu_ref[0].astype(BF16)
        wbf[2] = wd_ref[0].astype(BF16)

    @pl.when(b < nu_ref[0])
    def _():
        xb = _load_token_tiles(x_ref, 0, blk).astype(BF16)
        gt = jnp.minimum(jnp.dot(xb, wbf[0], preferred_element_type=F32) + bg_ref[0], SWIGLU_LIMIT)
        up = jnp.clip(jnp.dot(xb, wbf[1], preferred_element_type=F32) + bu_ref[0], -SWIGLU_LIMIT, SWIGLU_LIMIT)
        hid = (up + 1.0) * gt * _sigmoid(SWIGLU_ALPHA * gt)
        _store_token_tiles(y_ref, 0, jnp.dot(hid.astype(BF16), wbf[2], preferred_element_type=F32) + bd_ref[0])

    @pl.when(b >= nu_ref[0])
    def _():
        y_ref[...] = jnp.zeros_like(y_ref)


def _experts(block_expert, first, n_used, xs, p, blk):
    n_blocks = block_expert.shape[0]
    used = lambda b, nu: jnp.minimum(b, nu[0] - 1)
    wspec = lambda: pl.BlockSpec((1, D_MODEL, D_MODEL), lambda b, be, fi, nu: (be[used(b, nu)], 0, 0))
    bspec = lambda: pl.BlockSpec((1, 1, D_MODEL), lambda b, be, fi, nu: (be[used(b, nu)], 0, 0))
    rspec = lambda: pl.BlockSpec((blk * TOKEN_ROWS, LANES), lambda b, be, fi, nu: (used(b, nu), 0))
    grid_spec = pltpu.PrefetchScalarGridSpec(
        num_scalar_prefetch=3,
        grid=(n_blocks,),
        in_specs=[rspec(), wspec(), bspec(), wspec(), bspec(), wspec(), bspec()],
        out_specs=pl.BlockSpec((blk * TOKEN_ROWS, LANES), lambda b, be, fi, nu: (b, 0)),
        scratch_shapes=[pltpu.VMEM((3, D_MODEL, D_MODEL), BF16)],
    )
    return pl.pallas_call(
        functools.partial(_expert_kernel, blk=blk),
        grid_spec=grid_spec,
        out_shape=jax.ShapeDtypeStruct(xs.shape, F32),
        compiler_params=_cparams(("arbitrary",)),
        name="experts",
    )(block_expert, first, n_used, xs, p["exp_w_gate"], p["exp_b_gate"], p["exp_w_up"], p["exp_b_up"],
      p["exp_w_down"], p["exp_b_down"])


def _combine_kernel(pos_ref, pw_ref, cnt_ref, off_ref, base_ref, cntn_ref, offn_ref, basen_ref,
                    ys_hbm, x1_ref, nrm_ref, op_ref, os_ref, ybuf0, ybuf1, mbuf, sems, *, tt, n_p):
    i = pl.program_id(0)
    n = pl.num_programs(0)
    ybufs = (ybuf0, ybuf1)

    @pl.when(i == 0)
    def _():
        _run_copies(cnt_ref, off_ref, base_ref, ybuf0, ys_hbm, sems.at[0], False)

    def step(cur):
        @pl.when(i + 1 < n)
        def _():
            _run_copies(cntn_ref, offn_ref, basen_ref, ybufs[1 - cur], ys_hbm, sems.at[1 - cur], False)

        _wait_run_copies(ybufs[cur], ys_hbm, sems.at[cur], False)
        ybuf = ybufs[cur]
        for t in range(tt):
            acc = None
            for kk in range(TOP_K):
                j = kk * tt + t
                row = ybuf[pl.ds(pl.multiple_of(pos_ref[0, 0, j], TOKEN_ROWS), TOKEN_ROWS), :]
                term = pw_ref[0, 0, j] * row
                acc = term if acc is None else acc + term
            mbuf[t * TOKEN_ROWS:(t + 1) * TOKEN_ROWS, :] = acc

    @pl.when(i % 2 == 0)
    def _():
        step(0)

    @pl.when(i % 2 == 1)
    def _():
        step(1)

    res = _rms(x1_ref[...] + _load_token_tiles(mbuf, 0, tt), nrm_ref[...])

    @pl.when(i < n_p)
    def _():
        op_ref[...] = res

    @pl.when(i >= n_p)
    def _():
        os_ref[...] = res


def _combine(pos, pw, cnt_t, off_t, base_t, ys, x1, final_norm, n_prompt, tt):
    n = x1.shape[0]
    n_p = n_prompt // tt
    n_s = (n - n_prompt) // tt
    nt = n_p + n_s
    smem = lambda w: pl.BlockSpec((1, 1, w), lambda i: (i, 0, 0), memory_space=pltpu.SMEM)
    smem_next = lambda w: pl.BlockSpec((1, 1, w), lambda i: (jnp.minimum(i + 1, nt - 1), 0, 0),
                                       memory_space=pltpu.SMEM)
    rows = TOP_K * tt * TOKEN_ROWS
    return pl.pallas_call(
        functools.partial(_combine_kernel, tt=tt, n_p=n_p),
        grid=(nt,),
        in_specs=[smem(TOP_K * tt), smem(TOP_K * tt), smem(LANES), smem(LANES), smem(LANES),
                  smem_next(LANES), smem_next(LANES), smem_next(LANES),
                  pl.BlockSpec(memory_space=pl.ANY),
                  pl.BlockSpec((tt, D_MODEL), lambda i: (i, 0)),
                  pl.BlockSpec((1, D_MODEL), lambda i: (0, 0))],
        out_specs=[pl.BlockSpec((tt, D_MODEL), lambda i: (jnp.minimum(i, n_p - 1), 0)),
                   pl.BlockSpec((tt, D_MODEL), lambda i: (jnp.maximum(i - n_p, 0), 0))],
        out_shape=[jax.ShapeDtypeStruct((n_prompt, D_MODEL), F32),
                   jax.ShapeDtypeStruct((n - n_prompt, D_MODEL), F32)],
        scratch_shapes=[pltpu.VMEM((rows, LANES), F32), pltpu.VMEM((rows, LANES), F32),
                        pltpu.VMEM((tt * TOKEN_ROWS, LANES), F32), pltpu.SemaphoreType.DMA((2,))],
        compiler_params=_cparams(("arbitrary",)),
        name="moe_combine",
    )(pos, pw, cnt_t, off_t, base_t, cnt_t, off_t, base_t, ys, x1, final_norm)


def _pad_rows(a, rows):
    return jnp.pad(a, ((0, 0), (rows - a.shape[1], 0), (0, 0)))


def _tile(n, pref):
    t = pref
    while n % t:
        t //= 2
    return t


def _mixer_group(x2d, n_streams, t_len, lru_conv0, lru_h0, gdn_conv0, gdn_s0, shared_init, reset_first, chunk, p):
    tt = _tile(t_len, TOKEN_TILE)
    lru_out, u, w, qd, kd, at, z, gl, lru_conv1, lru_h1, gdn_conv1 = _pre(
        x2d, lru_conv0, lru_h0, gdn_conv0, shared_init, n_streams, t_len, tt, chunk, reset_first, p)
    sb = _tile(n_streams, REC_STREAMS)
    gdn_out, gdn_s1 = _rec(u, w, qd, kd, at, z, gl, gdn_s0, shared_init, n_streams, t_len, sb, chunk, p)
    return lru_out, gdn_out, (lru_conv1, lru_h1, gdn_conv1, gdn_s1)


def _routing(cnt_tiles, n_tok, blk):
    cnt_t = cnt_tiles[:, 0, :]
    counts = jnp.sum(cnt_t, axis=0)
    padded = (counts + blk - 1) // blk * blk
    pends = jnp.cumsum(padded)
    pstarts = pends - padded
    off_t = jnp.cumsum(cnt_t, axis=1) - cnt_t
    base_t = pstarts[None, :] + jnp.cumsum(cnt_t, axis=0) - cnt_t
    n_blocks = (n_tok * TOP_K + N_EXPERTS * (blk - 1) + blk - 1) // blk
    starts = jnp.arange(n_blocks, dtype=jnp.int32) * blk
    block_expert = jnp.minimum(jnp.sum((pends[None, :N_EXPERTS] <= starts[:, None]).astype(jnp.int32), axis=1),
                               N_EXPERTS - 1)
    first = jnp.concatenate([jnp.ones((1,), jnp.int32),
                             (block_expert[1:] != block_expert[:-1]).astype(jnp.int32)])
    n_used = (pends[N_EXPERTS - 1] // blk).astype(jnp.int32)[None]
    r3 = lambda a: (a * TOKEN_ROWS).astype(jnp.int32)[:, None, :]
    zrow = ((pstarts + counts) * TOKEN_ROWS).astype(jnp.int32)[:N_EXPERTS]
    zcnt = (padded - counts).astype(jnp.int32)[:N_EXPERTS]
    return (cnt_t.astype(jnp.int32)[:, None, :], r3(off_t), r3(base_t), block_expert, first, n_used, zrow, zcnt,
            n_blocks * blk)


def kernel(x_prompt, x_sample, state_lru_conv, state_lru_h, state_gdn_conv, state_gdn_S, meta_tokens, norm_mix, w_in, lru_conv_w, lru_conv_b, lru_w_a, lru_b_a, lru_w_x, lru_b_x, lru_lambda, lru_norm, gdn_conv_w, gdn_A_log, gdn_dt_bias, gdn_norm, w_out, norm_ffn, router_w, router_b, exp_w_gate, exp_b_gate, exp_w_up, exp_b_up, exp_w_down, exp_b_down, final_norm):
    bp, tp, _ = x_prompt.shape
    bs, ts, _ = x_sample.shape
    n_meta = meta_tokens.shape[0]
    n_prompt, n_sample = bp * tp, bs * ts
    tt = TOKEN_TILE
    assert n_prompt % tt == 0 and n_sample % tt == 0 and EXPERT_ROWS <= 2 * TOKEN_TILE

    def blockdiag(w):
        eye = jnp.eye(LRU_BLOCKS, dtype=w.dtype)
        return jnp.einsum("ncd,nm->ncmd", w, eye).reshape(D_LRU, D_LRU)

    lane_pad = lambda v: jnp.pad(v, (0, LANES - v.shape[0]))[None]
    p = dict(
        norm_mix=norm_mix[0][None],
        w_in=jnp.pad(w_in[0], ((0, 0), (0, D_IN_PAD - D_IN))).astype(BF16),
        lru_conv_w=lru_conv_w[0], lru_conv_b=lru_conv_b[0][None],
        lru_wa_bd=blockdiag(lru_w_a[0]).astype(BF16), lru_b_a=lru_b_a[0][None],
        lru_wx_bd=blockdiag(lru_w_x[0]).astype(BF16), lru_b_x=lru_b_x[0][None],
        lru_lambda=lru_lambda[0][None], lru_norm=lru_norm[0][None],
        gdn_conv_w=gdn_conv_w[0],
        gdn_par=jnp.concatenate([lane_pad(gdn_A_log[0]), lane_pad(gdn_dt_bias[0])], axis=0),
        gdn_norm=gdn_norm[0][None],
        w_out1=w_out[0][:D_LRU].astype(BF16), w_out2=w_out[0][D_LRU:].astype(BF16),
        norm_ffn=norm_ffn[0][None],
        router_w=router_w[0].T.astype(BF16),
        router_b=jnp.broadcast_to(router_b[0][:, None], (N_EXPERTS, LANES)),
        exp_w_gate=exp_w_gate[0], exp_b_gate=exp_b_gate[0][:, None, :],
        exp_w_up=exp_w_up[0], exp_b_up=exp_b_up[0][:, None, :],
        exp_w_down=exp_w_down[0], exp_b_down=exp_b_down[0][:, None, :],
    )

    zc = lambda c: jnp.zeros((1, SUBLANES, c), F32)
    _, _, m_state = _mixer_group(meta_tokens, 1, n_meta, zc(D_LRU), jnp.zeros((1, 1, D_LRU), F32), zc(GDN_QKV),
                                 jnp.zeros((1, GDN_HEADS, GDN_DK, GDN_DV), F32), False, True, n_meta, p)
    xp2 = x_prompt.reshape(n_prompt, D_MODEL)
    xs2 = x_sample.reshape(n_sample, D_MODEL)
    p_lru, p_gdn, p_state = _mixer_group(xp2, bp, tp, m_state[0], m_state[1], m_state[2], m_state[3], True, False,
                                         GDN_CHUNK, p)
    s_lru, s_gdn, s_state = _mixer_group(xs2, bs, ts, _pad_rows(state_lru_conv[0], SUBLANES),
                                         state_lru_h[0][:, None, :], _pad_rows(state_gdn_conv[0], SUBLANES),
                                         state_gdn_S[0], False, False, GDN_CHUNK, p)

    x1, xn2, route, pw, cnt_tiles = _post(p_lru, p_gdn, xp2, s_lru, s_gdn, xs2, tt, p)
    n_tok = n_prompt + n_sample
    cnt_t, off_t, base_t, block_expert, first, n_used, zrow, zcnt, n_slots = _routing(cnt_tiles, n_tok, EXPERT_ROWS)
    n_tiles = n_tok // tt
    pos = (route.reshape(n_tiles, SUBLANES, tt)[:, TOP_K:2 * TOP_K] * TOKEN_ROWS).reshape(n_tiles, 1, TOP_K * tt)
    pw_s = pw.reshape(n_tiles, SUBLANES, tt)[:, :TOP_K].reshape(n_tiles, 1, TOP_K * tt)
    xs = _dispatch(zrow, zcnt, n_used, pos, cnt_t, off_t, base_t, xn2, n_slots, tt, EXPERT_ROWS)
    ys = _experts(block_expert, first, n_used, xs, p, EXPERT_ROWS)
    y_p, y_s = _combine(pos, pw_s, cnt_t, off_t, base_t, ys, x1, final_norm[None], n_prompt, tt)

    def states(st, b):
        return (st[0][:, SUBLANES - 3:, :][None], st[1].reshape(1, b, D_LRU), st[2][:, SUBLANES - 3:, :][None],
                st[3][None])

    return (y_p.reshape(bp, tp, D_MODEL), y_s.reshape(bs, ts, D_MODEL)) + states(p_state, bp) + states(s_state, bs)
```

```python
import functools

import jax
import jax.numpy as jnp
from jax import lax
from jax.experimental import pallas as pl
from jax.experimental.pallas import tpu as pltpu

F32 = jnp.float32
BF16 = jnp.bfloat16

D_MODEL = 1024
D_LRU = 512
LRU_BLOCKS = 8
LRU_C = 8.0
CONV_W = 4
GDN_HEADS = 4
GDN_DK = 128
GDN_DV = 128
GDN_QKV = GDN_HEADS * (2 * GDN_DK + GDN_DV)
D_IN = 2 * D_LRU + GDN_QKV + GDN_HEADS * GDN_DV + 2 * GDN_HEADS
N_EXPERTS = 32
TOP_K = 4
SWIGLU_LIMIT = 7.0
SWIGLU_ALPHA = 1.702
EPS = 1e-6

LANES = 128
SUBLANES = 8
D_IN_PAD = 3200
COL_AB = (2 * D_LRU + GDN_QKV + GDN_HEADS * GDN_DV) // LANES
VMEM_LIMIT = 56 * 1024 * 1024

TOKEN_TILE = 256
MOE_TILE = 512
EXPERT_ROWS = 512
GDN_CHUNK = 64
REC_STREAMS = 8
PRE_CHAINS = 8
PROJ_PIECE = 256
PROJ_EARLY_POINTS = 8
TOKEN_ROWS = D_MODEL // LANES
RUN_BITS = tuple(1 << i for i in range(MOE_TILE.bit_length() - 1, -1, -1))


def _cparams(sem):
    return pltpu.CompilerParams(dimension_semantics=sem, vmem_limit_bytes=VMEM_LIMIT)


def _rms(x, gain):
    return x * lax.rsqrt(jnp.mean(x * x, axis=-1, keepdims=True) + EPS) * gain


def _softplus(x):
    return jnp.maximum(x, 0.0) + jnp.log1p(jnp.exp(-jnp.abs(x)))


def _sigmoid(x):
    return 1.0 / (1.0 + jnp.exp(-x))


def _load_token_tiles(ref, tok0, n_tok):
    return jnp.concatenate(
        [ref[pl.ds(tok0 * TOKEN_ROWS + j, n_tok, stride=TOKEN_ROWS), :] for j in range(TOKEN_ROWS)], axis=1)


def _store_token_tiles(ref, tok0, val):
    for j in range(TOKEN_ROWS):
        ref[pl.ds(tok0 * TOKEN_ROWS + j, val.shape[0], stride=TOKEN_ROWS), :] = val[:, j * LANES:(j + 1) * LANES]


def _mm(a, b):
    return jnp.dot(a.astype(BF16), b.astype(BF16), preferred_element_type=F32)


def _mm_nt(a, b):
    return lax.dot_general(a.astype(BF16), b.astype(BF16), (((1,), (1,)), ((), ())), preferred_element_type=F32)


def _conv_step(buf, x, w_ref, tt):
    buf[SUBLANES:SUBLANES + tt, :] = x
    y = x * w_ref[CONV_W - 1:CONV_W, :]
    for j in range(CONV_W - 2, -1, -1):
        y = y + buf[SUBLANES - 3 + j:SUBLANES - 3 + j + tt, :] * w_ref[j:j + 1, :]
    tail = buf[tt:tt + SUBLANES, :]
    buf[0:SUBLANES, :] = tail
    return y


def _pre_kernel(x_ref, xnext_ref, nrm_ref, win_ref, lconv0_ref, h0_ref, gconv0_ref,
                lcw_ref, lcb_ref, wa_ref, ba_ref, wx_ref, bx_ref, lam_ref, lnrm_ref, gcw_ref, par_ref,
                lru_ref, u_ref, w_ref, qd_ref, kd_ref, at_ref, z_ref, gl_ref, lconv1_ref, h1_ref, gconv1_ref,
                lbuf, a_buf, b_buf, hcar, gbuf, pbuf, *, tt, c, reset_first):
    t = pl.program_id(1)
    hd = GDN_HEADS * GDN_DK

    @pl.when(t == 0)
    def _():
        lbuf[0:SUBLANES, :] = lconv0_ref[0]
        hcar[...] = h0_ref[0]
        for j in range(3):
            gbuf[j, 0:SUBLANES, :] = gconv0_ref[0, :, j * hd:(j + 1) * hd]

    @pl.when((pl.program_id(0) == 0) & (t == 0))
    def _():
        pbuf[...] = jnp.dot(_rms(x_ref[...], nrm_ref[...]).astype(BF16), win_ref[...], preferred_element_type=F32)

    th = tt
    projs = [pbuf]
    xnext = _rms(xnext_ref[...], nrm_ref[...]).astype(BF16)
    pieces = [(lo, min(lo + PROJ_PIECE, D_IN_PAD)) for lo in range(0, D_IN_PAD, PROJ_PIECE)]
    z_col = 2 * D_LRU + GDN_QKV
    assert z_col % PROJ_PIECE == 0
    pieces = [pc for pc in pieces if pc[0] >= z_col] + [pc for pc in pieces if pc[0] < z_col]
    n_levels = max(c.bit_length() - 1, 1)
    n_solve_points = -(-(tt // c) * GDN_HEADS // PRE_CHAINS) * n_levels
    n_late = max(len(pieces) - PROJ_EARLY_POINTS, 0)
    plan = [1] * PROJ_EARLY_POINTS + [(i + 1) * n_late // n_solve_points - i * n_late // n_solve_points
                                      for i in range(n_solve_points)]
    points_done = [0]
    pieces_done = [0]

    def next_proj_pieces():
        k = plan[points_done[0]]
        points_done[0] += 1
        for lo, hi in pieces[pieces_done[0]:pieces_done[0] + k]:
            pbuf[:, lo:hi] = jnp.dot(xnext, win_ref[:, lo:hi], preferred_element_type=F32)
        pieces_done[0] += k

    rowc = lax.broadcasted_iota(jnp.int32, (c, LANES), 0)
    ri = lax.broadcasted_iota(jnp.int32, (c, c), 0)
    ci = lax.broadcasted_iota(jnp.int32, (c, c), 1)
    causal = ri >= ci
    strict = ri > ci
    eye = ri == ci
    lane_pad = jnp.zeros((c, LANES - c), F32)
    row8 = lax.broadcasted_iota(jnp.int32, (SUBLANES, D_LRU), 0)

    def group(gi, h):
        g0 = pl.multiple_of(gi * SUBLANES, SUBLANES)
        a8 = a_buf[pl.ds(g0, SUBLANES), :]
        b8 = b_buf[pl.ds(g0, SUBLANES), :]
        for d in (1, 2, 4):
            keep = row8 >= d
            b8 = jnp.where(keep, a8 * pltpu.roll(b8, d, 0) + b8, b8)
            a8 = jnp.where(keep, a8 * pltpu.roll(a8, d, 0), a8)
        h8 = a8 * h + b8
        b_buf[pl.ds(g0, SUBLANES), :] = h8
        return h8[SUBLANES - 1:SUBLANES, :]

    h_last = hcar[...]
    for part, proj in enumerate(projs):
        r0 = part * th

        z_ref[r0:r0 + th, :] = proj[:, 2 * D_LRU + GDN_QKV:2 * D_LRU + GDN_QKV + hd]
        ab = proj[:, COL_AB * LANES:(COL_AB + 1) * LANES]
        g_all = -jnp.exp(par_ref[0:1, :]) * _softplus(ab + par_ref[1:2, :])
        beta_all = _sigmoid(ab)
        next_proj_pieces()

        xc = _conv_step(lbuf, proj[:, 0:D_LRU], lcw_ref, th) + lcb_ref[...]
        next_proj_pieces()
        xb = xc.astype(BF16)
        r = _sigmoid(jnp.dot(xb, wa_ref[...], preferred_element_type=F32) + ba_ref[...])
        i = _sigmoid(jnp.dot(xb, wx_ref[...], preferred_element_type=F32) + bx_ref[...])
        next_proj_pieces()
        log_a = (-LRU_C) * r * _softplus(-lam_ref[...])
        a = jnp.exp(log_a)
        mult = jnp.sqrt(-jnp.tanh(log_a) * (a * a + 1.0))
        if reset_first and part == 0:
            row = lax.broadcasted_iota(jnp.int32, (th, D_LRU), 0)
            mult = jnp.where((row == 0) & (t == 0), 1.0, mult)
        a_buf[0:th, :] = a
        b_buf[0:th, :] = mult * i * xc
        next_proj_pieces()
        h_last = lax.fori_loop(0, th // SUBLANES, group, h_last)
        lru_ref[r0:r0 + th, :] = _rms(b_buf[0:th, :] * jax.nn.gelu(proj[:, D_LRU:2 * D_LRU]),
                                      lnrm_ref[...]).astype(lru_ref.dtype)
        next_proj_pieces()

        def conv_silu(j, proj=proj):
            y = _conv_step(gbuf.at[j], proj[:, 2 * D_LRU + j * hd:2 * D_LRU + (j + 1) * hd],
                           gcw_ref.at[:, j * hd:(j + 1) * hd], th)
            return y * _sigmoid(y)

        qa = conv_silu(0)
        next_proj_pieces()
        ka = conv_silu(1)
        next_proj_pieces()
        va = conv_silu(2)
        next_proj_pieces()

        pairs = [(j, h) for j in range(th // c) for h in range(GDN_HEADS)]
        for p0 in range(0, len(pairs), PRE_CHAINS):
            group_pairs = pairs[p0:p0 + PRE_CHAINS]
            gcs = {}
            for j in sorted({j for j, _ in group_pairs}):
                gc = g_all[j * c:(j + 1) * c]
                d = 1
                while d < c:
                    gc = gc + jnp.where(rowc >= d, pltpu.roll(gc, d, 0), 0.0)
                    d *= 2
                gcs[j] = gc
            ns, xs = [], []
            for j, h in group_pairs:
                rows = slice(j * c, (j + 1) * c)
                orow = slice(r0 + j * c, r0 + (j + 1) * c)
                sl = slice(h * GDN_DK, (h + 1) * GDN_DK)
                gc = gcs[j]
                q = qa[rows, sl]
                k = ka[rows, sl]
                q = q * lax.rsqrt(jnp.sum(q * q, axis=-1, keepdims=True) + EPS) * (GDN_DK ** -0.5)
                k = k * lax.rsqrt(jnp.sum(k * k, axis=-1, keepdims=True) + EPS)
                gcol = gc[:, h:h + 1]
                bcol = beta_all[rows, GDN_HEADS + h:GDN_HEADS + h + 1]
                grow = jnp.sum(jnp.where(eye, gcol, 0.0), axis=0, keepdims=True)
                decay = jnp.where(causal, jnp.exp(jnp.where(causal, gcol - grow, 0.0)), 0.0)
                egc = jnp.exp(gcol)
                g_last = gc[c - 1:c, h:h + 1]
                kb = k * bcol
                qd_ref[orow, sl] = (q * egc).astype(qd_ref.dtype)
                kd_ref[orow, sl] = (k * jnp.exp(g_last - gcol)).astype(kd_ref.dtype)
                gl_ref[r0 // c + j, :, sl] = jnp.broadcast_to(jnp.exp(g_last), (1, GDN_DV))
                at_ref[orow, sl] = jnp.concatenate([_mm_nt(q, k) * decay, lane_pad], axis=1).astype(at_ref.dtype)
                ns.append(jnp.where(strict, _mm_nt(kb, k) * decay, 0.0))
                xs.append(jnp.concatenate([va[rows, sl] * bcol, kb * egc], axis=1))
            xs = [x - _mm(n, x) for n, x in zip(ns, xs)]
            next_proj_pieces()
            m = 2
            while m < c:
                ns = [_mm(n, n) for n in ns]
                xs = [x + _mm(n, x) for n, x in zip(ns, xs)]
                next_proj_pieces()
                m *= 2
            for (j, h), x in zip(group_pairs, xs):
                orow = slice(r0 + j * c, r0 + (j + 1) * c)
                sl = slice(h * GDN_DK, (h + 1) * GDN_DK)
                u_ref[orow, sl] = x[:, :GDN_DV]
                w_ref[orow, sl] = x[:, GDN_DV:].astype(w_ref.dtype)
    hcar[...] = h_last
    assert points_done[0] == len(plan) and pieces_done[0] == len(pieces)

    @pl.when(t == pl.num_programs(1) - 1)
    def _():
        lconv1_ref[0] = lbuf[0:SUBLANES, :]
        h1_ref[0] = h_last
        for j in range(3):
            gconv1_ref[0, :, j * hd:(j + 1) * hd] = gbuf[j, 0:SUBLANES, :]


def _pre(x2d, lconv0, h0, gconv0, shared_init, n_streams, t_len, tt, c, reset_first, p):
    nt = t_len // tt
    hd = GDN_HEADS * GDN_DK
    n = n_streams * t_len
    st = (lambda s, t: (0, 0, 0)) if shared_init else (lambda s, t: (s, 0, 0))
    full = lambda a, b: pl.BlockSpec((a, b), lambda s, t: (0, 0))
    row = lambda w: pl.BlockSpec((tt, w), lambda s, t: (s * nt + t, 0))
    return pl.pallas_call(
        functools.partial(_pre_kernel, tt=tt, c=c, reset_first=reset_first),
        grid=(n_streams, nt),
        in_specs=[row(D_MODEL),
                  pl.BlockSpec((tt, D_MODEL), lambda s, t: (jnp.minimum(s * nt + t + 1, n_streams * nt - 1), 0)),
                  full(1, D_MODEL), full(D_MODEL, D_IN_PAD),
                  pl.BlockSpec((1, SUBLANES, D_LRU), st), pl.BlockSpec((1, 1, D_LRU), st),
                  pl.BlockSpec((1, SUBLANES, GDN_QKV), st),
                  full(CONV_W, D_LRU), full(1, D_LRU), full(D_LRU, D_LRU), full(1, D_LRU), full(D_LRU, D_LRU),
                  full(1, D_LRU), full(1, D_LRU), full(1, D_LRU), full(CONV_W, GDN_QKV), full(2, LANES)],
        out_specs=[row(D_LRU), row(hd), row(hd), row(hd), row(hd), row(hd), row(hd),
                   pl.BlockSpec((tt // c, 1, hd), lambda s, t: (s * nt + t, 0, 0)),
                   pl.BlockSpec((1, SUBLANES, D_LRU), lambda s, t: (s, 0, 0)),
                   pl.BlockSpec((1, 1, D_LRU), lambda s, t: (s, 0, 0)),
                   pl.BlockSpec((1, SUBLANES, GDN_QKV), lambda s, t: (s, 0, 0))],
        out_shape=[jax.ShapeDtypeStruct((n, D_LRU), BF16),
                   jax.ShapeDtypeStruct((n, hd), F32),
                   jax.ShapeDtypeStruct((n, hd), BF16),
                   jax.ShapeDtypeStruct((n, hd), BF16),
                   jax.ShapeDtypeStruct((n, hd), BF16),
                   jax.ShapeDtypeStruct((n, hd), BF16),
                   jax.ShapeDtypeStruct((n, hd), F32),
                   jax.ShapeDtypeStruct((n // c, 1, hd), F32),
                   jax.ShapeDtypeStruct((n_streams, SUBLANES, D_LRU), F32),
                   jax.ShapeDtypeStruct((n_streams, 1, D_LRU), F32),
                   jax.ShapeDtypeStruct((n_streams, SUBLANES, GDN_QKV), F32)],
        scratch_shapes=[pltpu.VMEM((tt + SUBLANES, D_LRU), F32),
                        pltpu.VMEM((tt, D_LRU), F32),
                        pltpu.VMEM((tt, D_LRU), F32),
                        pltpu.VMEM((1, D_LRU), F32),
                        pltpu.VMEM((3, tt + SUBLANES, hd), F32),
                        pltpu.VMEM((tt, D_IN_PAD), F32)],
        compiler_params=_cparams(("arbitrary", "arbitrary")),
        name="mixer_pre",
    )(x2d, x2d, p["norm_mix"], p["w_in"], lconv0, h0, gconv0, p["lru_conv_w"], p["lru_conv_b"], p["lru_wa_bd"],
      p["lru_b_a"], p["lru_wx_bd"], p["lru_b_x"], p["lru_lambda"], p["lru_norm"], p["gdn_conv_w"], p["gdn_par"])


def _rec_kernel(u_ref, w_ref, qd_ref, kd_ref, at_ref, z_ref, gl_ref, s0_ref, nrm_ref, o_ref, s1_ref, s_scr,
                *, sb, c, shared_init):
    t = pl.program_id(1)

    @pl.when(t == 0)
    def _():
        for s in range(sb):
            s_scr[s] = s0_ref[0 if shared_init else s]

    chains = [(s, h, slice(h * GDN_DK, (h + 1) * GDN_DK)) for s in range(sb) for h in range(GDN_HEADS)]
    m1s = [jnp.dot(jnp.concatenate([w_ref[s, :, sl], qd_ref[s, :, sl]], axis=0), s_scr[s, h].astype(BF16),
                   preferred_element_type=F32) for s, h, sl in chains]
    vns = [(u_ref[s, :, sl] - m1[:c]).astype(BF16) for (s, h, sl), m1 in zip(chains, m1s)]
    for (s, h, sl), v_new in zip(chains, vns):
        s_scr[s, h] = s_scr[s, h] * gl_ref[s, 0, :, sl] + lax.dot_general(
            kd_ref[s, :, sl], v_new, (((0,), (0,)), ((), ())), preferred_element_type=F32)
    for (s, h, sl), m1, v_new in zip(chains, m1s, vns):
        o = m1[c:] + jnp.dot(at_ref[s, :, h * GDN_DK:h * GDN_DK + c], v_new, preferred_element_type=F32)
        z = z_ref[s, :, sl]
        o_ref[s, :, sl] = (_rms(o, nrm_ref[...]) * (z * _sigmoid(z))).astype(o_ref.dtype)

    @pl.when(t == pl.num_programs(1) - 1)
    def _():
        s1_ref[...] = s_scr[...]


def _rec(u, w, qd, kd, at, z, gl, s0, shared_init, n_streams, t_len, sb, c, p):
    hd = GDN_HEADS * GDN_DK
    nt = t_len // c
    v3 = lambda a: a.reshape(n_streams, t_len, hd)
    blk3 = lambda: pl.BlockSpec((sb, c, hd), lambda g, t: (g, t, 0))
    s_spec = (pl.BlockSpec((1, GDN_HEADS, GDN_DK, GDN_DV), lambda g, t: (0, 0, 0, 0)) if shared_init else
              pl.BlockSpec((sb, GDN_HEADS, GDN_DK, GDN_DV), lambda g, t: (g, 0, 0, 0)))
    out, s1 = pl.pallas_call(
        functools.partial(_rec_kernel, sb=sb, c=c, shared_init=shared_init),
        grid=(n_streams // sb, nt),
        in_specs=[blk3(), blk3(), blk3(), blk3(), blk3(), blk3(),
                  pl.BlockSpec((sb, 1, 1, hd), lambda g, t: (g, t, 0, 0)),
                  s_spec,
                  pl.BlockSpec((1, GDN_DV), lambda g, t: (0, 0))],
        out_specs=[blk3(), pl.BlockSpec((sb, GDN_HEADS, GDN_DK, GDN_DV), lambda g, t: (g, 0, 0, 0))],
        out_shape=[jax.ShapeDtypeStruct((n_streams, t_len, hd), BF16),
                   jax.ShapeDtypeStruct((n_streams, GDN_HEADS, GDN_DK, GDN_DV), F32)],
        scratch_shapes=[pltpu.VMEM((sb, GDN_HEADS, GDN_DK, GDN_DV), F32)],
        compiler_params=_cparams(("arbitrary", "arbitrary")),
        name="gdn_state",
    )(v3(u), v3(w), v3(qd), v3(kd), v3(at), v3(z), gl.reshape(n_streams, nt, 1, hd), s0, p["gdn_norm"])
    return out.reshape(n_streams * t_len, hd), s1


def _post_kernel(lru_p, gdn_p, x_p, lru_s, gdn_s, x_s, wo1_ref, wo2_ref, nrm_ref, rwt_ref, rb_ref,
                 x1_ref, xn2_ref, route_ref, pw_ref, cnt_ref, *, tt, n_p):
    i = pl.program_id(0)
    is_p = i < n_p
    lru = jnp.where(is_p, lru_p[...], lru_s[...])
    gdn = jnp.where(is_p, gdn_p[...], gdn_s[...])
    x = jnp.where(is_p, x_p[...], x_s[...])
    m = (jnp.dot(lru, wo1_ref[...], preferred_element_type=F32)
         + jnp.dot(gdn, wo2_ref[...], preferred_element_type=F32))
    x1 = x + m
    x1_ref[...] = x1
    xn2 = _rms(x1, nrm_ref[...])
    _store_token_tiles(xn2_ref, 0, xn2)
    logits = lax.dot_general(rwt_ref[...], xn2.astype(BF16), (((1,), (1,)), ((), ())),
                             preferred_element_type=F32) + rb_ref[:, 0:1]
    e_id = lax.broadcasted_iota(jnp.int32, (N_EXPERTS, tt), 0)
    vals, idxs = [], []
    for _ in range(TOP_K):
        mx = jnp.max(logits, axis=0, keepdims=True)
        ix = jnp.min(jnp.where(logits == mx, e_id, N_EXPERTS), axis=0, keepdims=True)
        logits = jnp.where(e_id == ix, -jnp.inf, logits)
        vals.append(mx)
        idxs.append(ix)
    es = [jnp.exp(v - vals[0]) for v in vals]
    den = es[0] + es[1] + es[2] + es[3]
    onehot = jnp.zeros((N_EXPERTS, tt), F32)
    for ix in idxs:
        onehot = onehot + (e_id == ix).astype(F32)
    ri = lax.broadcasted_iota(jnp.int32, (tt, tt), 0)
    ci = lax.broadcasted_iota(jnp.int32, (tt, tt), 1)
    earlier = jnp.dot(onehot.astype(BF16), (ri < ci).astype(BF16), preferred_element_type=F32)
    cnt = jnp.sum(onehot, axis=1, keepdims=True)
    ei = lax.broadcasted_iota(jnp.int32, (N_EXPERTS, N_EXPERTS), 0)
    ej = lax.broadcasted_iota(jnp.int32, (N_EXPERTS, N_EXPERTS), 1)
    lower = (ej < ei).astype(BF16)
    cnt_hi = jnp.floor(cnt * (1.0 / 16.0))
    excl = lambda v: jnp.dot(lower, jnp.broadcast_to(v, (N_EXPERTS, LANES)).astype(BF16),
                             preferred_element_type=F32)[:, 0:1]
    off = 16.0 * excl(cnt_hi) + excl(cnt - 16.0 * cnt_hi)
    place = earlier + off
    r8 = lax.broadcasted_iota(jnp.int32, (SUBLANES, tt), 0)
    route = jnp.zeros((SUBLANES, tt), jnp.int32)
    pw = jnp.zeros((SUBLANES, tt), F32)
    for kk in range(TOP_K):
        rank = jnp.sum(jnp.where(e_id == idxs[kk], place, 0.0), axis=0, keepdims=True).astype(jnp.int32)
        route = jnp.where(r8 == kk, idxs[kk], route)
        route = jnp.where(r8 == TOP_K + kk, rank, route)
        pw = jnp.where(r8 == kk, es[kk] / den, pw)
    route_ref[...] = route
    pw_ref[...] = pw
    le = lax.broadcasted_iota(jnp.int32, (N_EXPERTS, LANES), 0)
    ll = lax.broadcasted_iota(jnp.int32, (N_EXPERTS, LANES), 1)
    cnt_ref[0] = jnp.sum(jnp.where(le == ll, cnt, 0.0), axis=0, keepdims=True).astype(jnp.int32)


def _post(lru_p, gdn_p, x_p, lru_s, gdn_s, x_s, tt, p):
    n_p = x_p.shape[0] // tt
    n_s = x_s.shape[0] // tt
    n = x_p.shape[0] + x_s.shape[0]
    prow = lambda w: pl.BlockSpec((tt, w), lambda i: (jnp.minimum(i, n_p - 1), 0))
    srow = lambda w: pl.BlockSpec((tt, w), lambda i: (jnp.maximum(i - n_p, 0), 0))
    row = lambda w: pl.BlockSpec((tt, w), lambda i: (i, 0))
    full = lambda a, b: pl.BlockSpec((a, b), lambda i: (0, 0))
    return pl.pallas_call(
        functools.partial(_post_kernel, tt=tt, n_p=n_p),
        grid=(n_p + n_s,),
        in_specs=[prow(D_LRU), prow(D_LRU), prow(D_MODEL), srow(D_LRU), srow(D_LRU), srow(D_MODEL),
                  full(D_LRU, D_MODEL), full(D_LRU, D_MODEL),
                  full(1, D_MODEL), full(N_EXPERTS, D_MODEL), full(N_EXPERTS, LANES)],
        out_specs=[row(D_MODEL), pl.BlockSpec((tt * TOKEN_ROWS, LANES), lambda i: (i, 0)),
                   pl.BlockSpec((SUBLANES, tt), lambda i: (i, 0)), pl.BlockSpec((SUBLANES, tt), lambda i: (i, 0)),
                   pl.BlockSpec((1, 1, LANES), lambda i: (i, 0, 0))],
        out_shape=[jax.ShapeDtypeStruct((n, D_MODEL), F32), jax.ShapeDtypeStruct((n * TOKEN_ROWS, LANES), F32),
                   jax.ShapeDtypeStruct(((n_p + n_s) * SUBLANES, tt), jnp.int32),
                   jax.ShapeDtypeStruct(((n_p + n_s) * SUBLANES, tt), F32),
                   jax.ShapeDtypeStruct((n_p + n_s, 1, LANES), jnp.int32)],
        compiler_params=_cparams(("arbitrary",)),
        name="outproj_router",
    )(lru_p, gdn_p, x_p, lru_s, gdn_s, x_s, p["w_out1"], p["w_out2"], p["norm_ffn"], p["router_w"], p["router_b"])


def _run_copies(cnt_ref, off_ref, base_ref, vbuf, hbm, sem, to_hbm):
    for e in range(N_EXPERTS):
        cnt = cnt_ref[0, 0, e]
        off = off_ref[0, 0, e]
        base = base_ref[0, 0, e]
        for bit in RUN_BITS:
            done = (cnt & (-2 * bit)) * TOKEN_ROWS

            @pl.when((cnt & bit) != 0)
            def _():
                v = vbuf.at[pl.ds(pl.multiple_of(off + done, TOKEN_ROWS), bit * TOKEN_ROWS), :]
                h = hbm.at[pl.ds(pl.multiple_of(base + done, TOKEN_ROWS), bit * TOKEN_ROWS), :]
                if to_hbm:
                    pltpu.make_async_copy(v, h, sem).start()
                else:
                    pltpu.make_async_copy(h, v, sem).start()


def _wait_run_copies(vbuf, hbm, sem, to_hbm):
    h = hbm.at[pl.ds(0, vbuf.shape[0]), :]
    if to_hbm:
        pltpu.make_async_copy(vbuf, h, sem).wait()
    else:
        pltpu.make_async_copy(h, vbuf, sem).wait()


def _dispatch_kernel(zrow_ref, zcnt_ref, nu_ref, pos_ref, cnt_ref, off_ref, base_ref, x_ref, xs_hbm,
                     dbuf0, dbuf1, zbuf, sems, zsem, *, tt, blk):
    i = pl.program_id(0)
    n = pl.num_programs(0)
    dbufs = (dbuf0, dbuf1)

    def pad_copies(wait):
        for e in range(N_EXPERTS):
            cnt = zcnt_ref[e]
            row = zrow_ref[e]
            for bit in RUN_BITS:
                done = (cnt & (-2 * bit)) * TOKEN_ROWS

                @pl.when((cnt & bit) != 0)
                def _():
                    cp = pltpu.make_async_copy(
                        zbuf.at[pl.ds(0, bit * TOKEN_ROWS), :],
                        xs_hbm.at[pl.ds(pl.multiple_of(row + done, TOKEN_ROWS), bit * TOKEN_ROWS), :], zsem.at[0])
                    if wait:
                        cp.wait()
                    else:
                        cp.start()
        n_blocks = xs_hbm.shape[0] // (blk * TOKEN_ROWS)
        for j in range(N_EXPERTS):
            @pl.when(nu_ref[0] + j < n_blocks)
            def _():
                row = pl.multiple_of((nu_ref[0] + j) * (blk * TOKEN_ROWS), TOKEN_ROWS)
                cp = pltpu.make_async_copy(zbuf, xs_hbm.at[pl.ds(row, blk * TOKEN_ROWS), :], zsem.at[0])
                if wait:
                    cp.wait()
                else:
                    cp.start()

    @pl.when(i == 0)
    def _():
        zbuf[...] = jnp.zeros_like(zbuf)
        pad_copies(False)

    def step(cur):
        dbuf = dbufs[cur]

        @pl.when(i >= 2)
        def _():
            _wait_run_copies(dbuf, xs_hbm, sems.at[cur], True)

        for t in range(tt):
            v = x_ref[t * TOKEN_ROWS:(t + 1) * TOKEN_ROWS, :]
            for kk in range(TOP_K):
                dbuf[pl.ds(pl.multiple_of(pos_ref[0, 0, kk * tt + t], TOKEN_ROWS), TOKEN_ROWS), :] = v
        _run_copies(cnt_ref, off_ref, base_ref, dbuf, xs_hbm, sems.at[cur], True)

        @pl.when(i == n - 1)
        def _():
            _wait_run_copies(dbuf, xs_hbm, sems.at[cur], True)

            @pl.when(i >= 1)
            def _():
                _wait_run_copies(dbufs[1 - cur], xs_hbm, sems.at[1 - cur], True)

            pad_copies(True)

    @pl.when(i % 2 == 0)
    def _():
        step(0)

    @pl.when(i % 2 == 1)
    def _():
        step(1)


def _dispatch(zrow, zcnt, n_used, pos, cnt_t, off_t, base_t, xn2, n_slots, tt, blk):
    n_tiles = cnt_t.shape[0]
    smem = lambda w: pl.BlockSpec((1, 1, w), lambda i, zr, zc, nu: (i, 0, 0), memory_space=pltpu.SMEM)
    rows = TOP_K * tt * TOKEN_ROWS
    grid_spec = pltpu.PrefetchScalarGridSpec(
        num_scalar_prefetch=3,
        grid=(n_tiles,),
        in_specs=[smem(TOP_K * tt), smem(LANES), smem(LANES), smem(LANES),
                  pl.BlockSpec((tt * TOKEN_ROWS, LANES), lambda i, zr, zc, nu: (i, 0))],
        out_specs=pl.BlockSpec(memory_space=pl.ANY),
        scratch_shapes=[pltpu.VMEM((rows, LANES), F32), pltpu.VMEM((rows, LANES), F32),
                        pltpu.VMEM((blk * TOKEN_ROWS, LANES), F32),
                        pltpu.SemaphoreType.DMA((2,)), pltpu.SemaphoreType.DMA((1,))],
    )
    return pl.pallas_call(
        functools.partial(_dispatch_kernel, tt=tt, blk=blk),
        grid_spec=grid_spec,
        out_shape=jax.ShapeDtypeStruct((n_slots * TOKEN_ROWS, LANES), F32),
        compiler_params=_cparams(("arbitrary",)),
        name="moe_dispatch",
    )(zrow, zcnt, n_used, pos, cnt_t, off_t, base_t, xn2)


def _expert_kernel(be_ref, first_ref, nu_ref, x_ref, wg_ref, bg_ref, wu_ref, bu_ref, wd_ref, bd_ref, y_ref, wbf,
                   *, blk):
    b = pl.program_id(0)

    @pl.when((first_ref[b] == 1) & (b < nu_ref[0]))
    def _():
        wbf[0] = wg_ref[0].astype(BF16)
        wbf[1] = wu_ref[0].astype(BF16)
        wbf[2] = wd_ref[0].astype(BF16)

    @pl.when(b < nu_ref[0])
    def _():
        xb = _load_token_tiles(x_ref, 0, blk).astype(BF16)
        gt = jnp.minimum(jnp.dot(xb, wbf[0], preferred_element_type=F32) + bg_ref[0], SWIGLU_LIMIT)
        up = jnp.clip(jnp.dot(xb, wbf[1], preferred_element_type=F32) + bu_ref[0], -SWIGLU_LIMIT, SWIGLU_LIMIT)
        hid = (up + 1.0) * gt * _sigmoid(SWIGLU_ALPHA * gt)
        _store_token_tiles(y_ref, 0, jnp.dot(hid.astype(BF16), wbf[2], preferred_element_type=F32) + bd_ref[0])

    @pl.when(b >= nu_ref[0])
    def _():
        y_ref[...] = jnp.zeros_like(y_ref)


def _experts(block_expert, first, n_used, xs, p, blk):
    n_blocks = block_expert.shape[0]
    used = lambda b, nu: jnp.minimum(b, nu[0] - 1)
    wspec = lambda: pl.BlockSpec((1, D_MODEL, D_MODEL), lambda b, be, fi, nu: (be[used(b, nu)], 0, 0))
    bspec = lambda: pl.BlockSpec((1, 1, D_MODEL), lambda b, be, fi, nu: (be[used(b, nu)], 0, 0))
    rspec = lambda: pl.BlockSpec((blk * TOKEN_ROWS, LANES), lambda b, be, fi, nu: (used(b, nu), 0))
    grid_spec = pltpu.PrefetchScalarGridSpec(
        num_scalar_prefetch=3,
        grid=(n_blocks,),
        in_specs=[rspec(), wspec(), bspec(), wspec(), bspec(), wspec(), bspec()],
        out_specs=pl.BlockSpec((blk * TOKEN_ROWS, LANES), lambda b, be, fi, nu: (b, 0)),
        scratch_shapes=[pltpu.VMEM((3, D_MODEL, D_MODEL), BF16)],
    )
    return pl.pallas_call(
        functools.partial(_expert_kernel, blk=blk),
        grid_spec=grid_spec,
        out_shape=jax.ShapeDtypeStruct(xs.shape, F32),
        compiler_params=_cparams(("arbitrary",)),
        name="experts",
    )(block_expert, first, n_used, xs, p["exp_w_gate"], p["exp_b_gate"], p["exp_w_up"], p["exp_b_up"],
      p["exp_w_down"], p["exp_b_down"])


def _combine_kernel(pos_ref, pw_ref, cnt_ref, off_ref, base_ref, cntn_ref, offn_ref, basen_ref,
                    ys_hbm, x1_ref, nrm_ref, op_ref, os_ref, ybuf0, ybuf1, mbuf, sems, *, tt, n_p):
    i = pl.program_id(0)
    n = pl.num_programs(0)
    ybufs = (ybuf0, ybuf1)

    @pl.when(i == 0)
    def _():
        _run_copies(cnt_ref, off_ref, base_ref, ybuf0, ys_hbm, sems.at[0], False)

    def step(cur):
        @pl.when(i + 1 < n)
        def _():
            _run_copies(cntn_ref, offn_ref, basen_ref, ybufs[1 - cur], ys_hbm, sems.at[1 - cur], False)

        _wait_run_copies(ybufs[cur], ys_hbm, sems.at[cur], False)
        ybuf = ybufs[cur]
        for t in range(tt):
            acc = None
            for kk in range(TOP_K):
                j = kk * tt + t
                row = ybuf[pl.ds(pl.multiple_of(pos_ref[0, 0, j], TOKEN_ROWS), TOKEN_ROWS), :]
                term = pw_ref[0, 0, j] * row
                acc = term if acc is None else acc + term
            mbuf[t * TOKEN_ROWS:(t + 1) * TOKEN_ROWS, :] = acc

    @pl.when(i % 2 == 0)
    def _():
        step(0)

    @pl.when(i % 2 == 1)
    def _():
        step(1)

    res = _rms(x1_ref[...] + _load_token_tiles(mbuf, 0, tt), nrm_ref[...])

    @pl.when(i < n_p)
    def _():
        op_ref[...] = res

    @pl.when(i >= n_p)
    def _():
        os_ref[...] = res


def _combine(pos, pw, cnt_t, off_t, base_t, ys, x1, final_norm, n_prompt, tt):
    n = x1.shape[0]
    n_p = n_prompt // tt
    n_s = (n - n_prompt) // tt
    nt = n_p + n_s
    smem = lambda w: pl.BlockSpec((1, 1, w), lambda i: (i, 0, 0), memory_space=pltpu.SMEM)
    smem_next = lambda w: pl.BlockSpec((1, 1, w), lambda i: (jnp.minimum(i + 1, nt - 1), 0, 0),
                                       memory_space=pltpu.SMEM)
    rows = TOP_K * tt * TOKEN_ROWS
    return pl.pallas_call(
        functools.partial(_combine_kernel, tt=tt, n_p=n_p),
        grid=(nt,),
        in_specs=[smem(TOP_K * tt), smem(TOP_K * tt), smem(LANES), smem(LANES), smem(LANES),
                  smem_next(LANES), smem_next(LANES), smem_next(LANES),
                  pl.BlockSpec(memory_space=pl.ANY),
                  pl.BlockSpec((tt, D_MODEL), lambda i: (i, 0)),
                  pl.BlockSpec((1, D_MODEL), lambda i: (0, 0))],
        out_specs=[pl.BlockSpec((tt, D_MODEL), lambda i: (jnp.minimum(i, n_p - 1), 0)),
                   pl.BlockSpec((tt, D_MODEL), lambda i: (jnp.maximum(i - n_p, 0), 0))],
        out_shape=[jax.ShapeDtypeStruct((n_prompt, D_MODEL), F32),
                   jax.ShapeDtypeStruct((n - n_prompt, D_MODEL), F32)],
        scratch_shapes=[pltpu.VMEM((rows, LANES), F32), pltpu.VMEM((rows, LANES), F32),
                        pltpu.VMEM((tt * TOKEN_ROWS, LANES), F32), pltpu.SemaphoreType.DMA((2,))],
        compiler_params=_cparams(("arbitrary",)),
        name="moe_combine",
    )(pos, pw, cnt_t, off_t, base_t, cnt_t, off_t, base_t, ys, x1, final_norm)


def _pad_rows(a, rows):
    return jnp.pad(a, ((0, 0), (rows - a.shape[1], 0), (0, 0)))


def _tile(n, pref):
    t = pref
    while n % t:
        t //= 2
    return t


def _mixer_group(x2d, n_streams, t_len, lru_conv0, lru_h0, gdn_conv0, gdn_s0, shared_init, reset_first, chunk, p):
    tt = _tile(t_len, TOKEN_TILE)
    lru_out, u, w, qd, kd, at, z, gl, lru_conv1, lru_h1, gdn_conv1 = _pre(
        x2d, lru_conv0, lru_h0, gdn_conv0, shared_init, n_streams, t_len, tt, chunk, reset_first, p)
    sb = _tile(n_streams, REC_STREAMS)
    gdn_out, gdn_s1 = _rec(u, w, qd, kd, at, z, gl, gdn_s0, shared_init, n_streams, t_len, sb, chunk, p)
    return lru_out, gdn_out, (lru_conv1, lru_h1, gdn_conv1, gdn_s1)


def _routing(cnt_tiles, n_tok, blk):
    cnt_t = cnt_tiles[:, 0, :]
    counts = jnp.sum(cnt_t, axis=0)
    padded = (counts + blk - 1) // blk * blk
    pends = jnp.cumsum(padded)
    pstarts = pends - padded
    off_t = jnp.cumsum(cnt_t, axis=1) - cnt_t
    base_t = pstarts[None, :] + jnp.cumsum(cnt_t, axis=0) - cnt_t
    n_blocks = (n_tok * TOP_K + N_EXPERTS * (blk - 1) + blk - 1) // blk
    starts = jnp.arange(n_blocks, dtype=jnp.int32) * blk
    block_expert = jnp.minimum(jnp.sum((pends[None, :N_EXPERTS] <= starts[:, None]).astype(jnp.int32), axis=1),
                               N_EXPERTS - 1)
    first = jnp.concatenate([jnp.ones((1,), jnp.int32),
                             (block_expert[1:] != block_expert[:-1]).astype(jnp.int32)])
    n_used = (pends[N_EXPERTS - 1] // blk).astype(jnp.int32)[None]
    r3 = lambda a: (a * TOKEN_ROWS).astype(jnp.int32)[:, None, :]
    zrow = ((pstarts + counts) * TOKEN_ROWS).astype(jnp.int32)[:N_EXPERTS]
    zcnt = (padded - counts).astype(jnp.int32)[:N_EXPERTS]
    return (cnt_t.astype(jnp.int32)[:, None, :], r3(off_t), r3(base_t), block_expert, first, n_used, zrow, zcnt,
            n_blocks * blk)


def kernel(x_prompt, x_sample, state_lru_conv, state_lru_h, state_gdn_conv, state_gdn_S, meta_tokens, norm_mix, w_in, lru_conv_w, lru_conv_b, lru_w_a, lru_b_a, lru_w_x, lru_b_x, lru_lambda, lru_norm, gdn_conv_w, gdn_A_log, gdn_dt_bias, gdn_norm, w_out, norm_ffn, router_w, router_b, exp_w_gate, exp_b_gate, exp_w_up, exp_b_up, exp_w_down, exp_b_down, final_norm):
    bp, tp, _ = x_prompt.shape
    bs, ts, _ = x_sample.shape
    n_meta = meta_tokens.shape[0]
    n_prompt, n_sample = bp * tp, bs * ts
    tt = MOE_TILE
    assert n_prompt % tt == 0 and n_sample % tt == 0 and EXPERT_ROWS <= 2 * MOE_TILE

    def blockdiag(w):
        eye = jnp.eye(LRU_BLOCKS, dtype=w.dtype)
        return jnp.einsum("ncd,nm->ncmd", w, eye).reshape(D_LRU, D_LRU)

    lane_pad = lambda v: jnp.pad(v, (0, LANES - v.shape[0]))[None]
    p = dict(
        norm_mix=norm_mix[0][None],
        w_in=jnp.pad(w_in[0], ((0, 0), (0, D_IN_PAD - D_IN))).astype(BF16),
        lru_conv_w=lru_conv_w[0], lru_conv_b=lru_conv_b[0][None],
        lru_wa_bd=blockdiag(lru_w_a[0]).astype(BF16), lru_b_a=lru_b_a[0][None],
        lru_wx_bd=blockdiag(lru_w_x[0]).astype(BF16), lru_b_x=lru_b_x[0][None],
        lru_lambda=lru_lambda[0][None], lru_norm=lru_norm[0][None],
        gdn_conv_w=gdn_conv_w[0],
        gdn_par=jnp.concatenate([lane_pad(gdn_A_log[0]), lane_pad(gdn_dt_bias[0])], axis=0),
        gdn_norm=gdn_norm[0][None],
        w_out1=w_out[0][:D_LRU].astype(BF16), w_out2=w_out[0][D_LRU:].astype(BF16),
        norm_ffn=norm_ffn[0][None],
        router_w=router_w[0].T.astype(BF16),
        router_b=jnp.broadcast_to(router_b[0][:, None], (N_EXPERTS, LANES)),
        exp_w_gate=exp_w_gate[0], exp_b_gate=exp_b_gate[0][:, None, :],
        exp_w_up=exp_w_up[0], exp_b_up=exp_b_up[0][:, None, :],
        exp_w_down=exp_w_down[0], exp_b_down=exp_b_down[0][:, None, :],
    )

    zc = lambda c: jnp.zeros((1, SUBLANES, c), F32)
    _, _, m_state = _mixer_group(meta_tokens, 1, n_meta, zc(D_LRU), jnp.zeros((1, 1, D_LRU), F32), zc(GDN_QKV),
                                 jnp.zeros((1, GDN_HEADS, GDN_DK, GDN_DV), F32), False, True, n_meta, p)
    xp2 = x_prompt.reshape(n_prompt, D_MODEL)
    xs2 = x_sample.reshape(n_sample, D_MODEL)
    p_lru, p_gdn, p_state = _mixer_group(xp2, bp, tp, m_state[0], m_state[1], m_state[2], m_state[3], True, False,
                                         GDN_CHUNK, p)
    s_lru, s_gdn, s_state = _mixer_group(xs2, bs, ts, _pad_rows(state_lru_conv[0], SUBLANES),
                                         state_lru_h[0][:, None, :], _pad_rows(state_gdn_conv[0], SUBLANES),
                                         state_gdn_S[0], False, False, GDN_CHUNK, p)

    x1, xn2, route, pw, cnt_tiles = _post(p_lru, p_gdn, xp2, s_lru, s_gdn, xs2, tt, p)
    n_tok = n_prompt + n_sample
    cnt_t, off_t, base_t, block_expert, first, n_used, zrow, zcnt, n_slots = _routing(cnt_tiles, n_tok, EXPERT_ROWS)
    n_tiles = n_tok // tt
    pos = (route.reshape(n_tiles, SUBLANES, tt)[:, TOP_K:2 * TOP_K] * TOKEN_ROWS).reshape(n_tiles, 1, TOP_K * tt)
    pw_s = pw.reshape(n_tiles, SUBLANES, tt)[:, :TOP_K].reshape(n_tiles, 1, TOP_K * tt)
    xs = _dispatch(zrow, zcnt, n_used, pos, cnt_t, off_t, base_t, xn2, n_slots, tt, EXPERT_ROWS)
    ys = _experts(block_expert, first, n_used, xs, p, EXPERT_ROWS)
    y_p, y_s = _combine(pos, pw_s, cnt_t, off_t, base_t, ys, x1, final_norm[None], n_prompt, tt)

    def states(st, b):
        return (st[0][:, SUBLANES - 3:, :][None], st[1].reshape(1, b, D_LRU), st[2][:, SUBLANES - 3:, :][None],
                st[3][None])

    return (y_p.reshape(bp, tp, D_MODEL), y_s.reshape(bs, ts, D_MODEL)) + states(p_state, bp) + states(s_state, bs)
```

```python
import functools

import jax
import jax.numpy as jnp
from jax import lax
from jax.experimental import pallas as pl
from jax.experimental.pallas import tpu as pltpu

F32 = jnp.float32
BF16 = jnp.bfloat16

D_MODEL = 1024
D_LRU = 512
LRU_BLOCKS = 8
LRU_C = 8.0
CONV_W = 4
GDN_HEADS = 4
GDN_DK = 128
GDN_DV = 128
GDN_QKV = GDN_HEADS * (2 * GDN_DK + GDN_DV)
D_IN = 2 * D_LRU + GDN_QKV + GDN_HEADS * GDN_DV + 2 * GDN_HEADS
N_EXPERTS = 32
TOP_K = 4
SWIGLU_LIMIT = 7.0
SWIGLU_ALPHA = 1.702
EPS = 1e-6

LANES = 128
SUBLANES = 8
D_IN_PAD = 3200
COL_AB = (2 * D_LRU + GDN_QKV + GDN_HEADS * GDN_DV) // LANES
VMEM_LIMIT = 56 * 1024 * 1024

TOKEN_TILE = 256
MOE_TILE = 512
EXPERT_ROWS = 512
GDN_CHUNK = 64
REC_STREAMS = 8
PRE_CHAINS = 8
PROJ_PIECE = 256
PROJ_EARLY_POINTS = 8
TOKEN_ROWS = D_MODEL // LANES
RUN_BITS = tuple(1 << i for i in range(MOE_TILE.bit_length() - 1, -1, -1))


def _cparams(sem):
    return pltpu.CompilerParams(dimension_semantics=sem, vmem_limit_bytes=VMEM_LIMIT)


def _rms(x, gain):
    return x * lax.rsqrt(jnp.mean(x * x, axis=-1, keepdims=True) + EPS) * gain


def _softplus(x):
    return jnp.maximum(x, 0.0) + jnp.log1p(jnp.exp(-jnp.abs(x)))


def _sigmoid(x):
    return 1.0 / (1.0 + jnp.exp(-x))


def _load_token_tiles(ref, tok0, n_tok):
    return jnp.concatenate(
        [ref[pl.ds(tok0 * TOKEN_ROWS + j, n_tok, stride=TOKEN_ROWS), :] for j in range(TOKEN_ROWS)], axis=1)


def _store_token_tiles(ref, tok0, val):
    for j in range(TOKEN_ROWS):
        ref[pl.ds(tok0 * TOKEN_ROWS + j, val.shape[0], stride=TOKEN_ROWS), :] = val[:, j * LANES:(j + 1) * LANES]


def _mm(a, b):
    return jnp.dot(a.astype(BF16), b.astype(BF16), preferred_element_type=F32)


def _mm_nt(a, b):
    return lax.dot_general(a.astype(BF16), b.astype(BF16), (((1,), (1,)), ((), ())), preferred_element_type=F32)


def _conv_step(buf, x, w_ref, tt):
    buf[SUBLANES:SUBLANES + tt, :] = x
    y = x * w_ref[CONV_W - 1:CONV_W, :]
    for j in range(CONV_W - 2, -1, -1):
        y = y + buf[SUBLANES - 3 + j:SUBLANES - 3 + j + tt, :] * w_ref[j:j + 1, :]
    tail = buf[tt:tt + SUBLANES, :]
    buf[0:SUBLANES, :] = tail
    return y


def _pre_kernel(x_ref, xnext_ref, nrm_ref, win_ref, lconv0_ref, h0_ref, gconv0_ref,
                lcw_ref, lcb_ref, wa_ref, ba_ref, wx_ref, bx_ref, lam_ref, lnrm_ref, gcw_ref, par_ref,
                lru_ref, u_ref, w_ref, qd_ref, kd_ref, at_ref, z_ref, gl_ref, lconv1_ref, h1_ref, gconv1_ref,
                lbuf, a_buf, b_buf, hcar, gbuf, pbuf, *, tt, c, reset_first):
    t = pl.program_id(1)
    hd = GDN_HEADS * GDN_DK

    @pl.when(t == 0)
    def _():
        lbuf[0:SUBLANES, :] = lconv0_ref[0]
        hcar[...] = h0_ref[0]
        for j in range(3):
            gbuf[j, 0:SUBLANES, :] = gconv0_ref[0, :, j * hd:(j + 1) * hd]

    @pl.when((pl.program_id(0) == 0) & (t == 0))
    def _():
        pbuf[...] = jnp.dot(_rms(x_ref[...], nrm_ref[...]).astype(BF16), win_ref[...], preferred_element_type=F32)

    th = tt
    projs = [pbuf]
    xnext = _rms(xnext_ref[...], nrm_ref[...]).astype(BF16)
    pieces = [(lo, min(lo + PROJ_PIECE, D_IN_PAD)) for lo in range(0, D_IN_PAD, PROJ_PIECE)]
    z_col = 2 * D_LRU + GDN_QKV
    assert z_col % PROJ_PIECE == 0
    pieces = [pc for pc in pieces if pc[0] >= z_col] + [pc for pc in pieces if pc[0] < z_col]
    n_levels = max(c.bit_length() - 1, 1)
    n_solve_points = -(-(tt // c) * GDN_HEADS // PRE_CHAINS) * n_levels
    n_late = max(len(pieces) - PROJ_EARLY_POINTS, 0)
    plan = [1] * PROJ_EARLY_POINTS + [(i + 1) * n_late // n_solve_points - i * n_late // n_solve_points
                                      for i in range(n_solve_points)]
    points_done = [0]
    pieces_done = [0]

    def next_proj_pieces():
        k = plan[points_done[0]]
        points_done[0] += 1
        for lo, hi in pieces[pieces_done[0]:pieces_done[0] + k]:
            pbuf[:, lo:hi] = jnp.dot(xnext, win_ref[:, lo:hi], preferred_element_type=F32)
        pieces_done[0] += k

    rowc = lax.broadcasted_iota(jnp.int32, (c, LANES), 0)
    ri = lax.broadcasted_iota(jnp.int32, (c, c), 0)
    ci = lax.broadcasted_iota(jnp.int32, (c, c), 1)
    causal = ri >= ci
    strict = ri > ci
    eye = ri == ci
    lane_pad = jnp.zeros((c, LANES - c), F32)
    row8 = lax.broadcasted_iota(jnp.int32, (SUBLANES, D_LRU), 0)

    def group(gi, h):
        g0 = pl.multiple_of(gi * SUBLANES, SUBLANES)
        a8 = a_buf[pl.ds(g0, SUBLANES), :]
        b8 = b_buf[pl.ds(g0, SUBLANES), :]
        for d in (1, 2, 4):
            keep = row8 >= d
            b8 = jnp.where(keep, a8 * pltpu.roll(b8, d, 0) + b8, b8)
            a8 = jnp.where(keep, a8 * pltpu.roll(a8, d, 0), a8)
        h8 = a8 * h + b8
        b_buf[pl.ds(g0, SUBLANES), :] = h8
        return h8[SUBLANES - 1:SUBLANES, :]

    h_last = hcar[...]
    for part, proj in enumerate(projs):
        r0 = part * th

        z_ref[r0:r0 + th, :] = proj[:, 2 * D_LRU + GDN_QKV:2 * D_LRU + GDN_QKV + hd]
        ab = proj[:, COL_AB * LANES:(COL_AB + 1) * LANES]
        g_all = -jnp.exp(par_ref[0:1, :]) * _softplus(ab + par_ref[1:2, :])
        beta_all = _sigmoid(ab)
        next_proj_pieces()

        xc = _conv_step(lbuf, proj[:, 0:D_LRU], lcw_ref, th) + lcb_ref[...]
        next_proj_pieces()
        xb = xc.astype(BF16)
        r = _sigmoid(jnp.dot(xb, wa_ref[...], preferred_element_type=F32) + ba_ref[...])
        i = _sigmoid(jnp.dot(xb, wx_ref[...], preferred_element_type=F32) + bx_ref[...])
        next_proj_pieces()
        log_a = (-LRU_C) * r * _softplus(-lam_ref[...])
        a = jnp.exp(log_a)
        mult = jnp.sqrt(-jnp.tanh(log_a) * (a * a + 1.0))
        if reset_first and part == 0:
            row = lax.broadcasted_iota(jnp.int32, (th, D_LRU), 0)
            mult = jnp.where((row == 0) & (t == 0), 1.0, mult)
        a_buf[0:th, :] = a
        b_buf[0:th, :] = mult * i * xc
        next_proj_pieces()
        h_last = lax.fori_loop(0, th // SUBLANES, group, h_last)
        lru_ref[r0:r0 + th, :] = _rms(b_buf[0:th, :] * jax.nn.gelu(proj[:, D_LRU:2 * D_LRU]),
                                      lnrm_ref[...]).astype(lru_ref.dtype)
        next_proj_pieces()

        def conv_silu(j, proj=proj):
            y = _conv_step(gbuf.at[j], proj[:, 2 * D_LRU + j * hd:2 * D_LRU + (j + 1) * hd],
                           gcw_ref.at[:, j * hd:(j + 1) * hd], th)
            return y * _sigmoid(y)

        qa = conv_silu(0)
        next_proj_pieces()
        ka = conv_silu(1)
        next_proj_pieces()
        va = conv_silu(2)
        next_proj_pieces()

        pairs = [(j, h) for j in range(th // c) for h in range(GDN_HEADS)]
        for p0 in range(0, len(pairs), PRE_CHAINS):
            group_pairs = pairs[p0:p0 + PRE_CHAINS]
            gcs = {}
            for j in sorted({j for j, _ in group_pairs}):
                gc = g_all[j * c:(j + 1) * c]
                d = 1
                while d < c:
                    gc = gc + jnp.where(rowc >= d, pltpu.roll(gc, d, 0), 0.0)
                    d *= 2
                gcs[j] = gc
            ns, xs = [], []
            for j, h in group_pairs:
                rows = slice(j * c, (j + 1) * c)
                orow = slice(r0 + j * c, r0 + (j + 1) * c)
                sl = slice(h * GDN_DK, (h + 1) * GDN_DK)
                gc = gcs[j]
                q = qa[rows, sl]
                k = ka[rows, sl]
                q = q * lax.rsqrt(jnp.sum(q * q, axis=-1, keepdims=True) + EPS) * (GDN_DK ** -0.5)
                k = k * lax.rsqrt(jnp.sum(k * k, axis=-1, keepdims=True) + EPS)
                gcol = gc[:, h:h + 1]
                bcol = beta_all[rows, GDN_HEADS + h:GDN_HEADS + h + 1]
                grow = jnp.sum(jnp.where(eye, gcol, 0.0), axis=0, keepdims=True)
                decay = jnp.where(causal, jnp.exp(jnp.where(causal, gcol - grow, 0.0)), 0.0)
                egc = jnp.exp(gcol)
                g_last = gc[c - 1:c, h:h + 1]
                kb = k * bcol
                qd_ref[orow, sl] = (q * egc).astype(qd_ref.dtype)
                kd_ref[orow, sl] = (k * jnp.exp(g_last - gcol)).astype(kd_ref.dtype)
                gl_ref[r0 // c + j, :, sl] = jnp.broadcast_to(jnp.exp(g_last), (1, GDN_DV))
                at_ref[orow, sl] = jnp.concatenate([_mm_nt(q, k) * decay, lane_pad], axis=1).astype(at_ref.dtype)
                ns.append(jnp.where(strict, _mm_nt(kb, k) * decay, 0.0))
                xs.append(jnp.concatenate([va[rows, sl] * bcol, kb * egc], axis=1))
            xs = [x - _mm(n, x) for n, x in zip(ns, xs)]
            next_proj_pieces()
            m = 2
            while m < c:
                ns = [_mm(n, n) for n in ns]
                xs = [x + _mm(n, x) for n, x in zip(ns, xs)]
                next_proj_pieces()
                m *= 2
            for (j, h), x in zip(group_pairs, xs):
                orow = slice(r0 + j * c, r0 + (j + 1) * c)
                sl = slice(h * GDN_DK, (h + 1) * GDN_DK)
                u_ref[orow, sl] = x[:, :GDN_DV]
                w_ref[orow, sl] = x[:, GDN_DV:].astype(w_ref.dtype)
    hcar[...] = h_last
    assert points_done[0] == len(plan) and pieces_done[0] == len(pieces)

    @pl.when(t == pl.num_programs(1) - 1)
    def _():
        lconv1_ref[0] = lbuf[0:SUBLANES, :]
        h1_ref[0] = h_last
        for j in range(3):
            gconv1_ref[0, :, j * hd:(j + 1) * hd] = gbuf[j, 0:SUBLANES, :]


def _pre(x2d, lconv0, h0, gconv0, shared_init, n_streams, t_len, tt, c, reset_first, p):
    nt = t_len // tt
    hd = GDN_HEADS * GDN_DK
    n = n_streams * t_len
    st = (lambda s, t: (0, 0, 0)) if shared_init else (lambda s, t: (s, 0, 0))
    full = lambda a, b: pl.BlockSpec((a, b), lambda s, t: (0, 0))
    row = lambda w: pl.BlockSpec((tt, w), lambda s, t: (s * nt + t, 0))
    return pl.pallas_call(
        functools.partial(_pre_kernel, tt=tt, c=c, reset_first=reset_first),
        grid=(n_streams, nt),
        in_specs=[row(D_MODEL),
                  pl.BlockSpec((tt, D_MODEL), lambda s, t: (jnp.minimum(s * nt + t + 1, n_streams * nt - 1), 0)),
                  full(1, D_MODEL), full(D_MODEL, D_IN_PAD),
                  pl.BlockSpec((1, SUBLANES, D_LRU), st), pl.BlockSpec((1, 1, D_LRU), st),
                  pl.BlockSpec((1, SUBLANES, GDN_QKV), st),
                  full(CONV_W, D_LRU), full(1, D_LRU), full(D_LRU, D_LRU), full(1, D_LRU), full(D_LRU, D_LRU),
                  full(1, D_LRU), full(1, D_LRU), full(1, D_LRU), full(CONV_W, GDN_QKV), full(2, LANES)],
        out_specs=[row(D_LRU), row(hd), row(hd), row(hd), row(hd), row(hd), row(hd),
                   pl.BlockSpec((tt // c, 1, hd), lambda s, t: (s * nt + t, 0, 0)),
                   pl.BlockSpec((1, SUBLANES, D_LRU), lambda s, t: (s, 0, 0)),
                   pl.BlockSpec((1, 1, D_LRU), lambda s, t: (s, 0, 0)),
                   pl.BlockSpec((1, SUBLANES, GDN_QKV), lambda s, t: (s, 0, 0))],
        out_shape=[jax.ShapeDtypeStruct((n, D_LRU), BF16),
                   jax.ShapeDtypeStruct((n, hd), F32),
                   jax.ShapeDtypeStruct((n, hd), BF16),
                   jax.ShapeDtypeStruct((n, hd), BF16),
                   jax.ShapeDtypeStruct((n, hd), BF16),
                   jax.ShapeDtypeStruct((n, hd), BF16),
                   jax.ShapeDtypeStruct((n, hd), F32),
                   jax.ShapeDtypeStruct((n // c, 1, hd), F32),
                   jax.ShapeDtypeStruct((n_streams, SUBLANES, D_LRU), F32),
                   jax.ShapeDtypeStruct((n_streams, 1, D_LRU), F32),
                   jax.ShapeDtypeStruct((n_streams, SUBLANES, GDN_QKV), F32)],
        scratch_shapes=[pltpu.VMEM((tt + SUBLANES, D_LRU), F32),
                        pltpu.VMEM((tt, D_LRU), F32),
                        pltpu.VMEM((tt, D_LRU), F32),
                        pltpu.VMEM((1, D_LRU), F32),
                        pltpu.VMEM((3, tt + SUBLANES, hd), F32),
                        pltpu.VMEM((tt, D_IN_PAD), F32)],
        compiler_params=_cparams(("arbitrary", "arbitrary")),
        name="mixer_pre",
    )(x2d, x2d, p["norm_mix"], p["w_in"], lconv0, h0, gconv0, p["lru_conv_w"], p["lru_conv_b"], p["lru_wa_bd"],
      p["lru_b_a"], p["lru_wx_bd"], p["lru_b_x"], p["lru_lambda"], p["lru_norm"], p["gdn_conv_w"], p["gdn_par"])


def _rec_kernel(u_ref, w_ref, qd_ref, kd_ref, at_ref, z_ref, gl_ref, s0_ref, nrm_ref, o_ref, s1_ref, s_scr,
                *, sb, c, shared_init):
    t = pl.program_id(1)

    @pl.when(t == 0)
    def _():
        for s in range(sb):
            s_scr[s] = s0_ref[0 if shared_init else s]

    chains = [(s, h, slice(h * GDN_DK, (h + 1) * GDN_DK)) for s in range(sb) for h in range(GDN_HEADS)]
    m1s = [jnp.dot(jnp.concatenate([w_ref[s, :, sl], qd_ref[s, :, sl]], axis=0), s_scr[s, h].astype(BF16),
                   preferred_element_type=F32) for s, h, sl in chains]
    vns = [(u_ref[s, :, sl] - m1[:c]).astype(BF16) for (s, h, sl), m1 in zip(chains, m1s)]
    for (s, h, sl), v_new in zip(chains, vns):
        s_scr[s, h] = s_scr[s, h] * gl_ref[s, 0, :, sl] + lax.dot_general(
            kd_ref[s, :, sl], v_new, (((0,), (0,)), ((), ())), preferred_element_type=F32)
    for (s, h, sl), m1, v_new in zip(chains, m1s, vns):
        o = m1[c:] + jnp.dot(at_ref[s, :, h * GDN_DK:h * GDN_DK + c], v_new, preferred_element_type=F32)
        z = z_ref[s, :, sl]
        o_ref[s, :, sl] = (_rms(o, nrm_ref[...]) * (z * _sigmoid(z))).astype(o_ref.dtype)

    @pl.when(t == pl.num_programs(1) - 1)
    def _():
        s1_ref[...] = s_scr[...]


def _rec(u, w, qd, kd, at, z, gl, s0, shared_init, n_streams, t_len, sb, c, p):
    hd = GDN_HEADS * GDN_DK
    nt = t_len // c
    v3 = lambda a: a.reshape(n_streams, t_len, hd)
    blk3 = lambda: pl.BlockSpec((sb, c, hd), lambda g, t: (g, t, 0))
    s_spec = (pl.BlockSpec((1, GDN_HEADS, GDN_DK, GDN_DV), lambda g, t: (0, 0, 0, 0)) if shared_init else
              pl.BlockSpec((sb, GDN_HEADS, GDN_DK, GDN_DV), lambda g, t: (g, 0, 0, 0)))
    out, s1 = pl.pallas_call(
        functools.partial(_rec_kernel, sb=sb, c=c, shared_init=shared_init),
        grid=(n_streams // sb, nt),
        in_specs=[blk3(), blk3(), blk3(), blk3(), blk3(), blk3(),
                  pl.BlockSpec((sb, 1, 1, hd), lambda g, t: (g, t, 0, 0)),
                  s_spec,
                  pl.BlockSpec((1, GDN_DV), lambda g, t: (0, 0))],
        out_specs=[blk3(), pl.BlockSpec((sb, GDN_HEADS, GDN_DK, GDN_DV), lambda g, t: (g, 0, 0, 0))],
        out_shape=[jax.ShapeDtypeStruct((n_streams, t_len, hd), BF16),
                   jax.ShapeDtypeStruct((n_streams, GDN_HEADS, GDN_DK, GDN_DV), F32)],
        scratch_shapes=[pltpu.VMEM((sb, GDN_HEADS, GDN_DK, GDN_DV), F32)],
        compiler_params=_cparams(("arbitrary", "arbitrary")),
        name="gdn_state",
    )(v3(u), v3(w), v3(qd), v3(kd), v3(at), v3(z), gl.reshape(n_streams, nt, 1, hd), s0, p["gdn_norm"])
    return out.reshape(n_streams * t_len, hd), s1


def _post_kernel(lru_p, gdn_p, x_p, lru_s, gdn_s, x_s, wo1_ref, wo2_ref, nrm_ref, rwt_ref, rb_ref,
                 x1_ref, xn2_ref, route_ref, pw_ref, cnt_ref, *, tt, n_p):
    i = pl.program_id(0)
    is_p = i < n_p
    lru = jnp.where(is_p, lru_p[...], lru_s[...])
    gdn = jnp.where(is_p, gdn_p[...], gdn_s[...])
    x = jnp.where(is_p, x_p[...], x_s[...])
    m = (jnp.dot(lru, wo1_ref[...], preferred_element_type=F32)
         + jnp.dot(gdn, wo2_ref[...], preferred_element_type=F32))
    x1 = x + m
    x1_ref[...] = x1
    xn2 = _rms(x1, nrm_ref[...])
    _store_token_tiles(xn2_ref, 0, xn2)
    logits = lax.dot_general(rwt_ref[...], xn2.astype(BF16), (((1,), (1,)), ((), ())),
                             preferred_element_type=F32) + rb_ref[:, 0:1]
    e_id = lax.broadcasted_iota(jnp.int32, (N_EXPERTS, tt), 0)
    vals, idxs = [], []
    for _ in range(TOP_K):
        mx = jnp.max(logits, axis=0, keepdims=True)
        ix = jnp.min(jnp.where(logits == mx, e_id, N_EXPERTS), axis=0, keepdims=True)
        logits = jnp.where(e_id == ix, -jnp.inf, logits)
        vals.append(mx)
        idxs.append(ix)
    es = [jnp.exp(v - vals[0]) for v in vals]
    den = es[0] + es[1] + es[2] + es[3]
    onehot = jnp.zeros((N_EXPERTS, tt), F32)
    for ix in idxs:
        onehot = onehot + (e_id == ix).astype(F32)
    ri = lax.broadcasted_iota(jnp.int32, (tt, tt), 0)
    ci = lax.broadcasted_iota(jnp.int32, (tt, tt), 1)
    earlier = jnp.dot(onehot.astype(BF16), (ri < ci).astype(BF16), preferred_element_type=F32)
    cnt = jnp.sum(onehot, axis=1, keepdims=True)
    ei = lax.broadcasted_iota(jnp.int32, (N_EXPERTS, N_EXPERTS), 0)
    ej = lax.broadcasted_iota(jnp.int32, (N_EXPERTS, N_EXPERTS), 1)
    lower = (ej < ei).astype(BF16)
    cnt_hi = jnp.floor(cnt * (1.0 / 16.0))
    excl = lambda v: jnp.dot(lower, jnp.broadcast_to(v, (N_EXPERTS, LANES)).astype(BF16),
                             preferred_element_type=F32)[:, 0:1]
    off = 16.0 * excl(cnt_hi) + excl(cnt - 16.0 * cnt_hi)
    place = earlier + off
    r8 = lax.broadcasted_iota(jnp.int32, (SUBLANES, tt), 0)
    route = jnp.zeros((SUBLANES, tt), jnp.int32)
    pw = jnp.zeros((SUBLANES, tt), F32)
    for kk in range(TOP_K):
        rank = jnp.sum(jnp.where(e_id == idxs[kk], place, 0.0), axis=0, keepdims=True).astype(jnp.int32)
        route = jnp.where(r8 == kk, idxs[kk], route)
        route = jnp.where(r8 == TOP_K + kk, rank, route)
        pw = jnp.where(r8 == kk, es[kk] / den, pw)
    route_ref[...] = route
    pw_ref[...] = pw
    le = lax.broadcasted_iota(jnp.int32, (N_EXPERTS, LANES), 0)
    ll = lax.broadcasted_iota(jnp.int32, (N_EXPERTS, LANES), 1)
    cnt_ref[0] = jnp.sum(jnp.where(le == ll, cnt, 0.0), axis=0, keepdims=True).astype(jnp.int32)


def _post(lru_p, gdn_p, x_p, lru_s, gdn_s, x_s, tt, p):
    n_p = x_p.shape[0] // tt
    n_s = x_s.shape[0] // tt
    n = x_p.shape[0] + x_s.shape[0]
    prow = lambda w: pl.BlockSpec((tt, w), lambda i: (jnp.minimum(i, n_p - 1), 0))
    srow = lambda w: pl.BlockSpec((tt, w), lambda i: (jnp.maximum(i - n_p, 0), 0))
    row = lambda w: pl.BlockSpec((tt, w), lambda i: (i, 0))
    full = lambda a, b: pl.BlockSpec((a, b), lambda i: (0, 0))
    return pl.pallas_call(
        functools.partial(_post_kernel, tt=tt, n_p=n_p),
        grid=(n_p + n_s,),
        in_specs=[prow(D_LRU), prow(D_LRU), prow(D_MODEL), srow(D_LRU), srow(D_LRU), srow(D_MODEL),
                  full(D_LRU, D_MODEL), full(D_LRU, D_MODEL),
                  full(1, D_MODEL), full(N_EXPERTS, D_MODEL), full(N_EXPERTS, LANES)],
        out_specs=[row(D_MODEL), pl.BlockSpec((tt * TOKEN_ROWS, LANES), lambda i: (i, 0)),
                   pl.BlockSpec((SUBLANES, tt), lambda i: (i, 0)), pl.BlockSpec((SUBLANES, tt), lambda i: (i, 0)),
                   pl.BlockSpec((1, 1, LANES), lambda i: (i, 0, 0))],
        out_shape=[jax.ShapeDtypeStruct((n, D_MODEL), F32), jax.ShapeDtypeStruct((n * TOKEN_ROWS, LANES), F32),
                   jax.ShapeDtypeStruct(((n_p + n_s) * SUBLANES, tt), jnp.int32),
                   jax.ShapeDtypeStruct(((n_p + n_s) * SUBLANES, tt), F32),
                   jax.ShapeDtypeStruct((n_p + n_s, 1, LANES), jnp.int32)],
        compiler_params=_cparams(("arbitrary",)),
        name="outproj_router",
    )(lru_p, gdn_p, x_p, lru_s, gdn_s, x_s, p["w_out1"], p["w_out2"], p["norm_ffn"], p["router_w"], p["router_b"])


def _run_copies(cnt_ref, off_ref, base_ref, vbuf, hbm, sem, to_hbm):
    for e in range(N_EXPERTS):
        cnt = cnt_ref[0, 0, e]
        off = off_ref[0, 0, e]
        base = base_ref[0, 0, e]
        for bit in RUN_BITS:
            done = (cnt & (-2 * bit)) * TOKEN_ROWS

            @pl.when((cnt & bit) != 0)
            def _():
                v = vbuf.at[pl.ds(pl.multiple_of(off + done, TOKEN_ROWS), bit * TOKEN_ROWS), :]
                h = hbm.at[pl.ds(pl.multiple_of(base + done, TOKEN_ROWS), bit * TOKEN_ROWS), :]
                if to_hbm:
                    pltpu.make_async_copy(v, h, sem).start()
                else:
                    pltpu.make_async_copy(h, v, sem).start()


def _wait_run_copies(vbuf, hbm, sem, to_hbm):
    h = hbm.at[pl.ds(0, vbuf.shape[0]), :]
    if to_hbm:
        pltpu.make_async_copy(vbuf, h, sem).wait()
    else:
        pltpu.make_async_copy(h, vbuf, sem).wait()


def _dispatch_kernel(zrow_ref, zcnt_ref, nu_ref, pos_ref, cnt_ref, off_ref, base_ref, x_ref, xs_hbm,
                     dbuf0, dbuf1, zbuf, sems, zsem, *, tt, blk):
    i = pl.program_id(0)
    n = pl.num_programs(0)
    dbufs = (dbuf0, dbuf1)

    def pad_copies(wait):
        for e in range(N_EXPERTS):
            cnt = zcnt_ref[e]
            row = zrow_ref[e]
            for bit in RUN_BITS:
                done = (cnt & (-2 * bit)) * TOKEN_ROWS

                @pl.when((cnt & bit) != 0)
                def _():
                    cp = pltpu.make_async_copy(
                        zbuf.at[pl.ds(0, bit * TOKEN_ROWS), :],
                        xs_hbm.at[pl.ds(pl.multiple_of(row + done, TOKEN_ROWS), bit * TOKEN_ROWS), :], zsem.at[0])
                    if wait:
                        cp.wait()
                    else:
                        cp.start()
        n_blocks = xs_hbm.shape[0] // (blk * TOKEN_ROWS)
        for j in range(N_EXPERTS):
            @pl.when(nu_ref[0] + j < n_blocks)
            def _():
                row = pl.multiple_of((nu_ref[0] + j) * (blk * TOKEN_ROWS), TOKEN_ROWS)
                cp = pltpu.make_async_copy(zbuf, xs_hbm.at[pl.ds(row, blk * TOKEN_ROWS), :], zsem.at[0])
                if wait:
                    cp.wait()
                else:
                    cp.start()

    @pl.when(i == 0)
    def _():
        zbuf[...] = jnp.zeros_like(zbuf)
        pad_copies(False)

    def step(cur):
        dbuf = dbufs[cur]

        @pl.when(i >= 2)
        def _():
            _wait_run_copies(dbuf, xs_hbm, sems.at[cur], True)

        for t in range(tt):
            v = x_ref[t * TOKEN_ROWS:(t + 1) * TOKEN_ROWS, :]
            for kk in range(TOP_K):
                dbuf[pl.ds(pl.multiple_of(pos_ref[0, 0, kk * tt + t], TOKEN_ROWS), TOKEN_ROWS), :] = v
        _run_copies(cnt_ref, off_ref, base_ref, dbuf, xs_hbm, sems.at[cur], True)

        @pl.when(i == n - 1)
        def _():
            _wait_run_copies(dbuf, xs_hbm, sems.at[cur], True)

            @pl.when(i >= 1)
            def _():
                _wait_run_copies(dbufs[1 - cur], xs_hbm, sems.at[1 - cur], True)

            pad_copies(True)

    @pl.when(i % 2 == 0)
    def _():
        step(0)

    @pl.when(i % 2 == 1)
    def _():
        step(1)


def _dispatch(zrow, zcnt, n_used, pos, cnt_t, off_t, base_t, xn2, n_slots, tt, blk):
    n_tiles = cnt_t.shape[0]
    smem = lambda w: pl.BlockSpec((1, 1, w), lambda i, zr, zc, nu: (i, 0, 0), memory_space=pltpu.SMEM)
    rows = TOP_K * tt * TOKEN_ROWS
    grid_spec = pltpu.PrefetchScalarGridSpec(
        num_scalar_prefetch=3,
        grid=(n_tiles,),
        in_specs=[smem(TOP_K * tt), smem(LANES), smem(LANES), smem(LANES),
                  pl.BlockSpec((tt * TOKEN_ROWS, LANES), lambda i, zr, zc, nu: (i, 0))],
        out_specs=pl.BlockSpec(memory_space=pl.ANY),
        scratch_shapes=[pltpu.VMEM((rows, LANES), F32), pltpu.VMEM((rows, LANES), F32),
                        pltpu.VMEM((blk * TOKEN_ROWS, LANES), F32),
                        pltpu.SemaphoreType.DMA((2,)), pltpu.SemaphoreType.DMA((1,))],
    )
    return pl.pallas_call(
        functools.partial(_dispatch_kernel, tt=tt, blk=blk),
        grid_spec=grid_spec,
        out_shape=jax.ShapeDtypeStruct((n_slots * TOKEN_ROWS, LANES), F32),
        compiler_params=_cparams(("arbitrary",)),
        name="moe_dispatch",
    )(zrow, zcnt, n_used, pos, cnt_t, off_t, base_t, xn2)


def _expert_kernel(be_ref, res_ref, nu_ref, cself_ref, cnext_ref, slot_ref, x_ref, wg_ref, bg_ref, wu_ref, bu_ref,
                   wd_ref, bd_ref, y_ref, wbf, *, blk):
    b = pl.program_id(0)
    w_refs = (wg_ref, wu_ref, wd_ref)
    slot = slot_ref[b]

    @pl.when(cself_ref[b] == 1)
    def _():
        for j in range(3):
            wbf[slot, j] = w_refs[j][0].astype(BF16)

    @pl.when(b < nu_ref[0])
    def _():
        xb = _load_token_tiles(x_ref, 0, blk).astype(BF16)
        gt = jnp.minimum(jnp.dot(xb, wbf[slot, 0], preferred_element_type=F32) + bg_ref[0], SWIGLU_LIMIT)
        up = jnp.clip(jnp.dot(xb, wbf[slot, 1], preferred_element_type=F32) + bu_ref[0],
                      -SWIGLU_LIMIT, SWIGLU_LIMIT)
        hid = (up + 1.0) * gt * _sigmoid(SWIGLU_ALPHA * gt)
        _store_token_tiles(y_ref, 0,
                           jnp.dot(hid.astype(BF16), wbf[slot, 2], preferred_element_type=F32) + bd_ref[0])

    @pl.when(cnext_ref[b] == 1)
    def _():
        for j in range(3):
            wbf[1 - slot, j] = w_refs[j][0].astype(BF16)

    @pl.when(b >= nu_ref[0])
    def _():
        y_ref[...] = jnp.zeros_like(y_ref)


def _experts(block_expert, resident, n_used, cast_self, cast_next, slot, xs, p, blk):
    n_blocks = block_expert.shape[0]
    used = lambda b, nu: jnp.minimum(b, nu[0] - 1)
    wspec = lambda: pl.BlockSpec((1, D_MODEL, D_MODEL), lambda b, be, rs, nu, cs, cn, sl: (rs[used(b, nu)], 0, 0))
    bspec = lambda: pl.BlockSpec((1, 1, D_MODEL), lambda b, be, rs, nu, cs, cn, sl: (be[used(b, nu)], 0, 0))
    rspec = lambda: pl.BlockSpec((blk * TOKEN_ROWS, LANES), lambda b, be, rs, nu, cs, cn, sl: (used(b, nu), 0))
    grid_spec = pltpu.PrefetchScalarGridSpec(
        num_scalar_prefetch=6,
        grid=(n_blocks,),
        in_specs=[rspec(), wspec(), bspec(), wspec(), bspec(), wspec(), bspec()],
        out_specs=pl.BlockSpec((blk * TOKEN_ROWS, LANES), lambda b, be, rs, nu, cs, cn, sl: (b, 0)),
        scratch_shapes=[pltpu.VMEM((2, 3, D_MODEL, D_MODEL), BF16)],
    )
    return pl.pallas_call(
        functools.partial(_expert_kernel, blk=blk),
        grid_spec=grid_spec,
        out_shape=jax.ShapeDtypeStruct(xs.shape, F32),
        compiler_params=_cparams(("arbitrary",)),
        name="experts",
    )(block_expert, resident, n_used, cast_self, cast_next, slot, xs, p["exp_w_gate"], p["exp_b_gate"],
      p["exp_w_up"], p["exp_b_up"], p["exp_w_down"], p["exp_b_down"])


def _combine_kernel(pos_ref, pw_ref, cnt_ref, off_ref, base_ref, cntn_ref, offn_ref, basen_ref,
                    ys_hbm, x1_ref, nrm_ref, op_ref, os_ref, ybuf0, ybuf1, mbuf, sems, *, tt, n_p):
    i = pl.program_id(0)
    n = pl.num_programs(0)
    ybufs = (ybuf0, ybuf1)

    @pl.when(i == 0)
    def _():
        _run_copies(cnt_ref, off_ref, base_ref, ybuf0, ys_hbm, sems.at[0], False)

    def step(cur):
        @pl.when(i + 1 < n)
        def _():
            _run_copies(cntn_ref, offn_ref, basen_ref, ybufs[1 - cur], ys_hbm, sems.at[1 - cur], False)

        _wait_run_copies(ybufs[cur], ys_hbm, sems.at[cur], False)
        ybuf = ybufs[cur]
        for t in range(tt):
            acc = None
            for kk in range(TOP_K):
                j = kk * tt + t
                row = ybuf[pl.ds(pl.multiple_of(pos_ref[0, 0, j], TOKEN_ROWS), TOKEN_ROWS), :]
                term = pw_ref[0, 0, j] * row
                acc = term if acc is None else acc + term
            mbuf[t * TOKEN_ROWS:(t + 1) * TOKEN_ROWS, :] = acc

    @pl.when(i % 2 == 0)
    def _():
        step(0)

    @pl.when(i % 2 == 1)
    def _():
        step(1)

    res = _rms(x1_ref[...] + _load_token_tiles(mbuf, 0, tt), nrm_ref[...])

    @pl.when(i < n_p)
    def _():
        op_ref[...] = res

    @pl.when(i >= n_p)
    def _():
        os_ref[...] = res


def _combine(pos, pw, cnt_t, off_t, base_t, ys, x1, final_norm, n_prompt, tt):
    n = x1.shape[0]
    n_p = n_prompt // tt
    n_s = (n - n_prompt) // tt
    nt = n_p + n_s
    smem = lambda w: pl.BlockSpec((1, 1, w), lambda i: (i, 0, 0), memory_space=pltpu.SMEM)
    smem_next = lambda w: pl.BlockSpec((1, 1, w), lambda i: (jnp.minimum(i + 1, nt - 1), 0, 0),
                                       memory_space=pltpu.SMEM)
    rows = TOP_K * tt * TOKEN_ROWS
    return pl.pallas_call(
        functools.partial(_combine_kernel, tt=tt, n_p=n_p),
        grid=(nt,),
        in_specs=[smem(TOP_K * tt), smem(TOP_K * tt), smem(LANES), smem(LANES), smem(LANES),
                  smem_next(LANES), smem_next(LANES), smem_next(LANES),
                  pl.BlockSpec(memory_space=pl.ANY),
                  pl.BlockSpec((tt, D_MODEL), lambda i: (i, 0)),
                  pl.BlockSpec((1, D_MODEL), lambda i: (0, 0))],
        out_specs=[pl.BlockSpec((tt, D_MODEL), lambda i: (jnp.minimum(i, n_p - 1), 0)),
                   pl.BlockSpec((tt, D_MODEL), lambda i: (jnp.maximum(i - n_p, 0), 0))],
        out_shape=[jax.ShapeDtypeStruct((n_prompt, D_MODEL), F32),
                   jax.ShapeDtypeStruct((n - n_prompt, D_MODEL), F32)],
        scratch_shapes=[pltpu.VMEM((rows, LANES), F32), pltpu.VMEM((rows, LANES), F32),
                        pltpu.VMEM((tt * TOKEN_ROWS, LANES), F32), pltpu.SemaphoreType.DMA((2,))],
        compiler_params=_cparams(("arbitrary",)),
        name="moe_combine",
    )(pos, pw, cnt_t, off_t, base_t, cnt_t, off_t, base_t, ys, x1, final_norm)


def _pad_rows(a, rows):
    return jnp.pad(a, ((0, 0), (rows - a.shape[1], 0), (0, 0)))


def _tile(n, pref):
    t = pref
    while n % t:
        t //= 2
    return t


def _mixer_group(x2d, n_streams, t_len, lru_conv0, lru_h0, gdn_conv0, gdn_s0, shared_init, reset_first, chunk, p):
    tt = _tile(t_len, TOKEN_TILE)
    lru_out, u, w, qd, kd, at, z, gl, lru_conv1, lru_h1, gdn_conv1 = _pre(
        x2d, lru_conv0, lru_h0, gdn_conv0, shared_init, n_streams, t_len, tt, chunk, reset_first, p)
    sb = _tile(n_streams, REC_STREAMS)
    gdn_out, gdn_s1 = _rec(u, w, qd, kd, at, z, gl, gdn_s0, shared_init, n_streams, t_len, sb, chunk, p)
    return lru_out, gdn_out, (lru_conv1, lru_h1, gdn_conv1, gdn_s1)


def _routing(cnt_tiles, n_tok, blk):
    cnt_t = cnt_tiles[:, 0, :]
    counts = jnp.sum(cnt_t, axis=0)
    padded = (counts + blk - 1) // blk * blk
    pends = jnp.cumsum(padded)
    pstarts = pends - padded
    off_t = jnp.cumsum(cnt_t, axis=1) - cnt_t
    base_t = pstarts[None, :] + jnp.cumsum(cnt_t, axis=0) - cnt_t
    n_blocks = (n_tok * TOP_K + N_EXPERTS * (blk - 1) + blk - 1) // blk
    starts = jnp.arange(n_blocks, dtype=jnp.int32) * blk
    block_expert = jnp.minimum(jnp.sum((pends[None, :N_EXPERTS] <= starts[:, None]).astype(jnp.int32), axis=1),
                               N_EXPERTS - 1)
    n_used = (pends[N_EXPERTS - 1] // blk).astype(jnp.int32)[None]
    b_id = jnp.arange(n_blocks, dtype=jnp.int32)
    in_used = b_id < n_used[0]
    change = block_expert[1:] != block_expert[:-1]
    first = jnp.concatenate([jnp.ones((1,), bool), change])
    last = jnp.concatenate([change, jnp.ones((1,), bool)]) | (b_id == n_used[0] - 1)
    e_id = jnp.arange(N_EXPERTS, dtype=jnp.int32)
    nblk = padded[:N_EXPERTS] // blk
    has = nblk > 0
    later = (e_id[None, :] > e_id[:, None]) & has[None, :]
    earlier = (e_id[None, :] < e_id[:, None]) & has[None, :]
    next_e = jnp.min(jnp.where(later, e_id[None, :], N_EXPERTS), axis=1)
    prev_e = jnp.max(jnp.where(earlier, e_id[None, :], -1), axis=1)
    has_next = next_e < N_EXPERTS
    has_prev = prev_e >= 0
    next_e = jnp.where(has_next, next_e, e_id)
    prev_e = jnp.where(has_prev, prev_e, e_id)
    precast = has_prev & (nblk[prev_e] >= 2)
    resident = jnp.where(first, block_expert, next_e[block_expert]).astype(jnp.int32)
    cast_self = (first & ~precast[block_expert] & in_used).astype(jnp.int32)
    cast_next = (last & ~first & has_next[block_expert] & in_used).astype(jnp.int32)
    slot = ((jnp.cumsum(has.astype(jnp.int32)) - 1) % 2)[block_expert].astype(jnp.int32)
    r3 = lambda a: (a * TOKEN_ROWS).astype(jnp.int32)[:, None, :]
    zrow = ((pstarts + counts) * TOKEN_ROWS).astype(jnp.int32)[:N_EXPERTS]
    zcnt = (padded - counts).astype(jnp.int32)[:N_EXPERTS]
    return (cnt_t.astype(jnp.int32)[:, None, :], r3(off_t), r3(base_t),
            (block_expert, resident, n_used, cast_self, cast_next, slot), n_used, zrow, zcnt, n_blocks * blk)


def kernel(x_prompt, x_sample, state_lru_conv, state_lru_h, state_gdn_conv, state_gdn_S, meta_tokens, norm_mix, w_in, lru_conv_w, lru_conv_b, lru_w_a, lru_b_a, lru_w_x, lru_b_x, lru_lambda, lru_norm, gdn_conv_w, gdn_A_log, gdn_dt_bias, gdn_norm, w_out, norm_ffn, router_w, router_b, exp_w_gate, exp_b_gate, exp_w_up, exp_b_up, exp_w_down, exp_b_down, final_norm):
    bp, tp, _ = x_prompt.shape
    bs, ts, _ = x_sample.shape
    n_meta = meta_tokens.shape[0]
    n_prompt, n_sample = bp * tp, bs * ts
    tt = MOE_TILE
    assert n_prompt % tt == 0 and n_sample % tt == 0 and EXPERT_ROWS <= 2 * MOE_TILE

    def blockdiag(w):
        eye = jnp.eye(LRU_BLOCKS, dtype=w.dtype)
        return jnp.einsum("ncd,nm->ncmd", w, eye).reshape(D_LRU, D_LRU)

    lane_pad = lambda v: jnp.pad(v, (0, LANES - v.shape[0]))[None]
    p = dict(
        norm_mix=norm_mix[0][None],
        w_in=jnp.pad(w_in[0], ((0, 0), (0, D_IN_PAD - D_IN))).astype(BF16),
        lru_conv_w=lru_conv_w[0], lru_conv_b=lru_conv_b[0][None],
        lru_wa_bd=blockdiag(lru_w_a[0]).astype(BF16), lru_b_a=lru_b_a[0][None],
        lru_wx_bd=blockdiag(lru_w_x[0]).astype(BF16), lru_b_x=lru_b_x[0][None],
        lru_lambda=lru_lambda[0][None], lru_norm=lru_norm[0][None],
        gdn_conv_w=gdn_conv_w[0],
        gdn_par=jnp.concatenate([lane_pad(gdn_A_log[0]), lane_pad(gdn_dt_bias[0])], axis=0),
        gdn_norm=gdn_norm[0][None],
        w_out1=w_out[0][:D_LRU].astype(BF16), w_out2=w_out[0][D_LRU:].astype(BF16),
        norm_ffn=norm_ffn[0][None],
        router_w=router_w[0].T.astype(BF16),
        router_b=jnp.broadcast_to(router_b[0][:, None], (N_EXPERTS, LANES)),
        exp_w_gate=exp_w_gate[0], exp_b_gate=exp_b_gate[0][:, None, :],
        exp_w_up=exp_w_up[0], exp_b_up=exp_b_up[0][:, None, :],
        exp_w_down=exp_w_down[0], exp_b_down=exp_b_down[0][:, None, :],
    )

    zc = lambda c: jnp.zeros((1, SUBLANES, c), F32)
    _, _, m_state = _mixer_group(meta_tokens, 1, n_meta, zc(D_LRU), jnp.zeros((1, 1, D_LRU), F32), zc(GDN_QKV),
                                 jnp.zeros((1, GDN_HEADS, GDN_DK, GDN_DV), F32), False, True, n_meta, p)
    xp2 = x_prompt.reshape(n_prompt, D_MODEL)
    xs2 = x_sample.reshape(n_sample, D_MODEL)
    p_lru, p_gdn, p_state = _mixer_group(xp2, bp, tp, m_state[0], m_state[1], m_state[2], m_state[3], True, False,
                                         GDN_CHUNK, p)
    s_lru, s_gdn, s_state = _mixer_group(xs2, bs, ts, _pad_rows(state_lru_conv[0], SUBLANES),
                                         state_lru_h[0][:, None, :], _pad_rows(state_gdn_conv[0], SUBLANES),
                                         state_gdn_S[0], False, False, GDN_CHUNK, p)

    x1, xn2, route, pw, cnt_tiles = _post(p_lru, p_gdn, xp2, s_lru, s_gdn, xs2, tt, p)
    n_tok = n_prompt + n_sample
    cnt_t, off_t, base_t, expert_tables, n_used, zrow, zcnt, n_slots = _routing(cnt_tiles, n_tok, EXPERT_ROWS)
    n_tiles = n_tok // tt
    pos = (route.reshape(n_tiles, SUBLANES, tt)[:, TOP_K:2 * TOP_K] * TOKEN_ROWS).reshape(n_tiles, 1, TOP_K * tt)
    pw_s = pw.reshape(n_tiles, SUBLANES, tt)[:, :TOP_K].reshape(n_tiles, 1, TOP_K * tt)
    xs = _dispatch(zrow, zcnt, n_used, pos, cnt_t, off_t, base_t, xn2, n_slots, tt, EXPERT_ROWS)
    ys = _experts(*expert_tables, xs, p, EXPERT_ROWS)
    y_p, y_s = _combine(pos, pw_s, cnt_t, off_t, base_t, ys, x1, final_norm[None], n_prompt, tt)

    def states(st, b):
        return (st[0][:, SUBLANES - 3:, :][None], st[1].reshape(1, b, D_LRU), st[2][:, SUBLANES - 3:, :][None],
                st[3][None])

    return (y_p.reshape(bp, tp, D_MODEL), y_s.reshape(bs, ts, D_MODEL)) + states(p_state, bp) + states(s_state, bs)
```

```python
import functools

import jax
import jax.numpy as jnp
from jax import lax
from jax.experimental import pallas as pl
from jax.experimental.pallas import tpu as pltpu

F32 = jnp.float32
BF16 = jnp.bfloat16

D_MODEL = 1024
D_LRU = 512
LRU_BLOCKS = 8
LRU_C = 8.0
CONV_W = 4
GDN_HEADS = 4
GDN_DK = 128
GDN_DV = 128
GDN_QKV = GDN_HEADS * (2 * GDN_DK + GDN_DV)
D_IN = 2 * D_LRU + GDN_QKV + GDN_HEADS * GDN_DV + 2 * GDN_HEADS
N_EXPERTS = 32
TOP_K = 4
SWIGLU_LIMIT = 7.0
SWIGLU_ALPHA = 1.702
EPS = 1e-6

LANES = 128
SUBLANES = 8
D_IN_PAD = 3200
COL_AB = (2 * D_LRU + GDN_QKV + GDN_HEADS * GDN_DV) // LANES
VMEM_LIMIT = 56 * 1024 * 1024

TOKEN_TILE = 256
MOE_TILE = 512
EXPERT_ROWS = 512
GDN_CHUNK = 64
REC_STREAMS = 8
PRE_CHAINS = 8
PROJ_PIECE = 256
PROJ_EARLY_POINTS = 8
TOKEN_ROWS = D_MODEL // LANES
RUN_BITS = tuple(1 << i for i in range(MOE_TILE.bit_length() - 1, -1, -1))


def _cparams(sem):
    return pltpu.CompilerParams(dimension_semantics=sem, vmem_limit_bytes=VMEM_LIMIT)


def _rms(x, gain):
    return x * lax.rsqrt(jnp.mean(x * x, axis=-1, keepdims=True) + EPS) * gain


def _softplus(x):
    return jnp.maximum(x, 0.0) + jnp.log1p(jnp.exp(-jnp.abs(x)))


def _sigmoid(x):
    return 1.0 / (1.0 + jnp.exp(-x))


def _load_token_tiles(ref, tok0, n_tok):
    return jnp.concatenate(
        [ref[pl.ds(tok0 * TOKEN_ROWS + j, n_tok, stride=TOKEN_ROWS), :] for j in range(TOKEN_ROWS)], axis=1)


def _store_token_tiles(ref, tok0, val):
    for j in range(TOKEN_ROWS):
        ref[pl.ds(tok0 * TOKEN_ROWS + j, val.shape[0], stride=TOKEN_ROWS), :] = val[:, j * LANES:(j + 1) * LANES]


def _mm(a, b):
    return jnp.dot(a.astype(BF16), b.astype(BF16), preferred_element_type=F32)


def _mm_nt(a, b):
    return lax.dot_general(a.astype(BF16), b.astype(BF16), (((1,), (1,)), ((), ())), preferred_element_type=F32)


def _conv_step(buf, x, w_ref, tt):
    buf[SUBLANES:SUBLANES + tt, :] = x
    y = x * w_ref[CONV_W - 1:CONV_W, :]
    for j in range(CONV_W - 2, -1, -1):
        y = y + buf[SUBLANES - 3 + j:SUBLANES - 3 + j + tt, :] * w_ref[j:j + 1, :]
    tail = buf[tt:tt + SUBLANES, :]
    buf[0:SUBLANES, :] = tail
    return y


def _pre_kernel(x_ref, xnext_ref, nrm_ref, win_ref, lconv0_ref, h0_ref, gconv0_ref,
                lcw_ref, lcb_ref, wa_ref, ba_ref, wx_ref, bx_ref, lam_ref, lnrm_ref, gcw_ref, par_ref,
                lru_ref, u_ref, w_ref, qd_ref, kd_ref, at_ref, z_ref, gl_ref, lconv1_ref, h1_ref, gconv1_ref,
                lbuf, a_buf, b_buf, hcar, gbuf, pbuf, *, tt, c, reset_first):
    t = pl.program_id(1)
    hd = GDN_HEADS * GDN_DK

    @pl.when(t == 0)
    def _():
        lbuf[0:SUBLANES, :] = lconv0_ref[0]
        hcar[...] = h0_ref[0]
        for j in range(3):
            gbuf[j, 0:SUBLANES, :] = gconv0_ref[0, :, j * hd:(j + 1) * hd]

    @pl.when((pl.program_id(0) == 0) & (t == 0))
    def _():
        pbuf[...] = jnp.dot(_rms(x_ref[...], nrm_ref[...]).astype(BF16), win_ref[...], preferred_element_type=F32)

    th = tt
    projs = [pbuf]
    xnext = _rms(xnext_ref[...], nrm_ref[...]).astype(BF16)
    pieces = [(lo, min(lo + PROJ_PIECE, D_IN_PAD)) for lo in range(0, D_IN_PAD, PROJ_PIECE)]
    z_col = 2 * D_LRU + GDN_QKV
    assert z_col % PROJ_PIECE == 0
    pieces = [pc for pc in pieces if pc[0] >= z_col] + [pc for pc in pieces if pc[0] < z_col]
    n_levels = max(c.bit_length() - 1, 1)
    n_solve_points = -(-(tt // c) * GDN_HEADS // PRE_CHAINS) * n_levels
    n_late = max(len(pieces) - PROJ_EARLY_POINTS, 0)
    plan = [1] * PROJ_EARLY_POINTS + [(i + 1) * n_late // n_solve_points - i * n_late // n_solve_points
                                      for i in range(n_solve_points)]
    points_done = [0]
    pieces_done = [0]

    def next_proj_pieces():
        k = plan[points_done[0]]
        points_done[0] += 1
        for lo, hi in pieces[pieces_done[0]:pieces_done[0] + k]:
            pbuf[:, lo:hi] = jnp.dot(xnext, win_ref[:, lo:hi], preferred_element_type=F32)
        pieces_done[0] += k

    rowc = lax.broadcasted_iota(jnp.int32, (c, LANES), 0)
    ri = lax.broadcasted_iota(jnp.int32, (c, c), 0)
    ci = lax.broadcasted_iota(jnp.int32, (c, c), 1)
    causal = ri >= ci
    strict = ri > ci
    eye = ri == ci
    lane_pad = jnp.zeros((c, LANES - c), F32)
    row8 = lax.broadcasted_iota(jnp.int32, (SUBLANES, D_LRU), 0)

    def group(gi, h):
        g0 = pl.multiple_of(gi * SUBLANES, SUBLANES)
        a8 = a_buf[pl.ds(g0, SUBLANES), :]
        b8 = b_buf[pl.ds(g0, SUBLANES), :]
        for d in (1, 2, 4):
            keep = row8 >= d
            b8 = jnp.where(keep, a8 * pltpu.roll(b8, d, 0) + b8, b8)
            a8 = jnp.where(keep, a8 * pltpu.roll(a8, d, 0), a8)
        h8 = a8 * h + b8
        b_buf[pl.ds(g0, SUBLANES), :] = h8
        return h8[SUBLANES - 1:SUBLANES, :]

    h_last = hcar[...]
    for part, proj in enumerate(projs):
        r0 = part * th

        z_ref[r0:r0 + th, :] = proj[:, 2 * D_LRU + GDN_QKV:2 * D_LRU + GDN_QKV + hd]
        ab = proj[:, COL_AB * LANES:(COL_AB + 1) * LANES]
        g_all = -jnp.exp(par_ref[0:1, :]) * _softplus(ab + par_ref[1:2, :])
        beta_all = _sigmoid(ab)
        next_proj_pieces()

        xc = _conv_step(lbuf, proj[:, 0:D_LRU], lcw_ref, th) + lcb_ref[...]
        next_proj_pieces()
        xb = xc.astype(BF16)
        r = _sigmoid(jnp.dot(xb, wa_ref[...], preferred_element_type=F32) + ba_ref[...])
        i = _sigmoid(jnp.dot(xb, wx_ref[...], preferred_element_type=F32) + bx_ref[...])
        next_proj_pieces()
        log_a = (-LRU_C) * r * _softplus(-lam_ref[...])
        a = jnp.exp(log_a)
        mult = jnp.sqrt(-jnp.tanh(log_a) * (a * a + 1.0))
        if reset_first and part == 0:
            row = lax.broadcasted_iota(jnp.int32, (th, D_LRU), 0)
            mult = jnp.where((row == 0) & (t == 0), 1.0, mult)
        a_buf[0:th, :] = a
        b_buf[0:th, :] = mult * i * xc
        next_proj_pieces()
        h_last = lax.fori_loop(0, th // SUBLANES, group, h_last)
        lru_ref[r0:r0 + th, :] = _rms(b_buf[0:th, :] * jax.nn.gelu(proj[:, D_LRU:2 * D_LRU]),
                                      lnrm_ref[...]).astype(lru_ref.dtype)
        next_proj_pieces()

        def conv_silu(j, proj=proj):
            y = _conv_step(gbuf.at[j], proj[:, 2 * D_LRU + j * hd:2 * D_LRU + (j + 1) * hd],
                           gcw_ref.at[:, j * hd:(j + 1) * hd], th)
            return y * _sigmoid(y)

        qa = conv_silu(0)
        next_proj_pieces()
        ka = conv_silu(1)
        next_proj_pieces()
        va = conv_silu(2)
        next_proj_pieces()

        pairs = [(j, h) for j in range(th // c) for h in range(GDN_HEADS)]
        for p0 in range(0, len(pairs), PRE_CHAINS):
            group_pairs = pairs[p0:p0 + PRE_CHAINS]
            gcs = {}
            for j in sorted({j for j, _ in group_pairs}):
                gc = g_all[j * c:(j + 1) * c]
                d = 1
                while d < c:
                    gc = gc + jnp.where(rowc >= d, pltpu.roll(gc, d, 0), 0.0)
                    d *= 2
                gcs[j] = gc
            ns, xs = [], []
            for j, h in group_pairs:
                rows = slice(j * c, (j + 1) * c)
                orow = slice(r0 + j * c, r0 + (j + 1) * c)
                sl = slice(h * GDN_DK, (h + 1) * GDN_DK)
                gc = gcs[j]
                q = qa[rows, sl]
                k = ka[rows, sl]
                q = q * lax.rsqrt(jnp.sum(q * q, axis=-1, keepdims=True) + EPS) * (GDN_DK ** -0.5)
                k = k * lax.rsqrt(jnp.sum(k * k, axis=-1, keepdims=True) + EPS)
                gcol = gc[:, h:h + 1]
                bcol = beta_all[rows, GDN_HEADS + h:GDN_HEADS + h + 1]
                grow = jnp.sum(jnp.where(eye, gcol, 0.0), axis=0, keepdims=True)
                decay = jnp.where(causal, jnp.exp(jnp.where(causal, gcol - grow, 0.0)), 0.0)
                egc = jnp.exp(gcol)
                g_last = gc[c - 1:c, h:h + 1]
                kb = k * bcol
                qd_ref[orow, sl] = (q * egc).astype(qd_ref.dtype)
                kd_ref[orow, sl] = (k * jnp.exp(g_last - gcol)).astype(kd_ref.dtype)
                gl_ref[r0 // c + j, :, sl] = jnp.broadcast_to(jnp.exp(g_last), (1, GDN_DV))
                at_ref[orow, sl] = jnp.concatenate([_mm_nt(q, k) * decay, lane_pad], axis=1).astype(at_ref.dtype)
                ns.append(jnp.where(strict, _mm_nt(kb, k) * decay, 0.0))
                xs.append(jnp.concatenate([va[rows, sl] * bcol, kb * egc], axis=1))
            xs = [x - _mm(n, x) for n, x in zip(ns, xs)]
            next_proj_pieces()
            m = 2
            while m < c:
                ns = [_mm(n, n) for n in ns]
                xs = [x + _mm(n, x) for n, x in zip(ns, xs)]
                next_proj_pieces()
                m *= 2
            for (j, h), x in zip(group_pairs, xs):
                orow = slice(r0 + j * c, r0 + (j + 1) * c)
                sl = slice(h * GDN_DK, (h + 1) * GDN_DK)
                u_ref[orow, sl] = x[:, :GDN_DV]
                w_ref[orow, sl] = x[:, GDN_DV:].astype(w_ref.dtype)
    hcar[...] = h_last
    assert points_done[0] == len(plan) and pieces_done[0] == len(pieces)

    @pl.when(t == pl.num_programs(1) - 1)
    def _():
        lconv1_ref[0] = lbuf[0:SUBLANES, :]
        h1_ref[0] = h_last
        for j in range(3):
            gconv1_ref[0, :, j * hd:(j + 1) * hd] = gbuf[j, 0:SUBLANES, :]


def _pre(x2d, lconv0, h0, gconv0, shared_init, n_streams, t_len, tt, c, reset_first, p):
    nt = t_len // tt
    hd = GDN_HEADS * GDN_DK
    n = n_streams * t_len
    st = (lambda s, t: (0, 0, 0)) if shared_init else (lambda s, t: (s, 0, 0))
    full = lambda a, b: pl.BlockSpec((a, b), lambda s, t: (0, 0))
    row = lambda w: pl.BlockSpec((tt, w), lambda s, t: (s * nt + t, 0))
    return pl.pallas_call(
        functools.partial(_pre_kernel, tt=tt, c=c, reset_first=reset_first),
        grid=(n_streams, nt),
        in_specs=[row(D_MODEL),
                  pl.BlockSpec((tt, D_MODEL), lambda s, t: (jnp.minimum(s * nt + t + 1, n_streams * nt - 1), 0)),
                  full(1, D_MODEL), full(D_MODEL, D_IN_PAD),
                  pl.BlockSpec((1, SUBLANES, D_LRU), st), pl.BlockSpec((1, 1, D_LRU), st),
                  pl.BlockSpec((1, SUBLANES, GDN_QKV), st),
                  full(CONV_W, D_LRU), full(1, D_LRU), full(D_LRU, D_LRU), full(1, D_LRU), full(D_LRU, D_LRU),
                  full(1, D_LRU), full(1, D_LRU), full(1, D_LRU), full(CONV_W, GDN_QKV), full(2, LANES)],
        out_specs=[row(D_LRU), row(hd), row(hd), row(hd), row(hd), row(hd), row(hd),
                   pl.BlockSpec((tt // c, 1, hd), lambda s, t: (s * nt + t, 0, 0)),
                   pl.BlockSpec((1, SUBLANES, D_LRU), lambda s, t: (s, 0, 0)),
                   pl.BlockSpec((1, 1, D_LRU), lambda s, t: (s, 0, 0)),
                   pl.BlockSpec((1, SUBLANES, GDN_QKV), lambda s, t: (s, 0, 0))],
        out_shape=[jax.ShapeDtypeStruct((n, D_LRU), BF16),
                   jax.ShapeDtypeStruct((n, hd), F32),
                   jax.ShapeDtypeStruct((n, hd), BF16),
                   jax.ShapeDtypeStruct((n, hd), BF16),
                   jax.ShapeDtypeStruct((n, hd), BF16),
                   jax.ShapeDtypeStruct((n, hd), BF16),
                   jax.ShapeDtypeStruct((n, hd), F32),
                   jax.ShapeDtypeStruct((n // c, 1, hd), F32),
                   jax.ShapeDtypeStruct((n_streams, SUBLANES, D_LRU), F32),
                   jax.ShapeDtypeStruct((n_streams, 1, D_LRU), F32),
                   jax.ShapeDtypeStruct((n_streams, SUBLANES, GDN_QKV), F32)],
        scratch_shapes=[pltpu.VMEM((tt + SUBLANES, D_LRU), F32),
                        pltpu.VMEM((tt, D_LRU), F32),
                        pltpu.VMEM((tt, D_LRU), F32),
                        pltpu.VMEM((1, D_LRU), F32),
                        pltpu.VMEM((3, tt + SUBLANES, hd), F32),
                        pltpu.VMEM((tt, D_IN_PAD), F32)],
        compiler_params=_cparams(("arbitrary", "arbitrary")),
        name="mixer_pre",
    )(x2d, x2d, p["norm_mix"], p["w_in"], lconv0, h0, gconv0, p["lru_conv_w"], p["lru_conv_b"], p["lru_wa_bd"],
      p["lru_b_a"], p["lru_wx_bd"], p["lru_b_x"], p["lru_lambda"], p["lru_norm"], p["gdn_conv_w"], p["gdn_par"])


def _rec_kernel(u_ref, w_ref, qd_ref, kd_ref, at_ref, z_ref, gl_ref, s0_ref, nrm_ref, o_ref, s1_ref, s_scr,
                *, sb, c, shared_init):
    t = pl.program_id(1)

    @pl.when(t == 0)
    def _():
        for s in range(sb):
            s_scr[s] = s0_ref[0 if shared_init else s]

    chains = [(s, h, slice(h * GDN_DK, (h + 1) * GDN_DK)) for s in range(sb) for h in range(GDN_HEADS)]
    m1s = [jnp.dot(jnp.concatenate([w_ref[s, :, sl], qd_ref[s, :, sl]], axis=0), s_scr[s, h].astype(BF16),
                   preferred_element_type=F32) for s, h, sl in chains]
    vns = [(u_ref[s, :, sl] - m1[:c]).astype(BF16) for (s, h, sl), m1 in zip(chains, m1s)]
    for (s, h, sl), v_new in zip(chains, vns):
        s_scr[s, h] = s_scr[s, h] * gl_ref[s, 0, :, sl] + lax.dot_general(
            kd_ref[s, :, sl], v_new, (((0,), (0,)), ((), ())), preferred_element_type=F32)
    for (s, h, sl), m1, v_new in zip(chains, m1s, vns):
        o = m1[c:] + jnp.dot(at_ref[s, :, h * GDN_DK:h * GDN_DK + c], v_new, preferred_element_type=F32)
        z = z_ref[s, :, sl]
        o_ref[s, :, sl] = (_rms(o, nrm_ref[...]) * (z * _sigmoid(z))).astype(o_ref.dtype)

    @pl.when(t == pl.num_programs(1) - 1)
    def _():
        s1_ref[...] = s_scr[...]


def _rec(u, w, qd, kd, at, z, gl, s0, shared_init, n_streams, t_len, sb, c, p):
    hd = GDN_HEADS * GDN_DK
    nt = t_len // c
    v3 = lambda a: a.reshape(n_streams, t_len, hd)
    blk3 = lambda: pl.BlockSpec((sb, c, hd), lambda g, t: (g, t, 0))
    s_spec = (pl.BlockSpec((1, GDN_HEADS, GDN_DK, GDN_DV), lambda g, t: (0, 0, 0, 0)) if shared_init else
              pl.BlockSpec((sb, GDN_HEADS, GDN_DK, GDN_DV), lambda g, t: (g, 0, 0, 0)))
    out, s1 = pl.pallas_call(
        functools.partial(_rec_kernel, sb=sb, c=c, shared_init=shared_init),
        grid=(n_streams // sb, nt),
        in_specs=[blk3(), blk3(), blk3(), blk3(), blk3(), blk3(),
                  pl.BlockSpec((sb, 1, 1, hd), lambda g, t: (g, t, 0, 0)),
                  s_spec,
                  pl.BlockSpec((1, GDN_DV), lambda g, t: (0, 0))],
        out_specs=[blk3(), pl.BlockSpec((sb, GDN_HEADS, GDN_DK, GDN_DV), lambda g, t: (g, 0, 0, 0))],
        out_shape=[jax.ShapeDtypeStruct((n_streams, t_len, hd), BF16),
                   jax.ShapeDtypeStruct((n_streams, GDN_HEADS, GDN_DK, GDN_DV), F32)],
        scratch_shapes=[pltpu.VMEM((sb, GDN_HEADS, GDN_DK, GDN_DV), F32)],
        compiler_params=_cparams(("arbitrary", "arbitrary")),
        name="gdn_state",
    )(v3(u), v3(w), v3(qd), v3(kd), v3(at), v3(z), gl.reshape(n_streams, nt, 1, hd), s0, p["gdn_norm"])
    return out.reshape(n_streams * t_len, hd), s1


def _post_kernel(lru_p, gdn_p, x_p, lru_s, gdn_s, x_s, wo1_ref, wo2_ref, nrm_ref, rwt_ref, rb_ref,
                 x1_ref, xn2_ref, route_ref, pw_ref, cnt_ref, *, tt, n_p):
    i = pl.program_id(0)
    is_p = i < n_p
    lru = jnp.where(is_p, lru_p[...], lru_s[...])
    gdn = jnp.where(is_p, gdn_p[...], gdn_s[...])
    x = jnp.where(is_p, x_p[...], x_s[...])
    m = (jnp.dot(lru, wo1_ref[...], preferred_element_type=F32)
         + jnp.dot(gdn, wo2_ref[...], preferred_element_type=F32))
    x1 = x + m
    x1_ref[...] = x1
    xn2 = _rms(x1, nrm_ref[...])
    _store_token_tiles(xn2_ref, 0, xn2)
    logits = lax.dot_general(rwt_ref[...], xn2.astype(BF16), (((1,), (1,)), ((), ())),
                             preferred_element_type=F32) + rb_ref[:, 0:1]
    e_id = lax.broadcasted_iota(jnp.int32, (N_EXPERTS, tt), 0)
    vals, idxs = [], []
    for _ in range(TOP_K):
        mx = jnp.max(logits, axis=0, keepdims=True)
        ix = jnp.min(jnp.where(logits == mx, e_id, N_EXPERTS), axis=0, keepdims=True)
        logits = jnp.where(e_id == ix, -jnp.inf, logits)
        vals.append(mx)
        idxs.append(ix)
    es = [jnp.exp(v - vals[0]) for v in vals]
    den = es[0] + es[1] + es[2] + es[3]
    onehot = jnp.zeros((N_EXPERTS, tt), F32)
    for ix in idxs:
        onehot = onehot + (e_id == ix).astype(F32)
    ri = lax.broadcasted_iota(jnp.int32, (tt, tt), 0)
    ci = lax.broadcasted_iota(jnp.int32, (tt, tt), 1)
    earlier = jnp.dot(onehot.astype(BF16), (ri < ci).astype(BF16), preferred_element_type=F32)
    cnt = jnp.sum(onehot, axis=1, keepdims=True)
    ei = lax.broadcasted_iota(jnp.int32, (N_EXPERTS, N_EXPERTS), 0)
    ej = lax.broadcasted_iota(jnp.int32, (N_EXPERTS, N_EXPERTS), 1)
    lower = (ej < ei).astype(BF16)
    cnt_hi = jnp.floor(cnt * (1.0 / 16.0))
    excl = lambda v: jnp.dot(lower, jnp.broadcast_to(v, (N_EXPERTS, LANES)).astype(BF16),
                             preferred_element_type=F32)[:, 0:1]
    off = 16.0 * excl(cnt_hi) + excl(cnt - 16.0 * cnt_hi)
    place = earlier + off
    r8 = lax.broadcasted_iota(jnp.int32, (SUBLANES, tt), 0)
    route = jnp.zeros((SUBLANES, tt), jnp.int32)
    pw = jnp.zeros((SUBLANES, tt), F32)
    for kk in range(TOP_K):
        rank = jnp.sum(jnp.where(e_id == idxs[kk], place, 0.0), axis=0, keepdims=True).astype(jnp.int32)
        route = jnp.where(r8 == kk, idxs[kk], route)
        route = jnp.where(r8 == TOP_K + kk, rank, route)
        pw = jnp.where(r8 == kk, es[kk] / den, pw)
    route_ref[...] = route
    pw_ref[...] = pw
    le = lax.broadcasted_iota(jnp.int32, (N_EXPERTS, LANES), 0)
    ll = lax.broadcasted_iota(jnp.int32, (N_EXPERTS, LANES), 1)
    cnt_ref[0] = jnp.sum(jnp.where(le == ll, cnt, 0.0), axis=0, keepdims=True).astype(jnp.int32)


def _post(lru_p, gdn_p, x_p, lru_s, gdn_s, x_s, tt, p):
    n_p = x_p.shape[0] // tt
    n_s = x_s.shape[0] // tt
    n = x_p.shape[0] + x_s.shape[0]
    prow = lambda w: pl.BlockSpec((tt, w), lambda i: (jnp.minimum(i, n_p - 1), 0))
    srow = lambda w: pl.BlockSpec((tt, w), lambda i: (jnp.maximum(i - n_p, 0), 0))
    row = lambda w: pl.BlockSpec((tt, w), lambda i: (i, 0))
    full = lambda a, b: pl.BlockSpec((a, b), lambda i: (0, 0))
    return pl.pallas_call(
        functools.partial(_post_kernel, tt=tt, n_p=n_p),
        grid=(n_p + n_s,),
        in_specs=[prow(D_LRU), prow(D_LRU), prow(D_MODEL), srow(D_LRU), srow(D_LRU), srow(D_MODEL),
                  full(D_LRU, D_MODEL), full(D_LRU, D_MODEL),
                  full(1, D_MODEL), full(N_EXPERTS, D_MODEL), full(N_EXPERTS, LANES)],
        out_specs=[row(D_MODEL), pl.BlockSpec((tt * TOKEN_ROWS, LANES), lambda i: (i, 0)),
                   pl.BlockSpec((SUBLANES, tt), lambda i: (i, 0)), pl.BlockSpec((SUBLANES, tt), lambda i: (i, 0)),
                   pl.BlockSpec((1, 1, LANES), lambda i: (i, 0, 0))],
        out_shape=[jax.ShapeDtypeStruct((n, D_MODEL), F32), jax.ShapeDtypeStruct((n * TOKEN_ROWS, LANES), F32),
                   jax.ShapeDtypeStruct(((n_p + n_s) * SUBLANES, tt), jnp.int32),
                   jax.ShapeDtypeStruct(((n_p + n_s) * SUBLANES, tt), F32),
                   jax.ShapeDtypeStruct((n_p + n_s, 1, LANES), jnp.int32)],
        compiler_params=_cparams(("arbitrary",)),
        name="outproj_router",
    )(lru_p, gdn_p, x_p, lru_s, gdn_s, x_s, p["w_out1"], p["w_out2"], p["norm_ffn"], p["router_w"], p["router_b"])


def _run_copies(cnt_ref, off_ref, base_ref, vbuf, hbm, sem, to_hbm):
    for e in range(N_EXPERTS):
        cnt = cnt_ref[0, 0, e]
        off = off_ref[0, 0, e]
        base = base_ref[0, 0, e]
        for bit in RUN_BITS:
            done = (cnt & (-2 * bit)) * TOKEN_ROWS

            @pl.when((cnt & bit) != 0)
            def _():
                v = vbuf.at[pl.ds(pl.multiple_of(off + done, TOKEN_ROWS), bit * TOKEN_ROWS), :]
                h = hbm.at[pl.ds(pl.multiple_of(base + done, TOKEN_ROWS), bit * TOKEN_ROWS), :]
                if to_hbm:
                    pltpu.make_async_copy(v, h, sem).start()
                else:
                    pltpu.make_async_copy(h, v, sem).start()


def _wait_run_copies(vbuf, hbm, sem, to_hbm):
    h = hbm.at[pl.ds(0, vbuf.shape[0]), :]
    if to_hbm:
        pltpu.make_async_copy(vbuf, h, sem).wait()
    else:
        pltpu.make_async_copy(h, vbuf, sem).wait()


def _dispatch_kernel(zrow_ref, zcnt_ref, nu_ref, pos_ref, cnt_ref, off_ref, base_ref, x_ref, xs_hbm,
                     dbuf0, dbuf1, zbuf, sems, zsem, *, tt, blk):
    i = pl.program_id(0)
    n = pl.num_programs(0)
    dbufs = (dbuf0, dbuf1)

    def pad_copies(wait):
        for e in range(N_EXPERTS):
            cnt = zcnt_ref[e]
            row = zrow_ref[e]
            for bit in RUN_BITS:
                done = (cnt & (-2 * bit)) * TOKEN_ROWS

                @pl.when((cnt & bit) != 0)
                def _():
                    cp = pltpu.make_async_copy(
                        zbuf.at[pl.ds(0, bit * TOKEN_ROWS), :],
                        xs_hbm.at[pl.ds(pl.multiple_of(row + done, TOKEN_ROWS), bit * TOKEN_ROWS), :], zsem.at[0])
                    if wait:
                        cp.wait()
                    else:
                        cp.start()
        n_blocks = xs_hbm.shape[0] // (blk * TOKEN_ROWS)
        for j in range(N_EXPERTS):
            @pl.when(nu_ref[0] + j < n_blocks)
            def _():
                row = pl.multiple_of((nu_ref[0] + j) * (blk * TOKEN_ROWS), TOKEN_ROWS)
                cp = pltpu.make_async_copy(zbuf, xs_hbm.at[pl.ds(row, blk * TOKEN_ROWS), :], zsem.at[0])
                if wait:
                    cp.wait()
                else:
                    cp.start()

    @pl.when(i == 0)
    def _():
        zbuf[...] = jnp.zeros_like(zbuf)
        pad_copies(False)

    def step(cur):
        dbuf = dbufs[cur]

        @pl.when(i >= 2)
        def _():
            _wait_run_copies(dbuf, xs_hbm, sems.at[cur], True)

        for t in range(tt):
            v = x_ref[t * TOKEN_ROWS:(t + 1) * TOKEN_ROWS, :]
            for kk in range(TOP_K):
                dbuf[pl.ds(pl.multiple_of(pos_ref[0, 0, kk * tt + t], TOKEN_ROWS), TOKEN_ROWS), :] = v
        _run_copies(cnt_ref, off_ref, base_ref, dbuf, xs_hbm, sems.at[cur], True)

        @pl.when(i == n - 1)
        def _():
            _wait_run_copies(dbuf, xs_hbm, sems.at[cur], True)

            @pl.when(i >= 1)
            def _():
                _wait_run_copies(dbufs[1 - cur], xs_hbm, sems.at[1 - cur], True)

            pad_copies(True)

    @pl.when(i % 2 == 0)
    def _():
        step(0)

    @pl.when(i % 2 == 1)
    def _():
        step(1)


def _dispatch(zrow, zcnt, n_used, pos, cnt_t, off_t, base_t, xn2, n_slots, tt, blk):
    n_tiles = cnt_t.shape[0]
    smem = lambda w: pl.BlockSpec((1, 1, w), lambda i, zr, zc, nu: (i, 0, 0), memory_space=pltpu.SMEM)
    rows = TOP_K * tt * TOKEN_ROWS
    grid_spec = pltpu.PrefetchScalarGridSpec(
        num_scalar_prefetch=3,
        grid=(n_tiles,),
        in_specs=[smem(TOP_K * tt), smem(LANES), smem(LANES), smem(LANES),
                  pl.BlockSpec((tt * TOKEN_ROWS, LANES), lambda i, zr, zc, nu: (i, 0))],
        out_specs=pl.BlockSpec(memory_space=pl.ANY),
        scratch_shapes=[pltpu.VMEM((rows, LANES), F32), pltpu.VMEM((rows, LANES), F32),
                        pltpu.VMEM((blk * TOKEN_ROWS, LANES), F32),
                        pltpu.SemaphoreType.DMA((2,)), pltpu.SemaphoreType.DMA((1,))],
    )
    return pl.pallas_call(
        functools.partial(_dispatch_kernel, tt=tt, blk=blk),
        grid_spec=grid_spec,
        out_shape=jax.ShapeDtypeStruct((n_slots * TOKEN_ROWS, LANES), F32),
        compiler_params=_cparams(("arbitrary",)),
        name="moe_dispatch",
    )(zrow, zcnt, n_used, pos, cnt_t, off_t, base_t, xn2)


def _expert_kernel(be_ref, first_ref, nu_ref, nxt_ref, slot_ref, x_ref, wg_hbm, bg_ref, wu_hbm, bu_ref, wd_hbm,
                   bd_ref, y_ref, wst, wbf, wsem, *, blk):
    b = pl.program_id(0)
    w_hbm = (wg_hbm, wu_hbm, wd_hbm)

    def fetch(e, s, wait):
        for j in range(3):
            cp = pltpu.make_async_copy(w_hbm[j].at[pl.ds(e, 1)], wst.at[s, pl.ds(j, 1)], wsem.at[s])
            if wait:
                cp.wait()
            else:
                cp.start()

    @pl.when(b == 0)
    def _():
        fetch(be_ref[0], 0, False)

    for s in range(2):
        @pl.when((first_ref[b] == 1) & (b < nu_ref[0]) & (slot_ref[b] == s))
        def _(s=s):
            fetch(be_ref[b], s, True)
            fetch(nxt_ref[b], 1 - s, False)
            for j in range(3):
                wbf[j] = wst[s, j].astype(BF16)

        @pl.when((b == pl.num_programs(0) - 1) & (slot_ref[nu_ref[0] - 1] == s))
        def _(s=s):
            fetch(be_ref[nu_ref[0] - 1], 1 - s, True)

    @pl.when(b < nu_ref[0])
    def _():
        xb = _load_token_tiles(x_ref, 0, blk).astype(BF16)
        gt = jnp.minimum(jnp.dot(xb, wbf[0], preferred_element_type=F32) + bg_ref[0], SWIGLU_LIMIT)
        up = jnp.clip(jnp.dot(xb, wbf[1], preferred_element_type=F32) + bu_ref[0], -SWIGLU_LIMIT, SWIGLU_LIMIT)
        hid = (up + 1.0) * gt * _sigmoid(SWIGLU_ALPHA * gt)
        _store_token_tiles(y_ref, 0, jnp.dot(hid.astype(BF16), wbf[2], preferred_element_type=F32) + bd_ref[0])

    @pl.when(b >= nu_ref[0])
    def _():
        y_ref[...] = jnp.zeros_like(y_ref)


def _experts(block_expert, first, n_used, nxt, slot, xs, p, blk):
    n_blocks = block_expert.shape[0]
    used = lambda b, nu: jnp.minimum(b, nu[0] - 1)
    wspec = lambda: pl.BlockSpec(memory_space=pl.ANY)
    bspec = lambda: pl.BlockSpec((1, 1, D_MODEL), lambda b, be, fi, nu, nx, sl: (be[used(b, nu)], 0, 0))
    grid_spec = pltpu.PrefetchScalarGridSpec(
        num_scalar_prefetch=5,
        grid=(n_blocks,),
        in_specs=[pl.BlockSpec((blk * TOKEN_ROWS, LANES), lambda b, be, fi, nu, nx, sl: (used(b, nu), 0)),
                  wspec(), bspec(), wspec(), bspec(), wspec(), bspec()],
        out_specs=pl.BlockSpec((blk * TOKEN_ROWS, LANES), lambda b, be, fi, nu, nx, sl: (b, 0)),
        scratch_shapes=[pltpu.VMEM((2, 3, D_MODEL, D_MODEL), F32), pltpu.VMEM((3, D_MODEL, D_MODEL), BF16),
                        pltpu.SemaphoreType.DMA((2,))],
    )
    return pl.pallas_call(
        functools.partial(_expert_kernel, blk=blk),
        grid_spec=grid_spec,
        out_shape=jax.ShapeDtypeStruct(xs.shape, F32),
        compiler_params=_cparams(("arbitrary",)),
        name="experts",
    )(block_expert, first, n_used, nxt, slot, xs, p["exp_w_gate"], p["exp_b_gate"], p["exp_w_up"], p["exp_b_up"],
      p["exp_w_down"], p["exp_b_down"])


def _combine_kernel(pos_ref, pw_ref, cnt_ref, off_ref, base_ref, cntn_ref, offn_ref, basen_ref,
                    ys_hbm, x1_ref, nrm_ref, op_ref, os_ref, ybuf0, ybuf1, mbuf, sems, *, tt, n_p):
    i = pl.program_id(0)
    n = pl.num_programs(0)
    ybufs = (ybuf0, ybuf1)

    @pl.when(i == 0)
    def _():
        _run_copies(cnt_ref, off_ref, base_ref, ybuf0, ys_hbm, sems.at[0], False)

    def step(cur):
        @pl.when(i + 1 < n)
        def _():
            _run_copies(cntn_ref, offn_ref, basen_ref, ybufs[1 - cur], ys_hbm, sems.at[1 - cur], False)

        _wait_run_copies(ybufs[cur], ys_hbm, sems.at[cur], False)
        ybuf = ybufs[cur]
        for t in range(tt):
            acc = None
            for kk in range(TOP_K):
                j = kk * tt + t
                row = ybuf[pl.ds(pl.multiple_of(pos_ref[0, 0, j], TOKEN_ROWS), TOKEN_ROWS), :]
                term = pw_ref[0, 0, j] * row
                acc = term if acc is None else acc + term
            mbuf[t * TOKEN_ROWS:(t + 1) * TOKEN_ROWS, :] = acc

    @pl.when(i % 2 == 0)
    def _():
        step(0)

    @pl.when(i % 2 == 1)
    def _():
        step(1)

    res = _rms(x1_ref[...] + _load_token_tiles(mbuf, 0, tt), nrm_ref[...])

    @pl.when(i < n_p)
    def _():
        op_ref[...] = res

    @pl.when(i >= n_p)
    def _():
        os_ref[...] = res


def _combine(pos, pw, cnt_t, off_t, base_t, ys, x1, final_norm, n_prompt, tt):
    n = x1.shape[0]
    n_p = n_prompt // tt
    n_s = (n - n_prompt) // tt
    nt = n_p + n_s
    smem = lambda w: pl.BlockSpec((1, 1, w), lambda i: (i, 0, 0), memory_space=pltpu.SMEM)
    smem_next = lambda w: pl.BlockSpec((1, 1, w), lambda i: (jnp.minimum(i + 1, nt - 1), 0, 0),
                                       memory_space=pltpu.SMEM)
    rows = TOP_K * tt * TOKEN_ROWS
    return pl.pallas_call(
        functools.partial(_combine_kernel, tt=tt, n_p=n_p),
        grid=(nt,),
        in_specs=[smem(TOP_K * tt), smem(TOP_K * tt), smem(LANES), smem(LANES), smem(LANES),
                  smem_next(LANES), smem_next(LANES), smem_next(LANES),
                  pl.BlockSpec(memory_space=pl.ANY),
                  pl.BlockSpec((tt, D_MODEL), lambda i: (i, 0)),
                  pl.BlockSpec((1, D_MODEL), lambda i: (0, 0))],
        out_specs=[pl.BlockSpec((tt, D_MODEL), lambda i: (jnp.minimum(i, n_p - 1), 0)),
                   pl.BlockSpec((tt, D_MODEL), lambda i: (jnp.maximum(i - n_p, 0), 0))],
        out_shape=[jax.ShapeDtypeStruct((n_prompt, D_MODEL), F32),
                   jax.ShapeDtypeStruct((n - n_prompt, D_MODEL), F32)],
        scratch_shapes=[pltpu.VMEM((rows, LANES), F32), pltpu.VMEM((rows, LANES), F32),
                        pltpu.VMEM((tt * TOKEN_ROWS, LANES), F32), pltpu.SemaphoreType.DMA((2,))],
        compiler_params=_cparams(("arbitrary",)),
        name="moe_combine",
    )(pos, pw, cnt_t, off_t, base_t, cnt_t, off_t, base_t, ys, x1, final_norm)


def _pad_rows(a, rows):
    return jnp.pad(a, ((0, 0), (rows - a.shape[1], 0), (0, 0)))


def _tile(n, pref):
    t = pref
    while n % t:
        t //= 2
    return t


def _mixer_group(x2d, n_streams, t_len, lru_conv0, lru_h0, gdn_conv0, gdn_s0, shared_init, reset_first, chunk, p):
    tt = _tile(t_len, TOKEN_TILE)
    lru_out, u, w, qd, kd, at, z, gl, lru_conv1, lru_h1, gdn_conv1 = _pre(
        x2d, lru_conv0, lru_h0, gdn_conv0, shared_init, n_streams, t_len, tt, chunk, reset_first, p)
    sb = _tile(n_streams, REC_STREAMS)
    gdn_out, gdn_s1 = _rec(u, w, qd, kd, at, z, gl, gdn_s0, shared_init, n_streams, t_len, sb, chunk, p)
    return lru_out, gdn_out, (lru_conv1, lru_h1, gdn_conv1, gdn_s1)


def _routing(cnt_tiles, n_tok, blk):
    cnt_t = cnt_tiles[:, 0, :]
    counts = jnp.sum(cnt_t, axis=0)
    padded = (counts + blk - 1) // blk * blk
    pends = jnp.cumsum(padded)
    pstarts = pends - padded
    off_t = jnp.cumsum(cnt_t, axis=1) - cnt_t
    base_t = pstarts[None, :] + jnp.cumsum(cnt_t, axis=0) - cnt_t
    n_blocks = (n_tok * TOP_K + N_EXPERTS * (blk - 1) + blk - 1) // blk
    starts = jnp.arange(n_blocks, dtype=jnp.int32) * blk
    block_expert = jnp.minimum(jnp.sum((pends[None, :N_EXPERTS] <= starts[:, None]).astype(jnp.int32), axis=1),
                               N_EXPERTS - 1)
    first = jnp.concatenate([jnp.ones((1,), jnp.int32),
                             (block_expert[1:] != block_expert[:-1]).astype(jnp.int32)])
    n_used = (pends[N_EXPERTS - 1] // blk).astype(jnp.int32)[None]
    e_id = jnp.arange(N_EXPERTS, dtype=jnp.int32)
    later = (e_id[None, :] > e_id[:, None]) & (padded[None, :N_EXPERTS] > 0)
    next_e = jnp.min(jnp.where(later, e_id[None, :], N_EXPERTS), axis=1)
    next_e = jnp.where(next_e < N_EXPERTS, next_e, e_id)
    nxt = next_e[block_expert]
    slot = (jnp.cumsum(first) - 1).astype(jnp.int32) % 2
    r3 = lambda a: (a * TOKEN_ROWS).astype(jnp.int32)[:, None, :]
    zrow = ((pstarts + counts) * TOKEN_ROWS).astype(jnp.int32)[:N_EXPERTS]
    zcnt = (padded - counts).astype(jnp.int32)[:N_EXPERTS]
    return (cnt_t.astype(jnp.int32)[:, None, :], r3(off_t), r3(base_t), block_expert, first, n_used, nxt, slot,
            zrow, zcnt, n_blocks * blk)


def kernel(x_prompt, x_sample, state_lru_conv, state_lru_h, state_gdn_conv, state_gdn_S, meta_tokens, norm_mix, w_in, lru_conv_w, lru_conv_b, lru_w_a, lru_b_a, lru_w_x, lru_b_x, lru_lambda, lru_norm, gdn_conv_w, gdn_A_log, gdn_dt_bias, gdn_norm, w_out, norm_ffn, router_w, router_b, exp_w_gate, exp_b_gate, exp_w_up, exp_b_up, exp_w_down, exp_b_down, final_norm):
    bp, tp, _ = x_prompt.shape
    bs, ts, _ = x_sample.shape
    n_meta = meta_tokens.shape[0]
    n_prompt, n_sample = bp * tp, bs * ts
    tt = MOE_TILE
    assert n_prompt % tt == 0 and n_sample % tt == 0 and EXPERT_ROWS <= 2 * MOE_TILE

    def blockdiag(w):
        eye = jnp.eye(LRU_BLOCKS, dtype=w.dtype)
        return jnp.einsum("ncd,nm->ncmd", w, eye).reshape(D_LRU, D_LRU)

    lane_pad = lambda v: jnp.pad(v, (0, LANES - v.shape[0]))[None]
    p = dict(
        norm_mix=norm_mix[0][None],
        w_in=jnp.pad(w_in[0], ((0, 0), (0, D_IN_PAD - D_IN))).astype(BF16),
        lru_conv_w=lru_conv_w[0], lru_conv_b=lru_conv_b[0][None],
        lru_wa_bd=blockdiag(lru_w_a[0]).astype(BF16), lru_b_a=lru_b_a[0][None],
        lru_wx_bd=blockdiag(lru_w_x[0]).astype(BF16), lru_b_x=lru_b_x[0][None],
        lru_lambda=lru_lambda[0][None], lru_norm=lru_norm[0][None],
        gdn_conv_w=gdn_conv_w[0],
        gdn_par=jnp.concatenate([lane_pad(gdn_A_log[0]), lane_pad(gdn_dt_bias[0])], axis=0),
        gdn_norm=gdn_norm[0][None],
        w_out1=w_out[0][:D_LRU].astype(BF16), w_out2=w_out[0][D_LRU:].astype(BF16),
        norm_ffn=norm_ffn[0][None],
        router_w=router_w[0].T.astype(BF16),
        router_b=jnp.broadcast_to(router_b[0][:, None], (N_EXPERTS, LANES)),
        exp_w_gate=exp_w_gate[0], exp_b_gate=exp_b_gate[0][:, None, :],
        exp_w_up=exp_w_up[0], exp_b_up=exp_b_up[0][:, None, :],
        exp_w_down=exp_w_down[0], exp_b_down=exp_b_down[0][:, None, :],
    )

    zc = lambda c: jnp.zeros((1, SUBLANES, c), F32)
    _, _, m_state = _mixer_group(meta_tokens, 1, n_meta, zc(D_LRU), jnp.zeros((1, 1, D_LRU), F32), zc(GDN_QKV),
                                 jnp.zeros((1, GDN_HEADS, GDN_DK, GDN_DV), F32), False, True, n_meta, p)
    xp2 = x_prompt.reshape(n_prompt, D_MODEL)
    xs2 = x_sample.reshape(n_sample, D_MODEL)
    p_lru, p_gdn, p_state = _mixer_group(xp2, bp, tp, m_state[0], m_state[1], m_state[2], m_state[3], True, False,
                                         GDN_CHUNK, p)
    s_lru, s_gdn, s_state = _mixer_group(xs2, bs, ts, _pad_rows(state_lru_conv[0], SUBLANES),
                                         state_lru_h[0][:, None, :], _pad_rows(state_gdn_conv[0], SUBLANES),
                                         state_gdn_S[0], False, False, GDN_CHUNK, p)

    x1, xn2, route, pw, cnt_tiles = _post(p_lru, p_gdn, xp2, s_lru, s_gdn, xs2, tt, p)
    n_tok = n_prompt + n_sample
    (cnt_t, off_t, base_t, block_expert, first, n_used, nxt, slot, zrow, zcnt,
     n_slots) = _routing(cnt_tiles, n_tok, EXPERT_ROWS)
    n_tiles = n_tok // tt
    pos = (route.reshape(n_tiles, SUBLANES, tt)[:, TOP_K:2 * TOP_K] * TOKEN_ROWS).reshape(n_tiles, 1, TOP_K * tt)
    pw_s = pw.reshape(n_tiles, SUBLANES, tt)[:, :TOP_K].reshape(n_tiles, 1, TOP_K * tt)
    xs = _dispatch(zrow, zcnt, n_used, pos, cnt_t, off_t, base_t, xn2, n_slots, tt, EXPERT_ROWS)
    ys = _experts(block_expert, first, n_used, nxt, slot, xs, p, EXPERT_ROWS)
    y_p, y_s = _combine(pos, pw_s, cnt_t, off_t, base_t, ys, x1, final_norm[None], n_prompt, tt)

    def states(st, b):
        return (st[0][:, SUBLANES - 3:, :][None], st[1].reshape(1, b, D_LRU), st[2][:, SUBLANES - 3:, :][None],
                st[3][None])

    return (y_p.reshape(bp, tp, D_MODEL), y_s.reshape(bs, ts, D_MODEL)) + states(p_state, bp) + states(s_state, bs)
```

```python
import functools

import jax
import jax.numpy as jnp
from jax import lax
from jax.experimental import pallas as pl
from jax.experimental.pallas import tpu as pltpu

F32 = jnp.float32
BF16 = jnp.bfloat16

D_MODEL = 1024
D_LRU = 512
LRU_BLOCKS = 8
LRU_C = 8.0
CONV_W = 4
GDN_HEADS = 4
GDN_DK = 128
GDN_DV = 128
GDN_QKV = GDN_HEADS * (2 * GDN_DK + GDN_DV)
D_IN = 2 * D_LRU + GDN_QKV + GDN_HEADS * GDN_DV + 2 * GDN_HEADS
N_EXPERTS = 32
TOP_K = 4
SWIGLU_LIMIT = 7.0
SWIGLU_ALPHA = 1.702
EPS = 1e-6

LANES = 128
SUBLANES = 8
D_IN_PAD = 3200
COL_AB = (2 * D_LRU + GDN_QKV + GDN_HEADS * GDN_DV) // LANES
VMEM_LIMIT = 56 * 1024 * 1024

TOKEN_TILE = 256
MOE_TILE = 512
EXPERT_ROWS = 512
GDN_CHUNK = 64
REC_STREAMS = 8
PRE_CHAINS = 16
PROJ_PIECE = 256
PROJ_EARLY_POINTS = 8
TOKEN_ROWS = D_MODEL // LANES
RUN_BITS = tuple(1 << i for i in range(MOE_TILE.bit_length() - 1, -1, -1))


def _cparams(sem):
    return pltpu.CompilerParams(dimension_semantics=sem, vmem_limit_bytes=VMEM_LIMIT)


def _rms(x, gain):
    return x * lax.rsqrt(jnp.mean(x * x, axis=-1, keepdims=True) + EPS) * gain


def _softplus(x):
    return jnp.maximum(x, 0.0) + jnp.log1p(jnp.exp(-jnp.abs(x)))


def _sigmoid(x):
    return 1.0 / (1.0 + jnp.exp(-x))


def _load_token_tiles(ref, tok0, n_tok):
    return jnp.concatenate(
        [ref[pl.ds(tok0 * TOKEN_ROWS + j, n_tok, stride=TOKEN_ROWS), :] for j in range(TOKEN_ROWS)], axis=1)


def _store_token_tiles(ref, tok0, val):
    for j in range(TOKEN_ROWS):
        ref[pl.ds(tok0 * TOKEN_ROWS + j, val.shape[0], stride=TOKEN_ROWS), :] = val[:, j * LANES:(j + 1) * LANES]


def _mm(a, b):
    return jnp.dot(a.astype(BF16), b.astype(BF16), preferred_element_type=F32)


def _mm_nt(a, b):
    return lax.dot_general(a.astype(BF16), b.astype(BF16), (((1,), (1,)), ((), ())), preferred_element_type=F32)


def _conv_step(buf, x, w_ref, tt):
    buf[SUBLANES:SUBLANES + tt, :] = x
    y = x * w_ref[CONV_W - 1:CONV_W, :]
    for j in range(CONV_W - 2, -1, -1):
        y = y + buf[SUBLANES - 3 + j:SUBLANES - 3 + j + tt, :] * w_ref[j:j + 1, :]
    tail = buf[tt:tt + SUBLANES, :]
    buf[0:SUBLANES, :] = tail
    return y


def _pre_kernel(x_ref, xnext_ref, nrm_ref, win_ref, lconv0_ref, h0_ref, gconv0_ref,
                lcw_ref, lcb_ref, wa_ref, ba_ref, wx_ref, bx_ref, lam_ref, lnrm_ref, gcw_ref, par_ref,
                lru_ref, u_ref, w_ref, qd_ref, kd_ref, at_ref, z_ref, gl_ref, lconv1_ref, h1_ref, gconv1_ref,
                lbuf, a_buf, b_buf, hcar, gbuf, pbuf, *, tt, c, reset_first):
    t = pl.program_id(1)
    hd = GDN_HEADS * GDN_DK

    @pl.when(t == 0)
    def _():
        lbuf[0:SUBLANES, :] = lconv0_ref[0]
        hcar[...] = h0_ref[0]
        for j in range(3):
            gbuf[j, 0:SUBLANES, :] = gconv0_ref[0, :, j * hd:(j + 1) * hd]

    @pl.when((pl.program_id(0) == 0) & (t == 0))
    def _():
        pbuf[...] = jnp.dot(_rms(x_ref[...], nrm_ref[...]).astype(BF16), win_ref[...], preferred_element_type=F32)

    th = tt
    projs = [pbuf]
    xnext = _rms(xnext_ref[...], nrm_ref[...]).astype(BF16)
    pieces = [(lo, min(lo + PROJ_PIECE, D_IN_PAD)) for lo in range(0, D_IN_PAD, PROJ_PIECE)]
    z_col = 2 * D_LRU + GDN_QKV
    assert z_col % PROJ_PIECE == 0
    pieces = [pc for pc in pieces if pc[0] >= z_col] + [pc for pc in pieces if pc[0] < z_col]
    n_levels = max(c.bit_length() - 1, 1)
    n_solve_points = -(-(tt // c) * GDN_HEADS // PRE_CHAINS) * n_levels
    n_late = max(len(pieces) - PROJ_EARLY_POINTS, 0)
    plan = [1] * PROJ_EARLY_POINTS + [(i + 1) * n_late // n_solve_points - i * n_late // n_solve_points
                                      for i in range(n_solve_points)]
    points_done = [0]
    pieces_done = [0]

    def next_proj_pieces():
        k = plan[points_done[0]]
        points_done[0] += 1
        for lo, hi in pieces[pieces_done[0]:pieces_done[0] + k]:
            pbuf[:, lo:hi] = jnp.dot(xnext, win_ref[:, lo:hi], preferred_element_type=F32)
        pieces_done[0] += k

    rowc = lax.broadcasted_iota(jnp.int32, (c, LANES), 0)
    ri = lax.broadcasted_iota(jnp.int32, (c, c), 0)
    ci = lax.broadcasted_iota(jnp.int32, (c, c), 1)
    causal = ri >= ci
    strict = ri > ci
    eye = ri == ci
    lane_pad = jnp.zeros((c, LANES - c), F32)
    row8 = lax.broadcasted_iota(jnp.int32, (SUBLANES, D_LRU), 0)

    def group(gi, h):
        g0 = pl.multiple_of(gi * SUBLANES, SUBLANES)
        a8 = a_buf[pl.ds(g0, SUBLANES), :]
        b8 = b_buf[pl.ds(g0, SUBLANES), :]
        for d in (1, 2, 4):
            keep = row8 >= d
            b8 = jnp.where(keep, a8 * pltpu.roll(b8, d, 0) + b8, b8)
            a8 = jnp.where(keep, a8 * pltpu.roll(a8, d, 0), a8)
        h8 = a8 * h + b8
        b_buf[pl.ds(g0, SUBLANES), :] = h8
        return h8[SUBLANES - 1:SUBLANES, :]

    h_last = hcar[...]
    for part, proj in enumerate(projs):
        r0 = part * th

        z_ref[r0:r0 + th, :] = proj[:, 2 * D_LRU + GDN_QKV:2 * D_LRU + GDN_QKV + hd]
        ab = proj[:, COL_AB * LANES:(COL_AB + 1) * LANES]
        g_all = -jnp.exp(par_ref[0:1, :]) * _softplus(ab + par_ref[1:2, :])
        beta_all = _sigmoid(ab)
        next_proj_pieces()

        xc = _conv_step(lbuf, proj[:, 0:D_LRU], lcw_ref, th) + lcb_ref[...]
        next_proj_pieces()
        xb = xc.astype(BF16)
        r = _sigmoid(jnp.dot(xb, wa_ref[...], preferred_element_type=F32) + ba_ref[...])
        i = _sigmoid(jnp.dot(xb, wx_ref[...], preferred_element_type=F32) + bx_ref[...])
        next_proj_pieces()
        log_a = (-LRU_C) * r * _softplus(-lam_ref[...])
        a = jnp.exp(log_a)
        mult = jnp.sqrt(-jnp.tanh(log_a) * (a * a + 1.0))
        if reset_first and part == 0:
            row = lax.broadcasted_iota(jnp.int32, (th, D_LRU), 0)
            mult = jnp.where((row == 0) & (t == 0), 1.0, mult)
        a_buf[0:th, :] = a
        b_buf[0:th, :] = mult * i * xc
        next_proj_pieces()
        h_last = lax.fori_loop(0, th // SUBLANES, group, h_last)
        lru_ref[r0:r0 + th, :] = _rms(b_buf[0:th, :] * jax.nn.gelu(proj[:, D_LRU:2 * D_LRU]),
                                      lnrm_ref[...]).astype(lru_ref.dtype)
        next_proj_pieces()

        def conv_silu(j, proj=proj):
            y = _conv_step(gbuf.at[j], proj[:, 2 * D_LRU + j * hd:2 * D_LRU + (j + 1) * hd],
                           gcw_ref.at[:, j * hd:(j + 1) * hd], th)
            return y * _sigmoid(y)

        qa = conv_silu(0)
        next_proj_pieces()
        ka = conv_silu(1)
        next_proj_pieces()
        va = conv_silu(2)
        next_proj_pieces()

        pairs = [(j, h) for j in range(th // c) for h in range(GDN_HEADS)]
        for p0 in range(0, len(pairs), PRE_CHAINS):
            group_pairs = pairs[p0:p0 + PRE_CHAINS]
            gcs = {}
            for j in sorted({j for j, _ in group_pairs}):
                gc = g_all[j * c:(j + 1) * c]
                d = 1
                while d < c:
                    gc = gc + jnp.where(rowc >= d, pltpu.roll(gc, d, 0), 0.0)
                    d *= 2
                gcs[j] = gc
            ns, xs = [], []
            for j, h in group_pairs:
                rows = slice(j * c, (j + 1) * c)
                orow = slice(r0 + j * c, r0 + (j + 1) * c)
                sl = slice(h * GDN_DK, (h + 1) * GDN_DK)
                gc = gcs[j]
                q = qa[rows, sl]
                k = ka[rows, sl]
                q = q * lax.rsqrt(jnp.sum(q * q, axis=-1, keepdims=True) + EPS) * (GDN_DK ** -0.5)
                k = k * lax.rsqrt(jnp.sum(k * k, axis=-1, keepdims=True) + EPS)
                gcol = gc[:, h:h + 1]
                bcol = beta_all[rows, GDN_HEADS + h:GDN_HEADS + h + 1]
                grow = jnp.sum(jnp.where(eye, gcol, 0.0), axis=0, keepdims=True)
                decay = jnp.where(causal, jnp.exp(jnp.where(causal, gcol - grow, 0.0)), 0.0)
                egc = jnp.exp(gcol)
                g_last = gc[c - 1:c, h:h + 1]
                kb = k * bcol
                qd_ref[orow, sl] = (q * egc).astype(qd_ref.dtype)
                kd_ref[orow, sl] = (k * jnp.exp(g_last - gcol)).astype(kd_ref.dtype)
                gl_ref[r0 // c + j, :, sl] = jnp.broadcast_to(jnp.exp(g_last), (1, GDN_DV))
                at_ref[orow, sl] = jnp.concatenate([_mm_nt(q, k) * decay, lane_pad], axis=1).astype(at_ref.dtype)
                ns.append(jnp.where(strict, _mm_nt(kb, k) * decay, 0.0))
                xs.append(jnp.concatenate([va[rows, sl] * bcol, kb * egc], axis=1))
            xs = [x - _mm(n, x) for n, x in zip(ns, xs)]
            next_proj_pieces()
            m = 2
            while m < c:
                ns = [_mm(n, n) for n in ns]
                xs = [x + _mm(n, x) for n, x in zip(ns, xs)]
                next_proj_pieces()
                m *= 2
            for (j, h), x in zip(group_pairs, xs):
                orow = slice(r0 + j * c, r0 + (j + 1) * c)
                sl = slice(h * GDN_DK, (h + 1) * GDN_DK)
                u_ref[orow, sl] = x[:, :GDN_DV]
                w_ref[orow, sl] = x[:, GDN_DV:].astype(w_ref.dtype)
    hcar[...] = h_last
    assert points_done[0] == len(plan) and pieces_done[0] == len(pieces)

    @pl.when(t == pl.num_programs(1) - 1)
    def _():
        lconv1_ref[0] = lbuf[0:SUBLANES, :]
        h1_ref[0] = h_last
        for j in range(3):
            gconv1_ref[0, :, j * hd:(j + 1) * hd] = gbuf[j, 0:SUBLANES, :]


def _pre(x2d, lconv0, h0, gconv0, shared_init, n_streams, t_len, tt, c, reset_first, p):
    nt = t_len // tt
    hd = GDN_HEADS * GDN_DK
    n = n_streams * t_len
    st = (lambda s, t: (0, 0, 0)) if shared_init else (lambda s, t: (s, 0, 0))
    full = lambda a, b: pl.BlockSpec((a, b), lambda s, t: (0, 0))
    row = lambda w: pl.BlockSpec((tt, w), lambda s, t: (s * nt + t, 0))
    return pl.pallas_call(
        functools.partial(_pre_kernel, tt=tt, c=c, reset_first=reset_first),
        grid=(n_streams, nt),
        in_specs=[row(D_MODEL),
                  pl.BlockSpec((tt, D_MODEL), lambda s, t: (jnp.minimum(s * nt + t + 1, n_streams * nt - 1), 0)),
                  full(1, D_MODEL), full(D_MODEL, D_IN_PAD),
                  pl.BlockSpec((1, SUBLANES, D_LRU), st), pl.BlockSpec((1, 1, D_LRU), st),
                  pl.BlockSpec((1, SUBLANES, GDN_QKV), st),
                  full(CONV_W, D_LRU), full(1, D_LRU), full(D_LRU, D_LRU), full(1, D_LRU), full(D_LRU, D_LRU),
                  full(1, D_LRU), full(1, D_LRU), full(1, D_LRU), full(CONV_W, GDN_QKV), full(2, LANES)],
        out_specs=[row(D_LRU), row(hd), row(hd), row(hd), row(hd), row(hd), row(hd),
                   pl.BlockSpec((tt // c, 1, hd), lambda s, t: (s * nt + t, 0, 0)),
                   pl.BlockSpec((1, SUBLANES, D_LRU), lambda s, t: (s, 0, 0)),
                   pl.BlockSpec((1, 1, D_LRU), lambda s, t: (s, 0, 0)),
                   pl.BlockSpec((1, SUBLANES, GDN_QKV), lambda s, t: (s, 0, 0))],
        out_shape=[jax.ShapeDtypeStruct((n, D_LRU), BF16),
                   jax.ShapeDtypeStruct((n, hd), F32),
                   jax.ShapeDtypeStruct((n, hd), BF16),
                   jax.ShapeDtypeStruct((n, hd), BF16),
                   jax.ShapeDtypeStruct((n, hd), BF16),
                   jax.ShapeDtypeStruct((n, hd), BF16),
                   jax.ShapeDtypeStruct((n, hd), F32),
                   jax.ShapeDtypeStruct((n // c, 1, hd), F32),
                   jax.ShapeDtypeStruct((n_streams, SUBLANES, D_LRU), F32),
                   jax.ShapeDtypeStruct((n_streams, 1, D_LRU), F32),
                   jax.ShapeDtypeStruct((n_streams, SUBLANES, GDN_QKV), F32)],
        scratch_shapes=[pltpu.VMEM((tt + SUBLANES, D_LRU), F32),
                        pltpu.VMEM((tt, D_LRU), F32),
                        pltpu.VMEM((tt, D_LRU), F32),
                        pltpu.VMEM((1, D_LRU), F32),
                        pltpu.VMEM((3, tt + SUBLANES, hd), F32),
                        pltpu.VMEM((tt, D_IN_PAD), F32)],
        compiler_params=_cparams(("arbitrary", "arbitrary")),
        name="mixer_pre",
    )(x2d, x2d, p["norm_mix"], p["w_in"], lconv0, h0, gconv0, p["lru_conv_w"], p["lru_conv_b"], p["lru_wa_bd"],
      p["lru_b_a"], p["lru_wx_bd"], p["lru_b_x"], p["lru_lambda"], p["lru_norm"], p["gdn_conv_w"], p["gdn_par"])


def _rec_kernel(u_ref, w_ref, qd_ref, kd_ref, at_ref, z_ref, gl_ref, s0_ref, nrm_ref, o_ref, s1_ref, s_scr,
                *, sb, c, shared_init):
    t = pl.program_id(1)

    @pl.when(t == 0)
    def _():
        for s in range(sb):
            s_scr[s] = s0_ref[0 if shared_init else s]

    chains = [(s, h, slice(h * GDN_DK, (h + 1) * GDN_DK)) for s in range(sb) for h in range(GDN_HEADS)]
    m1s = [jnp.dot(jnp.concatenate([w_ref[s, :, sl], qd_ref[s, :, sl]], axis=0), s_scr[s, h].astype(BF16),
                   preferred_element_type=F32) for s, h, sl in chains]
    vns = [(u_ref[s, :, sl] - m1[:c]).astype(BF16) for (s, h, sl), m1 in zip(chains, m1s)]
    for (s, h, sl), v_new in zip(chains, vns):
        s_scr[s, h] = s_scr[s, h] * gl_ref[s, 0, :, sl] + lax.dot_general(
            kd_ref[s, :, sl], v_new, (((0,), (0,)), ((), ())), preferred_element_type=F32)
    for (s, h, sl), m1, v_new in zip(chains, m1s, vns):
        o = m1[c:] + jnp.dot(at_ref[s, :, h * GDN_DK:h * GDN_DK + c], v_new, preferred_element_type=F32)
        z = z_ref[s, :, sl]
        o_ref[s, :, sl] = (_rms(o, nrm_ref[...]) * (z * _sigmoid(z))).astype(o_ref.dtype)

    @pl.when(t == pl.num_programs(1) - 1)
    def _():
        s1_ref[...] = s_scr[...]


def _rec(u, w, qd, kd, at, z, gl, s0, shared_init, n_streams, t_len, sb, c, p):
    hd = GDN_HEADS * GDN_DK
    nt = t_len // c
    v3 = lambda a: a.reshape(n_streams, t_len, hd)
    blk3 = lambda: pl.BlockSpec((sb, c, hd), lambda g, t: (g, t, 0))
    s_spec = (pl.BlockSpec((1, GDN_HEADS, GDN_DK, GDN_DV), lambda g, t: (0, 0, 0, 0)) if shared_init else
              pl.BlockSpec((sb, GDN_HEADS, GDN_DK, GDN_DV), lambda g, t: (g, 0, 0, 0)))
    out, s1 = pl.pallas_call(
        functools.partial(_rec_kernel, sb=sb, c=c, shared_init=shared_init),
        grid=(n_streams // sb, nt),
        in_specs=[blk3(), blk3(), blk3(), blk3(), blk3(), blk3(),
                  pl.BlockSpec((sb, 1, 1, hd), lambda g, t: (g, t, 0, 0)),
                  s_spec,
                  pl.BlockSpec((1, GDN_DV), lambda g, t: (0, 0))],
        out_specs=[blk3(), pl.BlockSpec((sb, GDN_HEADS, GDN_DK, GDN_DV), lambda g, t: (g, 0, 0, 0))],
        out_shape=[jax.ShapeDtypeStruct((n_streams, t_len, hd), BF16),
                   jax.ShapeDtypeStruct((n_streams, GDN_HEADS, GDN_DK, GDN_DV), F32)],
        scratch_shapes=[pltpu.VMEM((sb, GDN_HEADS, GDN_DK, GDN_DV), F32)],
        compiler_params=_cparams(("arbitrary", "arbitrary")),
        name="gdn_state",
    )(v3(u), v3(w), v3(qd), v3(kd), v3(at), v3(z), gl.reshape(n_streams, nt, 1, hd), s0, p["gdn_norm"])
    return out.reshape(n_streams * t_len, hd), s1


def _post_kernel(lru_p, gdn_p, x_p, lru_s, gdn_s, x_s, wo1_ref, wo2_ref, nrm_ref, rwt_ref, rb_ref,
                 x1_ref, xn2_ref, route_ref, pw_ref, cnt_ref, *, tt, n_p):
    i = pl.program_id(0)
    is_p = i < n_p
    lru = jnp.where(is_p, lru_p[...], lru_s[...])
    gdn = jnp.where(is_p, gdn_p[...], gdn_s[...])
    x = jnp.where(is_p, x_p[...], x_s[...])
    m = (jnp.dot(lru, wo1_ref[...], preferred_element_type=F32)
         + jnp.dot(gdn, wo2_ref[...], preferred_element_type=F32))
    x1 = x + m
    x1_ref[...] = x1
    xn2 = _rms(x1, nrm_ref[...])
    _store_token_tiles(xn2_ref, 0, xn2)
    logits = lax.dot_general(rwt_ref[...], xn2.astype(BF16), (((1,), (1,)), ((), ())),
                             preferred_element_type=F32) + rb_ref[:, 0:1]
    e_id = lax.broadcasted_iota(jnp.int32, (N_EXPERTS, tt), 0)
    vals, idxs = [], []
    for _ in range(TOP_K):
        mx = jnp.max(logits, axis=0, keepdims=True)
        ix = jnp.min(jnp.where(logits == mx, e_id, N_EXPERTS), axis=0, keepdims=True)
        logits = jnp.where(e_id == ix, -jnp.inf, logits)
        vals.append(mx)
        idxs.append(ix)
    es = [jnp.exp(v - vals[0]) for v in vals]
    den = es[0] + es[1] + es[2] + es[3]
    onehot = jnp.zeros((N_EXPERTS, tt), F32)
    for ix in idxs:
        onehot = onehot + (e_id == ix).astype(F32)
    ri = lax.broadcasted_iota(jnp.int32, (tt, tt), 0)
    ci = lax.broadcasted_iota(jnp.int32, (tt, tt), 1)
    earlier = jnp.dot(onehot.astype(BF16), (ri < ci).astype(BF16), preferred_element_type=F32)
    cnt = jnp.sum(onehot, axis=1, keepdims=True)
    ei = lax.broadcasted_iota(jnp.int32, (N_EXPERTS, N_EXPERTS), 0)
    ej = lax.broadcasted_iota(jnp.int32, (N_EXPERTS, N_EXPERTS), 1)
    lower = (ej < ei).astype(BF16)
    cnt_hi = jnp.floor(cnt * (1.0 / 16.0))
    excl = lambda v: jnp.dot(lower, jnp.broadcast_to(v, (N_EXPERTS, LANES)).astype(BF16),
                             preferred_element_type=F32)[:, 0:1]
    off = 16.0 * excl(cnt_hi) + excl(cnt - 16.0 * cnt_hi)
    place = earlier + off
    r8 = lax.broadcasted_iota(jnp.int32, (SUBLANES, tt), 0)
    route = jnp.zeros((SUBLANES, tt), jnp.int32)
    pw = jnp.zeros((SUBLANES, tt), F32)
    for kk in range(TOP_K):
        rank = jnp.sum(jnp.where(e_id == idxs[kk], place, 0.0), axis=0, keepdims=True).astype(jnp.int32)
        route = jnp.where(r8 == kk, idxs[kk], route)
        route = jnp.where(r8 == TOP_K + kk, rank, route)
        pw = jnp.where(r8 == kk, es[kk] / den, pw)
    route_ref[...] = route
    pw_ref[...] = pw
    le = lax.broadcasted_iota(jnp.int32, (N_EXPERTS, LANES), 0)
    ll = lax.broadcasted_iota(jnp.int32, (N_EXPERTS, LANES), 1)
    cnt_ref[0] = jnp.sum(jnp.where(le == ll, cnt, 0.0), axis=0, keepdims=True).astype(jnp.int32)


def _post(lru_p, gdn_p, x_p, lru_s, gdn_s, x_s, tt, p):
    n_p = x_p.shape[0] // tt
    n_s = x_s.shape[0] // tt
    n = x_p.shape[0] + x_s.shape[0]
    prow = lambda w: pl.BlockSpec((tt, w), lambda i: (jnp.minimum(i, n_p - 1), 0))
    srow = lambda w: pl.BlockSpec((tt, w), lambda i: (jnp.maximum(i - n_p, 0), 0))
    row = lambda w: pl.BlockSpec((tt, w), lambda i: (i, 0))
    full = lambda a, b: pl.BlockSpec((a, b), lambda i: (0, 0))
    return pl.pallas_call(
        functools.partial(_post_kernel, tt=tt, n_p=n_p),
        grid=(n_p + n_s,),
        in_specs=[prow(D_LRU), prow(D_LRU), prow(D_MODEL), srow(D_LRU), srow(D_LRU), srow(D_MODEL),
                  full(D_LRU, D_MODEL), full(D_LRU, D_MODEL),
                  full(1, D_MODEL), full(N_EXPERTS, D_MODEL), full(N_EXPERTS, LANES)],
        out_specs=[row(D_MODEL), pl.BlockSpec((tt * TOKEN_ROWS, LANES), lambda i: (i, 0)),
                   pl.BlockSpec((SUBLANES, tt), lambda i: (i, 0)), pl.BlockSpec((SUBLANES, tt), lambda i: (i, 0)),
                   pl.BlockSpec((1, 1, LANES), lambda i: (i, 0, 0))],
        out_shape=[jax.ShapeDtypeStruct((n, D_MODEL), F32), jax.ShapeDtypeStruct((n * TOKEN_ROWS, LANES), F32),
                   jax.ShapeDtypeStruct(((n_p + n_s) * SUBLANES, tt), jnp.int32),
                   jax.ShapeDtypeStruct(((n_p + n_s) * SUBLANES, tt), F32),
                   jax.ShapeDtypeStruct((n_p + n_s, 1, LANES), jnp.int32)],
        compiler_params=_cparams(("arbitrary",)),
        name="outproj_router",
    )(lru_p, gdn_p, x_p, lru_s, gdn_s, x_s, p["w_out1"], p["w_out2"], p["norm_ffn"], p["router_w"], p["router_b"])


def _run_copies(cnt_ref, off_ref, base_ref, vbuf, hbm, sem, to_hbm):
    for e in range(N_EXPERTS):
        cnt = cnt_ref[0, 0, e]
        off = off_ref[0, 0, e]
        base = base_ref[0, 0, e]
        for bit in RUN_BITS:
            done = (cnt & (-2 * bit)) * TOKEN_ROWS

            @pl.when((cnt & bit) != 0)
            def _():
                v = vbuf.at[pl.ds(pl.multiple_of(off + done, TOKEN_ROWS), bit * TOKEN_ROWS), :]
                h = hbm.at[pl.ds(pl.multiple_of(base + done, TOKEN_ROWS), bit * TOKEN_ROWS), :]
                if to_hbm:
                    pltpu.make_async_copy(v, h, sem).start()
                else:
                    pltpu.make_async_copy(h, v, sem).start()


def _wait_run_copies(vbuf, hbm, sem, to_hbm):
    h = hbm.at[pl.ds(0, vbuf.shape[0]), :]
    if to_hbm:
        pltpu.make_async_copy(vbuf, h, sem).wait()
    else:
        pltpu.make_async_copy(h, vbuf, sem).wait()


def _dispatch_kernel(zrow_ref, zcnt_ref, nu_ref, pos_ref, cnt_ref, off_ref, base_ref, x_ref, xs_hbm,
                     dbuf0, dbuf1, zbuf, sems, zsem, *, tt, blk):
    i = pl.program_id(0)
    n = pl.num_programs(0)
    dbufs = (dbuf0, dbuf1)

    def pad_copies(wait):
        for e in range(N_EXPERTS):
            cnt = zcnt_ref[e]
            row = zrow_ref[e]
            for bit in RUN_BITS:
                done = (cnt & (-2 * bit)) * TOKEN_ROWS

                @pl.when((cnt & bit) != 0)
                def _():
                    cp = pltpu.make_async_copy(
                        zbuf.at[pl.ds(0, bit * TOKEN_ROWS), :],
                        xs_hbm.at[pl.ds(pl.multiple_of(row + done, TOKEN_ROWS), bit * TOKEN_ROWS), :], zsem.at[0])
                    if wait:
                        cp.wait()
                    else:
                        cp.start()
        n_blocks = xs_hbm.shape[0] // (blk * TOKEN_ROWS)
        for j in range(N_EXPERTS):
            @pl.when(nu_ref[0] + j < n_blocks)
            def _():
                row = pl.multiple_of((nu_ref[0] + j) * (blk * TOKEN_ROWS), TOKEN_ROWS)
                cp = pltpu.make_async_copy(zbuf, xs_hbm.at[pl.ds(row, blk * TOKEN_ROWS), :], zsem.at[0])
                if wait:
                    cp.wait()
                else:
                    cp.start()

    @pl.when(i == 0)
    def _():
        zbuf[...] = jnp.zeros_like(zbuf)
        pad_copies(False)

    def step(cur):
        dbuf = dbufs[cur]

        @pl.when(i >= 2)
        def _():
            _wait_run_copies(dbuf, xs_hbm, sems.at[cur], True)

        for t in range(tt):
            v = x_ref[t * TOKEN_ROWS:(t + 1) * TOKEN_ROWS, :]
            for kk in range(TOP_K):
                dbuf[pl.ds(pl.multiple_of(pos_ref[0, 0, kk * tt + t], TOKEN_ROWS), TOKEN_ROWS), :] = v
        _run_copies(cnt_ref, off_ref, base_ref, dbuf, xs_hbm, sems.at[cur], True)

        @pl.when(i == n - 1)
        def _():
            _wait_run_copies(dbuf, xs_hbm, sems.at[cur], True)

            @pl.when(i >= 1)
            def _():
                _wait_run_copies(dbufs[1 - cur], xs_hbm, sems.at[1 - cur], True)

            pad_copies(True)

    @pl.when(i % 2 == 0)
    def _():
        step(0)

    @pl.when(i % 2 == 1)
    def _():
        step(1)


def _dispatch(zrow, zcnt, n_used, pos, cnt_t, off_t, base_t, xn2, n_slots, tt, blk):
    n_tiles = cnt_t.shape[0]
    smem = lambda w: pl.BlockSpec((1, 1, w), lambda i, zr, zc, nu: (i, 0, 0), memory_space=pltpu.SMEM)
    rows = TOP_K * tt * TOKEN_ROWS
    grid_spec = pltpu.PrefetchScalarGridSpec(
        num_scalar_prefetch=3,
        grid=(n_tiles,),
        in_specs=[smem(TOP_K * tt), smem(LANES), smem(LANES), smem(LANES),
                  pl.BlockSpec((tt * TOKEN_ROWS, LANES), lambda i, zr, zc, nu: (i, 0))],
        out_specs=pl.BlockSpec(memory_space=pl.ANY),
        scratch_shapes=[pltpu.VMEM((rows, LANES), F32), pltpu.VMEM((rows, LANES), F32),
                        pltpu.VMEM((blk * TOKEN_ROWS, LANES), F32),
                        pltpu.SemaphoreType.DMA((2,)), pltpu.SemaphoreType.DMA((1,))],
    )
    return pl.pallas_call(
        functools.partial(_dispatch_kernel, tt=tt, blk=blk),
        grid_spec=grid_spec,
        out_shape=jax.ShapeDtypeStruct((n_slots * TOKEN_ROWS, LANES), F32),
        compiler_params=_cparams(("arbitrary",)),
        name="moe_dispatch",
    )(zrow, zcnt, n_used, pos, cnt_t, off_t, base_t, xn2)


def _expert_kernel(be_ref, first_ref, nu_ref, nxt_ref, slot_ref, x_ref, wg_hbm, bg_ref, wu_hbm, bu_ref, wd_hbm,
                   bd_ref, y_ref, wst, wbf, wsem, *, blk):
    b = pl.program_id(0)
    w_hbm = (wg_hbm, wu_hbm, wd_hbm)

    def fetch(e, s, wait):
        for j in range(3):
            cp = pltpu.make_async_copy(w_hbm[j].at[pl.ds(e, 1)], wst.at[s, pl.ds(j, 1)], wsem.at[s])
            if wait:
                cp.wait()
            else:
                cp.start()

    @pl.when(b == 0)
    def _():
        fetch(be_ref[0], 0, False)

    for s in range(2):
        @pl.when((first_ref[b] == 1) & (b < nu_ref[0]) & (slot_ref[b] == s))
        def _(s=s):
            fetch(be_ref[b], s, True)
            fetch(nxt_ref[b], 1 - s, False)
            for j in range(3):
                wbf[j] = wst[s, j].astype(BF16)

        @pl.when((b == pl.num_programs(0) - 1) & (slot_ref[nu_ref[0] - 1] == s))
        def _(s=s):
            fetch(be_ref[nu_ref[0] - 1], 1 - s, True)

    @pl.when(b < nu_ref[0])
    def _():
        xb = _load_token_tiles(x_ref, 0, blk).astype(BF16)
        gt = jnp.minimum(jnp.dot(xb, wbf[0], preferred_element_type=F32) + bg_ref[0], SWIGLU_LIMIT)
        up = jnp.clip(jnp.dot(xb, wbf[1], preferred_element_type=F32) + bu_ref[0], -SWIGLU_LIMIT, SWIGLU_LIMIT)
        hid = (up + 1.0) * gt * _sigmoid(SWIGLU_ALPHA * gt)
        _store_token_tiles(y_ref, 0, jnp.dot(hid.astype(BF16), wbf[2], preferred_element_type=F32) + bd_ref[0])

    @pl.when(b >= nu_ref[0])
    def _():
        y_ref[...] = jnp.zeros_like(y_ref)


def _experts(block_expert, first, n_used, nxt, slot, xs, p, blk):
    n_blocks = block_expert.shape[0]
    used = lambda b, nu: jnp.minimum(b, nu[0] - 1)
    wspec = lambda: pl.BlockSpec(memory_space=pl.ANY)
    bspec = lambda: pl.BlockSpec((1, 1, D_MODEL), lambda b, be, fi, nu, nx, sl: (be[used(b, nu)], 0, 0))
    grid_spec = pltpu.PrefetchScalarGridSpec(
        num_scalar_prefetch=5,
        grid=(n_blocks,),
        in_specs=[pl.BlockSpec((blk * TOKEN_ROWS, LANES), lambda b, be, fi, nu, nx, sl: (used(b, nu), 0)),
                  wspec(), bspec(), wspec(), bspec(), wspec(), bspec()],
        out_specs=pl.BlockSpec((blk * TOKEN_ROWS, LANES), lambda b, be, fi, nu, nx, sl: (b, 0)),
        scratch_shapes=[pltpu.VMEM((2, 3, D_MODEL, D_MODEL), F32), pltpu.VMEM((3, D_MODEL, D_MODEL), BF16),
                        pltpu.SemaphoreType.DMA((2,))],
    )
    return pl.pallas_call(
        functools.partial(_expert_kernel, blk=blk),
        grid_spec=grid_spec,
        out_shape=jax.ShapeDtypeStruct(xs.shape, F32),
        compiler_params=_cparams(("arbitrary",)),
        name="experts",
    )(block_expert, first, n_used, nxt, slot, xs, p["exp_w_gate"], p["exp_b_gate"], p["exp_w_up"], p["exp_b_up"],
      p["exp_w_down"], p["exp_b_down"])


def _combine_kernel(pos_ref, pw_ref, cnt_ref, off_ref, base_ref, cntn_ref, offn_ref, basen_ref,
                    ys_hbm, x1_ref, nrm_ref, op_ref, os_ref, ybuf0, ybuf1, mbuf, sems, *, tt, n_p):
    i = pl.program_id(0)
    n = pl.num_programs(0)
    ybufs = (ybuf0, ybuf1)

    @pl.when(i == 0)
    def _():
        _run_copies(cnt_ref, off_ref, base_ref, ybuf0, ys_hbm, sems.at[0], False)

    def step(cur):
        @pl.when(i + 1 < n)
        def _():
            _run_copies(cntn_ref, offn_ref, basen_ref, ybufs[1 - cur], ys_hbm, sems.at[1 - cur], False)

        _wait_run_copies(ybufs[cur], ys_hbm, sems.at[cur], False)
        ybuf = ybufs[cur]
        for t in range(tt):
            acc = None
            for kk in range(TOP_K):
                j = kk * tt + t
                row = ybuf[pl.ds(pl.multiple_of(pos_ref[0, 0, j], TOKEN_ROWS), TOKEN_ROWS), :]
                term = pw_ref[0, 0, j] * row
                acc = term if acc is None else acc + term
            mbuf[t * TOKEN_ROWS:(t + 1) * TOKEN_ROWS, :] = acc

    @pl.when(i % 2 == 0)
    def _():
        step(0)

    @pl.when(i % 2 == 1)
    def _():
        step(1)

    res = _rms(x1_ref[...] + _load_token_tiles(mbuf, 0, tt), nrm_ref[...])

    @pl.when(i < n_p)
    def _():
        op_ref[...] = res

    @pl.when(i >= n_p)
    def _():
        os_ref[...] = res


def _combine(pos, pw, cnt_t, off_t, base_t, ys, x1, final_norm, n_prompt, tt):
    n = x1.shape[0]
    n_p = n_prompt // tt
    n_s = (n - n_prompt) // tt
    nt = n_p + n_s
    smem = lambda w: pl.BlockSpec((1, 1, w), lambda i: (i, 0, 0), memory_space=pltpu.SMEM)
    smem_next = lambda w: pl.BlockSpec((1, 1, w), lambda i: (jnp.minimum(i + 1, nt - 1), 0, 0),
                                       memory_space=pltpu.SMEM)
    rows = TOP_K * tt * TOKEN_ROWS
    return pl.pallas_call(
        functools.partial(_combine_kernel, tt=tt, n_p=n_p),
        grid=(nt,),
        in_specs=[smem(TOP_K * tt), smem(TOP_K * tt), smem(LANES), smem(LANES), smem(LANES),
                  smem_next(LANES), smem_next(LANES), smem_next(LANES),
                  pl.BlockSpec(memory_space=pl.ANY),
                  pl.BlockSpec((tt, D_MODEL), lambda i: (i, 0)),
                  pl.BlockSpec((1, D_MODEL), lambda i: (0, 0))],
        out_specs=[pl.BlockSpec((tt, D_MODEL), lambda i: (jnp.minimum(i, n_p - 1), 0)),
                   pl.BlockSpec((tt, D_MODEL), lambda i: (jnp.maximum(i - n_p, 0), 0))],
        out_shape=[jax.ShapeDtypeStruct((n_prompt, D_MODEL), F32),
                   jax.ShapeDtypeStruct((n - n_prompt, D_MODEL), F32)],
        scratch_shapes=[pltpu.VMEM((rows, LANES), F32), pltpu.VMEM((rows, LANES), F32),
                        pltpu.VMEM((tt * TOKEN_ROWS, LANES), F32), pltpu.SemaphoreType.DMA((2,))],
        compiler_params=_cparams(("arbitrary",)),
        name="moe_combine",
    )(pos, pw, cnt_t, off_t, base_t, cnt_t, off_t, base_t, ys, x1, final_norm)


def _pad_rows(a, rows):
    return jnp.pad(a, ((0, 0), (rows - a.shape[1], 0), (0, 0)))


def _tile(n, pref):
    t = pref
    while n % t:
        t //= 2
    return t


def _mixer_group(x2d, n_streams, t_len, lru_conv0, lru_h0, gdn_conv0, gdn_s0, shared_init, reset_first, chunk, p):
    tt = _tile(t_len, TOKEN_TILE)
    lru_out, u, w, qd, kd, at, z, gl, lru_conv1, lru_h1, gdn_conv1 = _pre(
        x2d, lru_conv0, lru_h0, gdn_conv0, shared_init, n_streams, t_len, tt, chunk, reset_first, p)
    sb = _tile(n_streams, REC_STREAMS)
    gdn_out, gdn_s1 = _rec(u, w, qd, kd, at, z, gl, gdn_s0, shared_init, n_streams, t_len, sb, chunk, p)
    return lru_out, gdn_out, (lru_conv1, lru_h1, gdn_conv1, gdn_s1)


def _routing(cnt_tiles, n_tok, blk):
    cnt_t = cnt_tiles[:, 0, :]
    counts = jnp.sum(cnt_t, axis=0)
    padded = (counts + blk - 1) // blk * blk
    pends = jnp.cumsum(padded)
    pstarts = pends - padded
    off_t = jnp.cumsum(cnt_t, axis=1) - cnt_t
    base_t = pstarts[None, :] + jnp.cumsum(cnt_t, axis=0) - cnt_t
    n_blocks = (n_tok * TOP_K + N_EXPERTS * (blk - 1) + blk - 1) // blk
    starts = jnp.arange(n_blocks, dtype=jnp.int32) * blk
    block_expert = jnp.minimum(jnp.sum((pends[None, :N_EXPERTS] <= starts[:, None]).astype(jnp.int32), axis=1),
                               N_EXPERTS - 1)
    first = jnp.concatenate([jnp.ones((1,), jnp.int32),
                             (block_expert[1:] != block_expert[:-1]).astype(jnp.int32)])
    n_used = (pends[N_EXPERTS - 1] // blk).astype(jnp.int32)[None]
    e_id = jnp.arange(N_EXPERTS, dtype=jnp.int32)
    later = (e_id[None, :] > e_id[:, None]) & (padded[None, :N_EXPERTS] > 0)
    next_e = jnp.min(jnp.where(later, e_id[None, :], N_EXPERTS), axis=1)
    next_e = jnp.where(next_e < N_EXPERTS, next_e, e_id)
    nxt = next_e[block_expert]
    slot = (jnp.cumsum(first) - 1).astype(jnp.int32) % 2
    r3 = lambda a: (a * TOKEN_ROWS).astype(jnp.int32)[:, None, :]
    zrow = ((pstarts + counts) * TOKEN_ROWS).astype(jnp.int32)[:N_EXPERTS]
    zcnt = (padded - counts).astype(jnp.int32)[:N_EXPERTS]
    return (cnt_t.astype(jnp.int32)[:, None, :], r3(off_t), r3(base_t), block_expert, first, n_used, nxt, slot,
            zrow, zcnt, n_blocks * blk)


def kernel(x_prompt, x_sample, state_lru_conv, state_lru_h, state_gdn_conv, state_gdn_S, meta_tokens, norm_mix, w_in, lru_conv_w, lru_conv_b, lru_w_a, lru_b_a, lru_w_x, lru_b_x, lru_lambda, lru_norm, gdn_conv_w, gdn_A_log, gdn_dt_bias, gdn_norm, w_out, norm_ffn, router_w, router_b, exp_w_gate, exp_b_gate, exp_w_up, exp_b_up, exp_w_down, exp_b_down, final_norm):
    bp, tp, _ = x_prompt.shape
    bs, ts, _ = x_sample.shape
    n_meta = meta_tokens.shape[0]
    n_prompt, n_sample = bp * tp, bs * ts
    tt = MOE_TILE
    assert n_prompt % tt == 0 and n_sample % tt == 0 and EXPERT_ROWS <= 2 * MOE_TILE

    def blockdiag(w):
        eye = jnp.eye(LRU_BLOCKS, dtype=w.dtype)
        return jnp.einsum("ncd,nm->ncmd", w, eye).reshape(D_LRU, D_LRU)

    lane_pad = lambda v: jnp.pad(v, (0, LANES - v.shape[0]))[None]
    p = dict(
        norm_mix=norm_mix[0][None],
        w_in=jnp.pad(w_in[0], ((0, 0), (0, D_IN_PAD - D_IN))).astype(BF16),
        lru_conv_w=lru_conv_w[0], lru_conv_b=lru_conv_b[0][None],
        lru_wa_bd=blockdiag(lru_w_a[0]).astype(BF16), lru_b_a=lru_b_a[0][None],
        lru_wx_bd=blockdiag(lru_w_x[0]).astype(BF16), lru_b_x=lru_b_x[0][None],
        lru_lambda=lru_lambda[0][None], lru_norm=lru_norm[0][None],
        gdn_conv_w=gdn_conv_w[0],
        gdn_par=jnp.concatenate([lane_pad(gdn_A_log[0]), lane_pad(gdn_dt_bias[0])], axis=0),
        gdn_norm=gdn_norm[0][None],
        w_out1=w_out[0][:D_LRU].astype(BF16), w_out2=w_out[0][D_LRU:].astype(BF16),
        norm_ffn=norm_ffn[0][None],
        router_w=router_w[0].T.astype(BF16),
        router_b=jnp.broadcast_to(router_b[0][:, None], (N_EXPERTS, LANES)),
        exp_w_gate=exp_w_gate[0], exp_b_gate=exp_b_gate[0][:, None, :],
        exp_w_up=exp_w_up[0], exp_b_up=exp_b_up[0][:, None, :],
        exp_w_down=exp_w_down[0], exp_b_down=exp_b_down[0][:, None, :],
    )

    zc = lambda c: jnp.zeros((1, SUBLANES, c), F32)
    _, _, m_state = _mixer_group(meta_tokens, 1, n_meta, zc(D_LRU), jnp.zeros((1, 1, D_LRU), F32), zc(GDN_QKV),
                                 jnp.zeros((1, GDN_HEADS, GDN_DK, GDN_DV), F32), False, True, n_meta, p)
    xp2 = x_prompt.reshape(n_prompt, D_MODEL)
    xs2 = x_sample.reshape(n_sample, D_MODEL)
    p_lru, p_gdn, p_state = _mixer_group(xp2, bp, tp, m_state[0], m_state[1], m_state[2], m_state[3], True, False,
                                         GDN_CHUNK, p)
    s_lru, s_gdn, s_state = _mixer_group(xs2, bs, ts, _pad_rows(state_lru_conv[0], SUBLANES),
                                         state_lru_h[0][:, None, :], _pad_rows(state_gdn_conv[0], SUBLANES),
                                         state_gdn_S[0], False, False, GDN_CHUNK, p)

    x1, xn2, route, pw, cnt_tiles = _post(p_lru, p_gdn, xp2, s_lru, s_gdn, xs2, tt, p)
    n_tok = n_prompt + n_sample
    (cnt_t, off_t, base_t, block_expert, first, n_used, nxt, slot, zrow, zcnt,
     n_slots) = _routing(cnt_tiles, n_tok, EXPERT_ROWS)
    n_tiles = n_tok // tt
    pos = (route.reshape(n_tiles, SUBLANES, tt)[:, TOP_K:2 * TOP_K] * TOKEN_ROWS).reshape(n_tiles, 1, TOP_K * tt)
    pw_s = pw.reshape(n_tiles, SUBLANES, tt)[:, :TOP_K].reshape(n_tiles, 1, TOP_K * tt)
    xs = _dispatch(zrow, zcnt, n_used, pos, cnt_t, off_t, base_t, xn2, n_slots, tt, EXPERT_ROWS)
    ys = _experts(block_expert, first, n_used, nxt, slot, xs, p, EXPERT_ROWS)
    y_p, y_s = _combine(pos, pw_s, cnt_t, off_t, base_t, ys, x1, final_norm[None], n_prompt, tt)

    def states(st, b):
        return (st[0][:, SUBLANES - 3:, :][None], st[1].reshape(1, b, D_LRU), st[2][:, SUBLANES - 3:, :][None],
                st[3][None])

    return (y_p.reshape(bp, tp, D_MODEL), y_s.reshape(bs, ts, D_MODEL)) + states(p_state, bp) + states(s_state, bs)
```
